```python
import jax, jax.numpy as jnp
from jax import lax
import numpy as np

D_MODEL = 2048
BATCH = 2
SEQ = 4096
DEPTH = 4
DEC_BATCH = 32
DEC_SEQ = 4
PAST_LEN = 16384
PAGE_SIZE = 128

WINDOW = 128
ATT_HEAD_DIM = 64
ATT_Q_HEADS = D_MODEL // ATT_HEAD_DIM
ATT_KV_HEADS = 8
ATT_GROUP = ATT_Q_HEADS // ATT_KV_HEADS
ATT_INNER = ATT_Q_HEADS * ATT_HEAD_DIM
ATT_KV_WIDTH = ATT_KV_HEADS * ATT_HEAD_DIM
ATT_BLOCK = WINDOW
ATT_IN_COLS = ATT_INNER + 2 * ATT_KV_WIDTH + ATT_INNER
M_HEADS = 8
M_DV = D_MODEL // M_HEADS
M_DK = M_DV // 2
M_QK_WIDTH = M_HEADS * M_DK
M_INNER = M_HEADS * M_DV
M_CHUNK = 64
M_IN_COLS = 2 * M_QK_WIDTH + 3 * M_INNER + 2 * M_HEADS
N_ATT_LAYERS = (DEPTH + 1) // 2
N_MLSTM_LAYERS = DEPTH // 2
NORM_EPS = 1e-6

kernel_name = "hybrid_swa_sink_mlstm_sandwich_adaln_step"


def rmsnorm(x, g):
    xf = x.astype(jnp.float32)
    y = xf * lax.rsqrt(jnp.mean(xf * xf, axis=-1, keepdims=True) + NORM_EPS)
    return (y * g.astype(jnp.float32)).astype(x.dtype)


def alibi_slopes():
    s = 2.0 ** (-8.0 * np.arange(1, ATT_Q_HEADS + 1) / ATT_Q_HEADS)
    return jnp.asarray(s, jnp.float32).reshape(ATT_KV_HEADS, ATT_GROUP, 1, 1)


def adaln(c, w, b):
    mod = jax.nn.silu(c) @ w + b
    shift, scale, gate = jnp.split(mod[:, None, :], 3, axis=-1)
    return shift, scale, gate


def sink_attend(q, k, v, bias, sinks):
    s = jnp.einsum('...qhgd,...khd->...hgqk', q, k).astype(jnp.float32) * (ATT_HEAD_DIM ** -0.5) + bias
    sink = sinks.astype(jnp.float32)[..., None, None]
    mx = jnp.maximum(jnp.max(s, axis=-1, keepdims=True), sink)
    p = jnp.exp(s - mx)
    p = p / (jnp.sum(p, axis=-1, keepdims=True) + jnp.exp(sink - mx))
    return jnp.einsum('...hgqk,...khd->...qhgd', p.astype(v.dtype), v)


def attention_branch(h, w_in, w_out, sinks, buf_k=None, buf_v=None):
    bn, t, _ = h.shape
    proj = h @ w_in
    q, k, v, z = jnp.split(proj, [ATT_INNER, ATT_INNER + ATT_KV_WIDTH, ATT_INNER + 2 * ATT_KV_WIDTH], axis=-1)
    q = q.reshape(bn, t, ATT_KV_HEADS, ATT_GROUP, ATT_HEAD_DIM)
    k = k.reshape(bn, t, ATT_KV_HEADS, ATT_HEAD_DIM)
    v = v.reshape(bn, t, ATT_KV_HEADS, ATT_HEAD_DIM)
    slopes = alibi_slopes()
    sk = sinks.reshape(ATT_KV_HEADS, ATT_GROUP)
    if buf_k is None:
        nb = t // ATT_BLOCK
        qb = q.reshape(bn, nb, ATT_BLOCK, ATT_KV_HEADS, ATT_GROUP, ATT_HEAD_DIM)
        kb = k.reshape(bn, nb, ATT_BLOCK, ATT_KV_HEADS, ATT_HEAD_DIM)
        vb = v.reshape(bn, nb, ATT_BLOCK, ATT_KV_HEADS, ATT_HEAD_DIM)
        kk = jnp.concatenate([jnp.concatenate([jnp.zeros_like(kb[:, :1]), kb[:, :-1]], axis=1), kb], axis=2)
        vv = jnp.concatenate([jnp.concatenate([jnp.zeros_like(vb[:, :1]), vb[:, :-1]], axis=1), vb], axis=2)
        qi = jnp.arange(ATT_BLOCK)[:, None]
        ki = jnp.arange(2 * ATT_BLOCK)[None, :]
        d = qi + ATT_BLOCK - ki
        blk = jnp.arange(nb)[:, None, None]
        valid = (d >= 0) & (d < WINDOW) & (blk * ATT_BLOCK - ATT_BLOCK + ki >= 0)
        bias = jnp.where(valid[:, None, None], -slopes * d.astype(jnp.float32), -jnp.inf)
        o = sink_attend(qb, kk, vv, bias, sk).reshape(bn, t, ATT_INNER)
        new_k, new_v = k[:, t - WINDOW:], v[:, t - WINDOW:]
    else:
        w = buf_k.shape[1]
        kk = jnp.concatenate([buf_k.astype(k.dtype), k], axis=1)
        vv = jnp.concatenate([buf_v.astype(v.dtype), v], axis=1)
        d = jnp.arange(t)[:, None] - jnp.arange(-w, t)[None, :]
        valid = (d >= 0) & (d < WINDOW)
        bias = jnp.where(valid, -slopes * d.astype(jnp.float32), -jnp.inf)
        o = sink_attend(q, kk, vv, bias, sk).reshape(bn, t, ATT_INNER)
        new_k, new_v = kk[:, -w:], vv[:, -w:]
    y = (o * jax.nn.silu(z)) @ w_out
    return y, new_k, new_v


def mlstm_chunk_step(carry, xs):
    S, n, m = carry
    q, k, v, li, lf = xs
    L = li.shape[-1]
    b = jnp.cumsum(lf, axis=-1)
    mt = b + jnp.maximum(m[..., None], lax.cummax(li - b, axis=2))
    a = jnp.exp(b + m[..., None] - mt)
    causal = jnp.tril(jnp.ones((L, L), dtype=bool))
    dlog = b[..., :, None] - b[..., None, :] + li[..., None, :] - mt[..., :, None]
    dmat = jnp.exp(jnp.where(causal, dlog, -jnp.inf))
    qs = q * (M_DK ** -0.5)
    qk = jnp.einsum('bhtd,bhsd->bhts', qs, k) * dmat
    num = a[..., None] * jnp.einsum('bhtd,bhde->bhte', qs, S) + jnp.einsum('bhts,bhse->bhte', qk, v)
    den = a * jnp.einsum('bhtd,bhd->bht', qs, n) + jnp.sum(qk, axis=-1)
    hout = num / jnp.maximum(jnp.abs(den), jnp.exp(-mt))[..., None]
    m_new = mt[..., -1]
    wts = jnp.exp(b[..., -1:] - b + li - m_new[..., None])
    a_last = a[..., -1]
    S_new = a_last[..., None, None] * S + jnp.einsum('bhs,bhsd,bhse->bhde', wts, k, v)
    n_new = a_last[..., None] * n + jnp.einsum('bhs,bhsd->bhd', wts, k)
    return (S_new, n_new, m_new), hout


def mlstm_branch(h, w_in, b_if, g_head, w_out, S0, n0, m0):
    bn, t, _ = h.shape
    proj = h @ w_in
    cuts = np.cumsum([M_QK_WIDTH, M_QK_WIDTH, M_INNER, M_INNER, M_INNER]).tolist()
    q, k, v, o, z, gates = jnp.split(proj, cuts, axis=-1)
    gates = gates.astype(jnp.float32) + b_if.astype(jnp.float32)
    li = gates[..., :M_HEADS]
    lf = jax.nn.log_sigmoid(gates[..., M_HEADS:])
    L = M_CHUNK if t % M_CHUNK == 0 else t
    nc = t // L

    def to_chunks(a):
        a = a.astype(jnp.float32).reshape((bn, nc, L, M_HEADS) + a.shape[3:])
        return jnp.moveaxis(a, (1, 3), (0, 2))

    xs = (to_chunks(q.reshape(bn, t, M_HEADS, M_DK)), to_chunks(k.reshape(bn, t, M_HEADS, M_DK)),
          to_chunks(v.reshape(bn, t, M_HEADS, M_DV)), to_chunks(li), to_chunks(lf))
    carry0 = (S0.astype(jnp.float32), n0.astype(jnp.float32), m0.astype(jnp.float32))
    (S, n, m), hs = lax.scan(mlstm_chunk_step, carry0, xs)
    hs = jnp.moveaxis(hs, (0, 2), (1, 3)).reshape(bn, t, M_HEADS, M_DV)
    hn = hs * lax.rsqrt(jnp.mean(hs * hs, axis=-1, keepdims=True) + NORM_EPS) * g_head.astype(jnp.float32)
    hn = hn * jax.nn.sigmoid(o.reshape(bn, t, M_HEADS, M_DV).astype(jnp.float32))
    out = hn.astype(h.dtype).reshape(bn, t, M_INNER) * jax.nn.silu(z)
    y = out @ w_out
    return y, S.astype(S0.dtype), n.astype(n0.dtype), m.astype(m0.dtype)


def setup_inputs(seed: int = 0) -> dict:
    key = jax.random.key(seed)
    ks = jax.random.split(key, 24)
    f32 = jnp.float32
    w_buf = min(WINDOW, PAST_LEN)
    nrm = lambda k, shape, s=1.0: (jax.random.normal(k, shape, f32) * s).astype(f32)
    b_if_i = nrm(ks[18], (N_MLSTM_LAYERS, M_HEADS), 0.1)
    b_if_f = jnp.broadcast_to(jnp.linspace(3.0, 6.0, M_HEADS, dtype=f32), (N_MLSTM_LAYERS, M_HEADS)) \
        + nrm(ks[19], (N_MLSTM_LAYERS, M_HEADS), 0.1)
    return {
        "x_prompt": nrm(ks[0], (BATCH, SEQ, D_MODEL)),
        "x_sample": nrm(ks[1], (DEC_BATCH, DEC_SEQ, D_MODEL)),
        "c_prompt": nrm(ks[2], (BATCH, D_MODEL)),
        "c_sample": nrm(ks[3], (DEC_BATCH, D_MODEL)),
        "cache_k": nrm(ks[4], (N_ATT_LAYERS, DEC_BATCH, w_buf, ATT_KV_HEADS, ATT_HEAD_DIM)),
        "cache_v": nrm(ks[5], (N_ATT_LAYERS, DEC_BATCH, w_buf, ATT_KV_HEADS, ATT_HEAD_DIM)),
        "state_C": nrm(ks[6], (N_MLSTM_LAYERS, DEC_BATCH, M_HEADS, M_DK, M_DV)),
        "state_n": nrm(ks[7], (N_MLSTM_LAYERS, DEC_BATCH, M_HEADS, M_DK)),
        "state_m": nrm(ks[8], (N_MLSTM_LAYERS, DEC_BATCH, M_HEADS)),
        "w_ada": nrm(ks[9], (DEPTH, D_MODEL, 3 * D_MODEL), 0.5 * D_MODEL ** -0.5),
        "b_ada": nrm(ks[10], (DEPTH, 3 * D_MODEL), 0.02),
        "g_pre": 1.0 + nrm(ks[11], (DEPTH, D_MODEL), 0.02),
        "g_post": 1.0 + nrm(ks[12], (DEPTH, D_MODEL), 0.02),
        "w_in_attn": nrm(ks[13], (N_ATT_LAYERS, D_MODEL, ATT_IN_COLS), D_MODEL ** -0.5),
        "sinks": nrm(ks[14], (N_ATT_LAYERS, ATT_Q_HEADS), 0.5),
        "w_out_attn": nrm(ks[15], (N_ATT_LAYERS, ATT_INNER, D_MODEL), ATT_INNER ** -0.5),
        "w_in_mlstm": nrm(ks[16], (N_MLSTM_LAYERS, D_MODEL, M_IN_COLS), D_MODEL ** -0.5),
        "b_if_mlstm": jnp.concatenate([b_if_i, b_if_f], axis=-1),
        "g_head_mlstm": 1.0 + nrm(ks[20], (N_MLSTM_LAYERS, M_HEADS, M_DV), 0.02),
        "w_out_mlstm": nrm(ks[17], (N_MLSTM_LAYERS, M_INNER, D_MODEL), M_INNER ** -0.5),
    }


def reference(x_prompt, x_sample, c_prompt, c_sample, cache_k, cache_v, state_C, state_n, state_m,
              w_ada, b_ada, g_pre, g_post, w_in_attn, sinks, w_out_attn,
              w_in_mlstm, b_if_mlstm, g_head_mlstm, w_out_mlstm):
    xp, xs = x_prompt, x_sample
    bp = x_prompt.shape[0]
    kp_l, vp_l, ks_l, vs_l = [], [], [], []
    Cp_l, np_l, mp_l, Cs_l, ns_l, ms_l = [], [], [], [], [], []
    for l in range(DEPTH):
        sh_p, sc_p, gt_p = adaln(c_prompt, w_ada[l], b_ada[l])
        sh_s, sc_s, gt_s = adaln(c_sample, w_ada[l], b_ada[l])
        hp = rmsnorm(xp, g_pre[l]) * (1.0 + sc_p) + sh_p
        hs = rmsnorm(xs, g_pre[l]) * (1.0 + sc_s) + sh_s
        j = l // 2
        if l % 2 == 0:
            yp, kp, vp = attention_branch(hp, w_in_attn[j], w_out_attn[j], sinks[j])
            ys, ksn, vsn = attention_branch(hs, w_in_attn[j], w_out_attn[j], sinks[j], cache_k[j], cache_v[j])
            kp_l.append(kp); vp_l.append(vp); ks_l.append(ksn); vs_l.append(vsn)
        else:
            S0 = jnp.zeros((bp, M_HEADS, M_DK, M_DV), xp.dtype)
            n0 = jnp.zeros((bp, M_HEADS, M_DK), xp.dtype)
            m0 = jnp.zeros((bp, M_HEADS), xp.dtype)
            yp, Cp, npn, mp = mlstm_branch(hp, w_in_mlstm[j], b_if_mlstm[j], g_head_mlstm[j], w_out_mlstm[j], S0, n0, m0)
            ys, Cs, nsn, ms = mlstm_branch(hs, w_in_mlstm[j], b_if_mlstm[j], g_head_mlstm[j], w_out_mlstm[j],
                                           state_C[j], state_n[j], state_m[j])
            Cp_l.append(Cp); np_l.append(npn); mp_l.append(mp)
            Cs_l.append(Cs); ns_l.append(nsn); ms_l.append(ms)
        xp = xp + gt_p * rmsnorm(yp, g_post[l])
        xs = xs + gt_s * rmsnorm(ys, g_post[l])
    new_k_prompt = jnp.stack(kp_l)
    new_v_prompt = jnp.stack(vp_l)
    new_k_sample = jnp.stack(ks_l)
    new_v_sample = jnp.stack(vs_l)
    new_C_prompt = jnp.stack(Cp_l)
    new_n_prompt = jnp.stack(np_l)
    new_m_prompt = jnp.stack(mp_l)
    new_C_sample = jnp.stack(Cs_l)
    new_n_sample = jnp.stack(ns_l)
    new_m_sample = jnp.stack(ms_l)
    return (xp, xs, new_k_prompt, new_v_prompt, new_k_sample, new_v_sample,
            new_C_prompt, new_n_prompt, new_m_prompt, new_C_sample, new_n_sample, new_m_sample)
```

```python
import functools

import jax
import jax.numpy as jnp
from jax import lax
from jax.experimental import pallas as pl
from jax.experimental.pallas import tpu as pltpu

F32 = jnp.float32
BF16 = jnp.bfloat16

NORM_EPS = 1e-6
WINDOW = 128
ATT_HEAD_DIM = 64
ATT_KV_HEADS = 8
ATT_GROUP = 4
ATT_Q_HEADS = ATT_KV_HEADS * ATT_GROUP
ATT_KV_WIDTH = ATT_KV_HEADS * ATT_HEAD_DIM
M_HEADS = 8
M_DK = 128
M_DV = 256
M_QK_WIDTH = M_HEADS * M_DK
M_CHUNK = 128
SAMPLE_ROWS = 8

V7X_VMEM_LIMIT = 56 * 1024 * 1024


def _params(sem, vmem=V7X_VMEM_LIMIT):
    return pltpu.CompilerParams(dimension_semantics=sem, vmem_limit_bytes=vmem)


def _sigmoid(x):
    return 1.0 / (1.0 + jnp.exp(-x))


def _silu(x):
    return x * _sigmoid(x)


def _log_sigmoid(x):
    return jnp.minimum(x, 0.0) - jnp.log1p(jnp.exp(-jnp.abs(x)))


def _alibi_slope(head):
    return float(2.0 ** (-8.0 * (head + 1) / ATT_Q_HEADS))


def _ada_kernel(c_ref, w_ref, b_ref, o_ref):
    s = _silu(c_ref[...]).astype(BF16)
    o_ref[...] = jnp.dot(s, w_ref[...].astype(BF16), preferred_element_type=F32) + b_ref[...]


def _ada_all_layers(c_all, w_ada, b_ada, tn=1024):
    depth, d, n = w_ada.shape
    r = c_all.shape[0]
    return pl.pallas_call(
        _ada_kernel,
        grid=(depth, n // tn),
        in_specs=[
            pl.BlockSpec((r, d), lambda l, j: (0, 0)),
            pl.BlockSpec((None, d, tn), lambda l, j: (l, 0, j)),
            pl.BlockSpec((None, 1, tn), lambda l, j: (l, 0, j)),
        ],
        out_specs=pl.BlockSpec((None, r, tn), lambda l, j: (l, 0, j)),
        out_shape=jax.ShapeDtypeStruct((depth, r, n), F32),
        compiler_params=_params(("arbitrary", "arbitrary")),
        name="adaln_mod",
    )(c_all, w_ada, b_ada.reshape(depth, 1, n))


def _prenorm_kernel(x_ref, sh_ref, sc_ref, g_ref, h_ref):
    x = x_ref[...]
    y = x * lax.rsqrt(jnp.mean(x * x, axis=-1, keepdims=True) + NORM_EPS) * g_ref[...]
    h_ref[...] = (y * (1.0 + sc_ref[...]) + sh_ref[...]).astype(h_ref.dtype)


def _prenorm(x, shift, scale, g, tm):
    m, d = x.shape
    groups, r, _ = shift.shape
    tiles_per_group = m // tm // groups
    mod_spec = pl.BlockSpec((None, r, d), lambda i: (i // tiles_per_group, 0, 0))
    return pl.pallas_call(
        _prenorm_kernel,
        grid=(m // tm,),
        in_specs=[pl.BlockSpec((tm, d), lambda i: (i, 0)), mod_spec, mod_spec,
                  pl.BlockSpec((1, d), lambda i: (0, 0))],
        out_specs=pl.BlockSpec((tm, d), lambda i: (i, 0)),
        out_shape=jax.ShapeDtypeStruct((m, d), BF16),
        compiler_params=_params(("arbitrary",)),
        name="prenorm_mod",
    )(x, shift, scale, g.reshape(1, d))


def _proj_kernel(h_ref, w_ref, o_ref, wb_ref):
    @pl.when(pl.program_id(1) == 0)
    def _():
        wb_ref[...] = w_ref[...].astype(BF16)

    o_ref[...] = jnp.dot(h_ref[...], wb_ref[...], preferred_element_type=F32).astype(o_ref.dtype)


def _proj(h, w_stack, layer, n_cols, tm, tn):
    m, d = h.shape
    return pl.pallas_call(
        _proj_kernel,
        grid=(n_cols // tn, m // tm),
        in_specs=[pl.BlockSpec((tm, d), lambda j, i: (i, 0)),
                  pl.BlockSpec((None, d, tn), lambda j, i: (layer, 0, j))],
        out_specs=pl.BlockSpec((tm, tn), lambda j, i: (i, j)),
        out_shape=jax.ShapeDtypeStruct((m, n_cols), F32),
        scratch_shapes=[pltpu.VMEM((d, tn), BF16)],
        compiler_params=_params(("arbitrary", "arbitrary")),
        name="in_proj",
    )(h, w_stack)


def _gate_proj_kernel(h_ref, w_ref, o_ref):
    o_ref[...] = jnp.dot(h_ref[...], w_ref[...].astype(BF16), preferred_element_type=F32)


def _gate_proj(h, w_gate, tm):
    m, d = h.shape
    n = w_gate.shape[1]
    return pl.pallas_call(
        _gate_proj_kernel,
        grid=(m // tm,),
        in_specs=[pl.BlockSpec((tm, d), lambda i: (i, 0)), pl.BlockSpec((d, n), lambda i: (0, 0))],
        out_specs=pl.BlockSpec((tm, n), lambda i: (i, 0)),
        out_shape=jax.ShapeDtypeStruct((m, n), F32),
        compiler_params=_params(("arbitrary",)),
        name="mlstm_gate_proj",
    )(h, w_gate)


def _out_kernel(a_ref, w_ref, x_ref, gt_ref, g_ref, o_ref, wb_ref):
    @pl.when(pl.program_id(0) == 0)
    def _():
        wb_ref[...] = w_ref[...].astype(BF16)

    y = jnp.dot(a_ref[...], wb_ref[...], preferred_element_type=F32)
    yn = y * lax.rsqrt(jnp.mean(y * y, axis=-1, keepdims=True) + NORM_EPS) * g_ref[...]
    o_ref[...] = x_ref[...] + gt_ref[...] * yn


def _out_proj(a, w_stack, layer, x, gate, g_post, tm):
    m, d_in = a.shape
    d = x.shape[1]
    groups, r, _ = gate.shape
    tiles_per_group = m // tm // groups
    return pl.pallas_call(
        _out_kernel,
        grid=(m // tm,),
        in_specs=[pl.BlockSpec((tm, d_in), lambda i: (i, 0)),
                  pl.BlockSpec((None, d_in, d), lambda i: (layer, 0, 0), pipeline_mode=pl.Buffered(1)),
                  pl.BlockSpec((tm, d), lambda i: (i, 0)),
                  pl.BlockSpec((None, r, d), lambda i: (i // tiles_per_group, 0, 0)),
                  pl.BlockSpec((1, d), lambda i: (0, 0))],
        out_specs=pl.BlockSpec((tm, d), lambda i: (i, 0)),
        out_shape=jax.ShapeDtypeStruct((m, d), F32),
        scratch_shapes=[pltpu.VMEM((d_in, d), BF16)],
        compiler_params=_params(("arbitrary",)),
        name="out_proj_postnorm",
    )(a, w_stack, x, gate, g_post.reshape(1, d))


def _group_select(group_col, values):
    out = values[ATT_GROUP - 1]
    for g in range(ATT_GROUP - 2, -1, -1):
        out = jnp.where(group_col == g, values[g], out)
    return out


def _attn_prompt_kernel(sink_ref, q_ref, kc_ref, kp_ref, vc_ref, vp_ref, z0_ref, z1_ref, o_ref):
    blk = WINDOW
    hd = ATT_HEAD_DIM
    i = pl.program_id(1)
    kcat = jnp.concatenate([kp_ref[...], kc_ref[...]], axis=0).astype(BF16)
    vcat = jnp.concatenate([vp_ref[...], vc_ref[...]], axis=0).astype(BF16)
    rows = ATT_GROUP * blk
    row = lax.broadcasted_iota(jnp.int32, (rows, 2 * blk), 0)
    col = lax.broadcasted_iota(jnp.int32, (rows, 2 * blk), 1)
    dist = (row % blk) + blk - col
    valid = (dist >= 0) & (dist < WINDOW) & ((col >= blk) | (i > 0))
    dist_f = dist.astype(F32)
    group_col = lax.broadcasted_iota(jnp.int32, (rows, 1), 0) // blk
    half = z0_ref.shape[1]
    for h in range(ATT_KV_HEADS):
        heads = [ATT_GROUP * h + g for g in range(ATT_GROUP)]
        qs = jnp.concatenate([q_ref[:, j * hd:(j + 1) * hd] for j in heads], axis=0).astype(BF16)
        s = lax.dot_general(qs, kcat[:, h * hd:(h + 1) * hd], (((1,), (1,)), ((), ())),
                            preferred_element_type=F32) * (hd ** -0.5)
        slope = _group_select(group_col, [_alibi_slope(j) for j in heads])
        sink = _group_select(group_col, [sink_ref[j] for j in heads])
        s = jnp.where(valid, s - slope * dist_f, -jnp.inf)
        mx = jnp.maximum(jnp.max(s, axis=-1, keepdims=True), sink)
        p = jnp.exp(s - mx)
        p = p / (jnp.sum(p, axis=-1, keepdims=True) + jnp.exp(sink - mx))
        o = jnp.dot(p.astype(BF16), vcat[:, h * hd:(h + 1) * hd], preferred_element_type=F32)
        for g, j in enumerate(heads):
            c0 = j * hd
            z_ref, zc = (z0_ref, c0) if c0 < half else (z1_ref, c0 - half)
            z = z_ref[:, zc:zc + hd]
            o_ref[:, c0:c0 + hd] = (o[g * blk:(g + 1) * blk] * _silu(z)).astype(o_ref.dtype)


def _attn_prompt(proj, sinks, batch, seq):
    blk = WINDOW
    nb = seq // blk
    inner = ATT_Q_HEADS * ATT_HEAD_DIM
    kvw = ATT_KV_WIDTH
    k_col = inner // kvw
    v_col = k_col + 1
    half = inner // 2
    z_col = (inner + 2 * kvw) // half
    cur = lambda b, i: b * nb + i
    prev = lambda b, i: b * nb + jnp.maximum(i - 1, 0)
    return pl.pallas_call(
        _attn_prompt_kernel,
        grid=(batch, nb),
        in_specs=[pl.BlockSpec(memory_space=pltpu.SMEM),
                  pl.BlockSpec((blk, inner), lambda b, i: (cur(b, i), 0)),
                  pl.BlockSpec((blk, kvw), lambda b, i: (cur(b, i), k_col)),
                  pl.BlockSpec((blk, kvw), lambda b, i: (prev(b, i), k_col)),
                  pl.BlockSpec((blk, kvw), lambda b, i: (cur(b, i), v_col)),
                  pl.BlockSpec((blk, kvw), lambda b, i: (prev(b, i), v_col)),
                  pl.BlockSpec((blk, half), lambda b, i: (cur(b, i), z_col)),
                  pl.BlockSpec((blk, half), lambda b, i: (cur(b, i), z_col + 1))],
        out_specs=pl.BlockSpec((blk, inner), lambda b, i: (cur(b, i), 0)),
        out_shape=jax.ShapeDtypeStruct((batch * seq, inner), BF16),
        compiler_params=_params(("arbitrary", "arbitrary")),
        name="attn_prompt",
    )(sinks, proj, proj, proj, proj, proj, proj, proj)


def _attn_sample_kernel(sink_ref, q_ref, kn_ref, vn_ref, z0_ref, z1_ref, kc_ref, vc_ref, o_ref, *, n_new):
    hd = ATT_HEAD_DIM
    tp = q_ref.shape[0]
    w = kc_ref.shape[0]
    rows = ATT_GROUP * tp
    t_c = lax.broadcasted_iota(jnp.int32, (rows, w), 0) % tp
    c_c = lax.broadcasted_iota(jnp.int32, (rows, w), 1)
    dist_c = t_c + w - c_c
    valid_c = (dist_c >= 0) & (dist_c < WINDOW)
    dist_cf = dist_c.astype(F32)
    t_col = lax.broadcasted_iota(jnp.int32, (rows, 1), 0) % tp
    group_col = lax.broadcasted_iota(jnp.int32, (rows, 1), 0) // tp
    half = z0_ref.shape[1]
    kc = kc_ref[...].astype(BF16)
    vc = vc_ref[...].astype(BF16)
    for h in range(ATT_KV_HEADS):
        heads = [ATT_GROUP * h + g for g in range(ATT_GROUP)]
        qf = jnp.concatenate([q_ref[:, j * hd:(j + 1) * hd] for j in heads], axis=0)
        slope = _group_select(group_col, [_alibi_slope(j) for j in heads])
        sink = _group_select(group_col, [sink_ref[j] for j in heads])
        s_c = lax.dot_general(qf.astype(BF16), kc[:, h * hd:(h + 1) * hd], (((1,), (1,)), ((), ())),
                              preferred_element_type=F32) * (hd ** -0.5)
        s_c = jnp.where(valid_c, s_c - slope * dist_cf, -jnp.inf)
        mx = jnp.maximum(jnp.max(s_c, axis=-1, keepdims=True), sink)
        s_n = []
        for c in range(n_new):
            kn = kn_ref[c:c + 1, h * hd:(h + 1) * hd]
            d_n = (t_col - c)
            sc = jnp.sum(qf * kn, axis=-1, keepdims=True) * (hd ** -0.5) - slope * d_n.astype(F32)
            sc = jnp.where((d_n >= 0) & (d_n < WINDOW), sc, -jnp.inf)
            s_n.append(sc)
            mx = jnp.maximum(mx, sc)
        p_c = jnp.exp(s_c - mx)
        p_n = [jnp.exp(sc - mx) for sc in s_n]
        den = jnp.sum(p_c, axis=-1, keepdims=True) + jnp.exp(sink - mx)
        for pn in p_n:
            den = den + pn
        o = jnp.dot((p_c / den).astype(BF16), vc[:, h * hd:(h + 1) * hd], preferred_element_type=F32)
        for c in range(n_new):
            o = o + (p_n[c] / den) * vn_ref[c:c + 1, h * hd:(h + 1) * hd]
        for g, j in enumerate(heads):
            c0 = j * hd
            z_ref, zc = (z0_ref, c0) if c0 < half else (z1_ref, c0 - half)
            z = z_ref[:, zc:zc + hd]
            o_ref[:, c0:c0 + hd] = (o[g * tp:(g + 1) * tp] * _silu(z)).astype(o_ref.dtype)


def _attn_sample(proj3, sinks, cache_k4, cache_v4, layer, n_new):
    nbatch, tp, _ = proj3.shape
    w = cache_k4.shape[2]
    inner = ATT_Q_HEADS * ATT_HEAD_DIM
    kvw = ATT_KV_WIDTH
    k_col = inner // kvw
    half = inner // 2
    z_col = (inner + 2 * kvw) // half
    cache_spec = pl.BlockSpec((None, None, w, kvw), lambda b: (layer, b, 0, 0))
    return pl.pallas_call(
        functools.partial(_attn_sample_kernel, n_new=n_new),
        grid=(nbatch,),
        in_specs=[pl.BlockSpec(memory_space=pltpu.SMEM),
                  pl.BlockSpec((None, tp, inner), lambda b: (b, 0, 0)),
                  pl.BlockSpec((None, tp, kvw), lambda b: (b, 0, k_col)),
                  pl.BlockSpec((None, tp, kvw), lambda b: (b, 0, k_col + 1)),
                  pl.BlockSpec((None, tp, half), lambda b: (b, 0, z_col)),
                  pl.BlockSpec((None, tp, half), lambda b: (b, 0, z_col + 1)),
                  cache_spec, cache_spec],
        out_specs=pl.BlockSpec((None, tp, inner), lambda b: (b, 0, 0)),
        out_shape=jax.ShapeDtypeStruct((nbatch, tp, inner), BF16),
        compiler_params=_params(("arbitrary",)),
        name="attn_sample",
    )(sinks, proj3, proj3, proj3, proj3, proj3, cache_k4, cache_v4)


def _lane_scan(x, op, lane):
    n = x.shape[-1]
    shift = 1
    while shift < n:
        x = jnp.where(lane >= shift, op(x, pltpu.roll(x, shift, axis=x.ndim - 1)), x)
        shift *= 2
    return x


def _gate_prep_kernel(bias_ref, g_ref, a_ref, em_ref, wt_ref, u_ref, w_ref, mn_ref, *, chunks_per_seq):
    rows, length = g_ref.shape[2], g_ref.shape[3]
    lane = lax.broadcasted_iota(jnp.int32, (rows, length), 1)
    chunk = lax.broadcasted_iota(jnp.int32, (rows, length), 0) % chunks_per_seq
    for h in range(M_HEADS):
        li = g_ref[0, h] + bias_ref[h]
        lf = _log_sigmoid(g_ref[1, h] + bias_ref[M_HEADS + h])
        bcum = _lane_scan(lf, jnp.add, lane)
        w = li - bcum
        cmax = _lane_scan(w, jnp.maximum, lane)
        e = jnp.broadcast_to(bcum[:, length - 1:length], (rows, length))
        y = e + jnp.broadcast_to(cmax[:, length - 1:length], (rows, length))
        shift = 1
        while shift < chunks_per_seq:
            e_prev = pltpu.roll(e, shift, axis=0)
            y_prev = pltpu.roll(y, shift, axis=0)
            take = chunk >= shift
            y = jnp.where(take, jnp.maximum(y_prev + e, y), y)
            e = jnp.where(take, e_prev + e, e)
            shift *= 2
        m_incl = jnp.maximum(e, y)
        m_prev = jnp.where(chunk >= 1, pltpu.roll(m_incl, 1, axis=0), 0.0)
        u = -jnp.maximum(m_prev, cmax)
        u_last = jnp.broadcast_to(u[:, length - 1:length], (rows, length))
        a_ref[h] = jnp.exp(m_prev + u)
        em_ref[h] = jnp.exp(u - bcum)
        wt_ref[h] = jnp.exp(w + u_last)
        u_ref[h] = u
        w_ref[h] = w
        mn_ref[h] = m_incl


def _gate_prep(gates_t, b_if, chunks_per_seq):
    shape = gates_t.shape[1:]
    out = jax.ShapeDtypeStruct(shape, F32)
    return pl.pallas_call(
        functools.partial(_gate_prep_kernel, chunks_per_seq=chunks_per_seq),
        in_specs=[pl.BlockSpec(memory_space=pltpu.SMEM), pl.BlockSpec(memory_space=pltpu.VMEM)],
        out_specs=[pl.BlockSpec(memory_space=pltpu.VMEM)] * 6,
        out_shape=[out] * 6,
        name="mlstm_gate_prep",
    )(b_if, gates_t)


def _head_norm_gate(hout, g_row, o, z):
    hn = hout * lax.rsqrt(jnp.mean(hout * hout, axis=-1, keepdims=True) + NORM_EPS) * g_row
    return hn * _sigmoid(o) * _silu(z)


def _mlstm_prompt_kernel(q_ref, k_ref, v_ref, o_ref, z_ref, col_ref, wrow_ref, gh_ref,
                         out_ref, s_out_ref, n_out_ref, s_ref, n_ref):
    c = pl.program_id(1)
    length = q_ref.shape[0]

    @pl.when(c == 0)
    def _():
        s_ref[...] = jnp.zeros_like(s_ref)
        n_ref[...] = jnp.zeros_like(n_ref)

    t_idx = lax.broadcasted_iota(jnp.int32, (length, length), 0)
    s_idx = lax.broadcasted_iota(jnp.int32, (length, length), 1)
    causal = s_idx <= t_idx
    col = col_ref[...]
    for h in range(M_HEADS):
        a_col = col[:, h:h + 1]
        e_col = col[:, M_HEADS + h:M_HEADS + h + 1]
        wt_col = col[:, 2 * M_HEADS + h:2 * M_HEADS + h + 1]
        u_col = col[:, 3 * M_HEADS + h:3 * M_HEADS + h + 1]
        a_last = a_col[length - 1:length, :]
        qf = q_ref[:, h * M_DK:(h + 1) * M_DK] * (M_DK ** -0.5)
        qb = qf.astype(BF16)
        kf = k_ref[:, h * M_DK:(h + 1) * M_DK]
        vb = v_ref[:, h * M_DV:(h + 1) * M_DV].astype(BF16)
        dmat = jnp.where(causal, jnp.exp(u_col + wrow_ref[h:h + 1, :]), 0.0)
        qk = lax.dot_general(qb, kf.astype(BF16), (((1,), (1,)), ((), ())), preferred_element_type=F32) * dmat
        s_old = s_ref[h]
        n_old = n_ref[h:h + 1, :]
        num = a_col * jnp.dot(qb, s_old.astype(BF16), preferred_element_type=F32) \
            + jnp.dot(qk.astype(BF16), vb, preferred_element_type=F32)
        den = a_col * jnp.sum(qf * n_old, axis=-1, keepdims=True) + jnp.sum(qk, axis=-1, keepdims=True)
        hout = num / jnp.maximum(jnp.abs(den), e_col)
        kw = kf * wt_col
        s_ref[h] = a_last * s_old + jnp.dot(kw.T.astype(BF16), vb, preferred_element_type=F32)
        n_ref[h:h + 1, :] = a_last * n_old + jnp.sum(kw, axis=0, keepdims=True)
        sl = slice(h * M_DV, (h + 1) * M_DV)
        out_ref[:, sl] = _head_norm_gate(hout, gh_ref[:, sl], o_ref[:, sl], z_ref[:, sl]).astype(out_ref.dtype)

    @pl.when(c == pl.num_programs(1) - 1)
    def _():
        s_out_ref[...] = s_ref[...]
        n_out_ref[...] = n_ref[...]


def _mlstm_prompt(proj, col, wrow, g_head, batch, seq):
    length = M_CHUNK
    nc = seq // length
    inner = M_HEADS * M_DV
    qkw = M_QK_WIDTH
    v_col = 2 * qkw // inner
    row = lambda b, c: b * nc + c
    return pl.pallas_call(
        _mlstm_prompt_kernel,
        grid=(batch, nc),
        in_specs=[pl.BlockSpec((length, qkw), lambda b, c: (row(b, c), 0)),
                  pl.BlockSpec((length, qkw), lambda b, c: (row(b, c), 1)),
                  pl.BlockSpec((length, inner), lambda b, c: (row(b, c), v_col)),
                  pl.BlockSpec((length, inner), lambda b, c: (row(b, c), v_col + 1)),
                  pl.BlockSpec((length, inner), lambda b, c: (row(b, c), v_col + 2)),
                  pl.BlockSpec((length, 4 * M_HEADS), lambda b, c: (row(b, c), 0)),
                  pl.BlockSpec((None, M_HEADS, length), lambda b, c: (row(b, c), 0, 0)),
                  pl.BlockSpec((1, inner), lambda b, c: (0, 0))],
        out_specs=[pl.BlockSpec((length, inner), lambda b, c: (row(b, c), 0)),
                   pl.BlockSpec((None, M_HEADS, M_DK, M_DV), lambda b, c: (b, 0, 0, 0)),
                   pl.BlockSpec((None, M_HEADS, M_DK), lambda b, c: (b, 0, 0))],
        out_shape=[jax.ShapeDtypeStruct((batch * seq, inner), BF16),
                   jax.ShapeDtypeStruct((batch, M_HEADS, M_DK, M_DV), F32),
                   jax.ShapeDtypeStruct((batch, M_HEADS, M_DK), F32)],
        scratch_shapes=[pltpu.VMEM((M_HEADS, M_DK, M_DV), F32), pltpu.VMEM((M_HEADS, M_DK), F32)],
        compiler_params=_params(("arbitrary", "arbitrary")),
        name="mlstm_prompt_scan",
    )(proj, proj, proj, proj, proj, col, wrow, g_head.reshape(1, inner))


def _prefix_scan(x, op, axis, n):
    idx = lax.broadcasted_iota(jnp.int32, x.shape, axis)
    take = (lambda t: x[t:t + 1, :]) if axis == 0 else (lambda t: x[:, t:t + 1])
    run = take(0)
    out = jnp.broadcast_to(run, x.shape)
    for t in range(1, n):
        run = op(run, take(t))
        out = jnp.where(idx >= t, run, out)
    return out


def _mlstm_sample_kernel(q_ref, k_ref, v_ref, o_ref, z_ref, kt_ref, g_ref, gt_ref, brow_ref, bcol_ref,
                         mrow_ref, gh_ref, s_in_ref, n_in_ref,
                         out_ref, s_out_ref, n_out_ref, m_out_ref, *, n_new):
    tp = q_ref.shape[0]
    last = n_new - 1
    hs = M_HEADS
    g = g_ref[...] + brow_ref[...]
    bcum_c = _prefix_scan(_log_sigmoid(g[:, hs:]), jnp.add, 0, n_new)
    w_c = g[:, :hs] - bcum_c
    m_prev_c = mrow_ref[...]
    u_c = -jnp.maximum(m_prev_c, _prefix_scan(w_c, jnp.maximum, 0, n_new))
    a_c = jnp.exp(m_prev_c + u_c)
    e_c = jnp.exp(u_c - bcum_c)
    real_c = lax.broadcasted_iota(jnp.int32, (tp, hs), 0) < n_new
    wt_c = jnp.where(real_c, jnp.exp(w_c + u_c[last:last + 1, :]), 0.0)
    m_out_ref[...] = bcum_c[last:last + 1, :] - u_c[last:last + 1, :]
    gt = gt_ref[...] + bcol_ref[...]
    w_r = gt[:hs, :] - _prefix_scan(_log_sigmoid(gt[hs:, :]), jnp.add, 1, n_new)
    t_idx = lax.broadcasted_iota(jnp.int32, (tp, tp), 0)
    s_idx = lax.broadcasted_iota(jnp.int32, (tp, tp), 1)
    causal = s_idx <= t_idx
    for h in range(hs):
        a_col = a_c[:, h:h + 1]
        a_last = a_col[last:last + 1, :]
        qf = q_ref[:, h * M_DK:(h + 1) * M_DK] * (M_DK ** -0.5)
        dmat = jnp.where(causal, jnp.exp(u_c[:, h:h + 1] + w_r[h:h + 1, :]), 0.0)
        s_old = s_in_ref[h]
        n_old = n_in_ref[h:h + 1, :]
        num = a_col * jnp.dot(qf.astype(BF16), s_old.astype(BF16), preferred_element_type=F32)
        den = a_col * jnp.sum(qf * n_old, axis=-1, keepdims=True)
        s_new = a_last * s_old
        n_new_row = a_last * n_old
        for s in range(n_new):
            k_row = k_ref[s:s + 1, h * M_DK:(h + 1) * M_DK]
            v_row = v_ref[s:s + 1, h * M_DV:(h + 1) * M_DV]
            qk = jnp.sum(qf * k_row, axis=-1, keepdims=True) * dmat[:, s:s + 1]
            num = num + qk * v_row
            den = den + qk
            wt = wt_c[s:s + 1, h:h + 1]
            s_new = s_new + (wt * kt_ref[h * M_DK:(h + 1) * M_DK, s:s + 1]) * v_row
            n_new_row = n_new_row + wt * k_row
        hout = num / jnp.maximum(jnp.abs(den), e_c[:, h:h + 1])
        s_out_ref[h] = s_new
        n_out_ref[h:h + 1, :] = n_new_row
        sl = slice(h * M_DV, (h + 1) * M_DV)
        out_ref[:, sl] = _head_norm_gate(hout, gh_ref[:, sl], o_ref[:, sl], z_ref[:, sl]).astype(out_ref.dtype)


def _mlstm_sample(proj3, k_t, gates3, gates3_t, b_if, state_c, state_n, state_m, g_head, layer, n_new):
    nbatch, tp, _ = proj3.shape
    hs = M_HEADS
    inner = hs * M_DV
    qkw = M_QK_WIDTH
    v_col = 2 * qkw // inner
    m_row = state_m[layer].reshape(nbatch, 1, hs)
    return pl.pallas_call(
        functools.partial(_mlstm_sample_kernel, n_new=n_new),
        grid=(nbatch,),
        in_specs=[pl.BlockSpec((None, tp, qkw), lambda b: (b, 0, 0)),
                  pl.BlockSpec((None, tp, qkw), lambda b: (b, 0, 1)),
                  pl.BlockSpec((None, tp, inner), lambda b: (b, 0, v_col)),
                  pl.BlockSpec((None, tp, inner), lambda b: (b, 0, v_col + 1)),
                  pl.BlockSpec((None, tp, inner), lambda b: (b, 0, v_col + 2)),
                  pl.BlockSpec((None, qkw, tp), lambda b: (b, 0, 0)),
                  pl.BlockSpec((None, tp, 2 * hs), lambda b: (b, 0, 0)),
                  pl.BlockSpec((None, 2 * hs, tp), lambda b: (b, 0, 0)),
                  pl.BlockSpec((1, 2 * hs), lambda b: (0, 0)),
                  pl.BlockSpec((2 * hs, 1), lambda b: (0, 0)),
                  pl.BlockSpec((None, 1, hs), lambda b: (b, 0, 0)),
                  pl.BlockSpec((1, inner), lambda b: (0, 0)),
                  pl.BlockSpec((None, None, hs, M_DK, M_DV), lambda b: (layer, b, 0, 0, 0)),
                  pl.BlockSpec((None, None, hs, M_DK), lambda b: (layer, b, 0, 0))],
        out_specs=[pl.BlockSpec((None, tp, inner), lambda b: (b, 0, 0)),
                   pl.BlockSpec((None, hs, M_DK, M_DV), lambda b: (b, 0, 0, 0)),
                   pl.BlockSpec((None, hs, M_DK), lambda b: (b, 0, 0)),
                   pl.BlockSpec((None, 1, hs), lambda b: (b, 0, 0))],
        out_shape=[jax.ShapeDtypeStruct((nbatch, tp, inner), BF16),
                   jax.ShapeDtypeStruct((nbatch, hs, M_DK, M_DV), F32),
                   jax.ShapeDtypeStruct((nbatch, hs, M_DK), F32),
                   jax.ShapeDtypeStruct((nbatch, 1, hs), F32)],
        compiler_params=_params(("arbitrary",)),
        name="mlstm_sample_step",
    )(proj3, proj3, proj3, proj3, proj3, k_t, gates3, gates3_t, b_if.reshape(1, 2 * hs), b_if.reshape(2 * hs, 1),
      m_row, g_head.reshape(1, inner), state_c, state_n)


def kernel(x_prompt, x_sample, c_prompt, c_sample, cache_k, cache_v, state_C, state_n, state_m, w_ada, b_ada,
           g_pre, g_post, w_in_attn, sinks, w_out_attn, w_in_mlstm, b_if_mlstm, g_head_mlstm, w_out_mlstm):
    batch, seq, d = x_prompt.shape
    dec_batch, dec_seq, _ = x_sample.shape
    depth = w_ada.shape[0]
    tp = SAMPLE_ROWS
    w_buf = cache_k.shape[2]
    rows_p = batch * seq
    rows_s = dec_batch * tp
    tm = 1024
    att_inner = ATT_Q_HEADS * ATT_HEAD_DIM
    att_cols = 2 * att_inner + 2 * ATT_KV_WIDTH
    m_cols = 2 * M_QK_WIDTH + 3 * M_HEADS * M_DV

    c_rows = batch + dec_batch
    c_pad = -c_rows % 8
    c_all = jnp.concatenate([c_prompt, c_sample, jnp.zeros((c_pad, d), F32)], axis=0)
    mod = _ada_all_layers(c_all, w_ada, b_ada)

    xp = x_prompt.reshape(rows_p, d)
    xs = jnp.pad(x_sample, ((0, 0), (0, tp - dec_seq), (0, 0))).reshape(rows_s, d)
    cache_k4 = cache_k.reshape(cache_k.shape[:3] + (ATT_KV_WIDTH,))
    cache_v4 = cache_v.reshape(cache_v.shape[:3] + (ATT_KV_WIDTH,))

    kp_l, vp_l, ks_l, vs_l = [], [], [], []
    cp_l, np_l, mp_l, cs_l, ns_l, ms_l = [], [], [], [], [], []
    for l in range(depth):
        j = l // 2
        mod_p = mod[l, :batch].reshape(batch, 1, 3 * d)
        mod_s = jnp.repeat(mod[l, batch:c_rows], tp, axis=0).reshape(1, rows_s, 3 * d)
        sh_p, sc_p, gt_p = mod_p[..., :d], mod_p[..., d:2 * d], mod_p[..., 2 * d:]
        sh_s, sc_s, gt_s = mod_s[..., :d], mod_s[..., d:2 * d], mod_s[..., 2 * d:]
        hp = _prenorm(xp, sh_p, sc_p, g_pre[l], tm)
        hs = _prenorm(xs, sh_s, sc_s, g_pre[l], rows_s)
        if l % 2 == 0:
            pp = _proj(hp, w_in_attn, j, att_cols, tm, 1024)
            ps = _proj(hs, w_in_attn, j, att_cols, rows_s, 1024)
            ap = _attn_prompt(pp, sinks[j], batch, seq)
            ps3 = ps.reshape(dec_batch, tp, att_cols)
            a_s = _attn_sample(ps3, sinks[j], cache_k4, cache_v4, j, dec_seq).reshape(rows_s, att_inner)
            k0, v0 = att_inner, att_inner + ATT_KV_WIDTH
            pp3 = pp.reshape(batch, seq, att_cols)
            kv_shape = (ATT_KV_HEADS, ATT_HEAD_DIM)
            kp_l.append(pp3[:, seq - WINDOW:, k0:v0].reshape((batch, WINDOW) + kv_shape))
            vp_l.append(pp3[:, seq - WINDOW:, v0:v0 + ATT_KV_WIDTH].reshape((batch, WINDOW) + kv_shape))
            k_new = ps3[:, :dec_seq, k0:v0].reshape((dec_batch, dec_seq) + kv_shape)
            v_new = ps3[:, :dec_seq, v0:v0 + ATT_KV_WIDTH].reshape((dec_batch, dec_seq) + kv_shape)
            ks_l.append(jnp.concatenate([cache_k[j], k_new], axis=1)[:, -w_buf:])
            vs_l.append(jnp.concatenate([cache_v[j], v_new], axis=1)[:, -w_buf:])
            w_out = w_out_attn
        else:
            w_gate = w_in_mlstm[j][:, m_cols:]
            pp = _proj(hp, w_in_mlstm, j, m_cols, tm, 1024)
            ps = _proj(hs, w_in_mlstm, j, m_cols, rows_s, 1024)
            gp = _gate_proj(hp, w_gate, tm)
            gs = _gate_proj(hs, w_gate, rows_s)
            nc = seq // M_CHUNK
            gates_t = gp.T.reshape(2, M_HEADS, batch * nc, M_CHUNK)
            a_q, em_q, wt_q, u_q, w_q, mn_q = _gate_prep(gates_t, b_if_mlstm[j], nc)
            col = jnp.stack([a_q, em_q, wt_q, u_q]).transpose(2, 3, 0, 1).reshape(rows_p, 4 * M_HEADS)
            wrow = w_q.transpose(1, 0, 2)
            ap, c_new, n_new = _mlstm_prompt(pp, col, wrow, g_head_mlstm[j], batch, seq)
            cp_l.append(c_new)
            np_l.append(n_new)
            mp_l.append(mn_q.reshape(M_HEADS, batch, nc, M_CHUNK)[:, :, nc - 1, 0].T)
            ps3 = ps.reshape(dec_batch, tp, m_cols)
            k_t = ps3[:, :, M_QK_WIDTH:2 * M_QK_WIDTH].transpose(0, 2, 1)
            gs3 = gs.reshape(dec_batch, tp, 2 * M_HEADS)
            a_s, c_new, n_new, m_new = _mlstm_sample(ps3, k_t, gs3, gs3.transpose(0, 2, 1), b_if_mlstm[j],
                                                     state_C, state_n, state_m, g_head_mlstm[j], j, dec_seq)
            a_s = a_s.reshape(rows_s, M_HEADS * M_DV)
            cs_l.append(c_new)
            ns_l.append(n_new)
            ms_l.append(m_new.reshape(dec_batch, M_HEADS))
            w_out = w_out_mlstm
        xp = _out_proj(ap, w_out, j, xp, gt_p, g_post[l], 512)
        xs = _out_proj(a_s, w_out, j, xs, gt_s, g_post[l], rows_s)

    y_prompt = xp.reshape(batch, seq, d)
    y_sample = xs.reshape(dec_batch, tp, d)[:, :dec_seq]
    return (y_prompt, y_sample, jnp.stack(kp_l), jnp.stack(vp_l), jnp.stack(ks_l), jnp.stack(vs_l),
            jnp.stack(cp_l), jnp.stack(np_l), jnp.stack(mp_l), jnp.stack(cs_l), jnp.stack(ns_l), jnp.stack(ms_l))
```

```python
import functools

import jax
import jax.numpy as jnp
from jax import lax
from jax.experimental import pallas as pl
from jax.experimental.pallas import tpu as pltpu

F32 = jnp.float32
BF16 = jnp.bfloat16

NORM_EPS = 1e-6
WINDOW = 128
ATT_HEAD_DIM = 64
ATT_KV_HEADS = 8
ATT_GROUP = 4
ATT_Q_HEADS = ATT_KV_HEADS * ATT_GROUP
ATT_KV_WIDTH = ATT_KV_HEADS * ATT_HEAD_DIM
M_HEADS = 8
M_DK = 128
M_DV = 256
M_QK_WIDTH = M_HEADS * M_DK
M_CHUNK = 128
SAMPLE_ROWS = 8
LANES = 128

V7X_VMEM_LIMIT = 56 * 1024 * 1024


def _params(sem, vmem=V7X_VMEM_LIMIT):
    return pltpu.CompilerParams(dimension_semantics=sem, vmem_limit_bytes=vmem)


def _sigmoid(x):
    return 1.0 / (1.0 + jnp.exp(-x))


def _silu(x):
    return x * _sigmoid(x)


def _log_sigmoid(x):
    return jnp.minimum(x, 0.0) - jnp.log1p(jnp.exp(-jnp.abs(x)))


def _alibi_slope(head):
    return float(2.0 ** (-8.0 * (head + 1) / ATT_Q_HEADS))


def _ada_kernel(c_ref, w_ref, b_ref, o_ref):
    s = _silu(c_ref[...]).astype(BF16)
    o_ref[...] = jnp.dot(s, w_ref[...].astype(BF16), preferred_element_type=F32) + b_ref[...]


def _ada_all_layers(c_all, w_ada, b_ada, tn=1024):
    depth, d, n = w_ada.shape
    r = c_all.shape[0]
    return pl.pallas_call(
        _ada_kernel,
        grid=(depth, n // tn),
        in_specs=[
            pl.BlockSpec((r, d), lambda l, j: (0, 0)),
            pl.BlockSpec((None, d, tn), lambda l, j: (l, 0, j)),
            pl.BlockSpec((None, 1, tn), lambda l, j: (l, 0, j)),
        ],
        out_specs=pl.BlockSpec((None, r, tn), lambda l, j: (l, 0, j)),
        out_shape=jax.ShapeDtypeStruct((depth, r, n), F32),
        compiler_params=_params(("arbitrary", "arbitrary")),
        name="adaln_mod",
    )(c_all, w_ada, b_ada.reshape(depth, 1, n))


def _prenorm_kernel(x_ref, sh_ref, sc_ref, g_ref, h_ref):
    x = x_ref[...]
    y = x * lax.rsqrt(jnp.mean(x * x, axis=-1, keepdims=True) + NORM_EPS) * g_ref[...]
    h_ref[...] = (y * (1.0 + sc_ref[...]) + sh_ref[...]).astype(h_ref.dtype)


def _prenorm(x, shift, scale, g, tm):
    m, d = x.shape
    groups, r, _ = shift.shape
    tiles_per_group = m // tm // groups
    mod_spec = pl.BlockSpec((None, r, d), lambda i: (i // tiles_per_group, 0, 0))
    return pl.pallas_call(
        _prenorm_kernel,
        grid=(m // tm,),
        in_specs=[pl.BlockSpec((tm, d), lambda i: (i, 0)), mod_spec, mod_spec,
                  pl.BlockSpec((1, d), lambda i: (0, 0))],
        out_specs=pl.BlockSpec((tm, d), lambda i: (i, 0)),
        out_shape=jax.ShapeDtypeStruct((m, d), BF16),
        compiler_params=_params(("arbitrary",)),
        name="prenorm_mod",
    )(x, shift, scale, g.reshape(1, d))


def _proj_kernel(h_ref, hs_ref, w_ref, o_ref, os_ref, wb_ref):
    @pl.when(pl.program_id(1) == 0)
    def _():
        wb_ref[...] = w_ref[...].astype(BF16)
        os_ref[...] = jnp.dot(hs_ref[...], wb_ref[...], preferred_element_type=F32)

    o_ref[...] = jnp.dot(h_ref[...], wb_ref[...], preferred_element_type=F32)


def _proj(h, hs, w_stack, layer, n_cols, tm, tn):
    m, d = h.shape
    ms = hs.shape[0]
    return pl.pallas_call(
        _proj_kernel,
        grid=(n_cols // tn, m // tm),
        in_specs=[pl.BlockSpec((tm, d), lambda j, i: (i, 0)),
                  pl.BlockSpec((ms, d), lambda j, i: (0, 0)),
                  pl.BlockSpec((None, d, tn), lambda j, i: (layer, 0, j))],
        out_specs=[pl.BlockSpec((tm, tn), lambda j, i: (i, j)),
                   pl.BlockSpec((ms, tn), lambda j, i: (0, j))],
        out_shape=[jax.ShapeDtypeStruct((m, n_cols), F32), jax.ShapeDtypeStruct((ms, n_cols), F32)],
        scratch_shapes=[pltpu.VMEM((d, tn), BF16)],
        compiler_params=_params(("arbitrary", "arbitrary")),
        name="in_proj",
    )(h, hs, w_stack)


def _gate_proj_kernel(h_ref, hs_ref, w_ref, o_ref, os_ref):
    wb = w_ref[...].astype(BF16)

    @pl.when(pl.program_id(0) == 0)
    def _():
        os_ref[...] = jnp.dot(hs_ref[...], wb, preferred_element_type=F32)

    o_ref[...] = jnp.dot(h_ref[...], wb, preferred_element_type=F32)


def _gate_proj(h, hs, w_stack, layer, col0, tm):
    m, d = h.shape
    ms = hs.shape[0]
    tn = LANES
    return pl.pallas_call(
        _gate_proj_kernel,
        grid=(m // tm,),
        in_specs=[pl.BlockSpec((tm, d), lambda i: (i, 0)),
                  pl.BlockSpec((ms, d), lambda i: (0, 0)),
                  pl.BlockSpec((None, d, tn), lambda i: (layer, 0, col0 // tn))],
        out_specs=[pl.BlockSpec((tm, tn), lambda i: (i, 0)), pl.BlockSpec((ms, tn), lambda i: (0, 0))],
        out_shape=[jax.ShapeDtypeStruct((m, tn), F32), jax.ShapeDtypeStruct((ms, tn), F32)],
        compiler_params=_params(("arbitrary",)),
        name="mlstm_gate_proj",
    )(h, hs, w_stack)


def _post_norm_residual(y, x, gate, g):
    return x + gate * (y * lax.rsqrt(jnp.mean(y * y, axis=-1, keepdims=True) + NORM_EPS) * g)


def _out_kernel(a_ref, as_ref, w_ref, x_ref, xs_ref, gt_ref, gts_ref, g_ref, o_ref, os_ref):
    @pl.when(pl.program_id(0) == 0)
    def _():
        ys = jnp.dot(as_ref[...], w_ref[...], preferred_element_type=F32)
        os_ref[...] = _post_norm_residual(ys, xs_ref[...], gts_ref[...], g_ref[...])

    y = jnp.dot(a_ref[...], w_ref[...], preferred_element_type=F32)
    o_ref[...] = _post_norm_residual(y, x_ref[...], gt_ref[...], g_ref[...])


def _out_proj(a, a_s, w_stack, layer, x, xs, gate, gate_s, g_post, tm):
    m, d_in = a.shape
    ms = a_s.shape[0]
    d = x.shape[1]
    tiles_per_seq = m // tm // gate.shape[0]
    once = pl.Buffered(1)
    return pl.pallas_call(
        _out_kernel,
        grid=(m // tm,),
        in_specs=[pl.BlockSpec((tm, d_in), lambda i: (i, 0)),
                  pl.BlockSpec((ms, d_in), lambda i: (0, 0), pipeline_mode=once),
                  pl.BlockSpec((None, d_in, d), lambda i: (layer, 0, 0), pipeline_mode=once),
                  pl.BlockSpec((tm, d), lambda i: (i, 0)),
                  pl.BlockSpec((ms, d), lambda i: (0, 0), pipeline_mode=once),
                  pl.BlockSpec((None, 1, d), lambda i: (i // tiles_per_seq, 0, 0)),
                  pl.BlockSpec((ms, d), lambda i: (0, 0), pipeline_mode=once),
                  pl.BlockSpec((1, d), lambda i: (0, 0))],
        out_specs=[pl.BlockSpec((tm, d), lambda i: (i, 0)), pl.BlockSpec((ms, d), lambda i: (0, 0))],
        out_shape=[jax.ShapeDtypeStruct((m, d), F32), jax.ShapeDtypeStruct((ms, d), F32)],
        compiler_params=_params(("arbitrary",)),
        name="out_proj_postnorm",
    )(a, a_s, w_stack, x, xs, gate, gate_s, g_post.reshape(1, d))


def _group_select(group_col, values):
    out = values[ATT_GROUP - 1]
    for g in range(ATT_GROUP - 2, -1, -1):
        out = jnp.where(group_col == g, values[g], out)
    return out


def _attn_prompt_kernel(sink_ref, q_ref, kc_ref, kp_ref, vc_ref, vp_ref, z0_ref, z1_ref, o_ref, bias_ref):
    blk = WINDOW
    hd = ATT_HEAD_DIM
    i = pl.program_id(1)
    cols = ATT_GROUP * blk

    @pl.when(i <= 1)
    def _():
        key = lax.broadcasted_iota(jnp.int32, (2 * blk, cols), 0)
        qcol = lax.broadcasted_iota(jnp.int32, (2 * blk, cols), 1)
        dist = (qcol % blk) + blk - key
        valid = (dist >= 0) & (dist < WINDOW) & ((key >= blk) | (i > 0))
        dist_f = dist.astype(F32)
        group = qcol // blk
        for h in range(ATT_KV_HEADS):
            slope = _group_select(group, [_alibi_slope(ATT_GROUP * h + g) for g in range(ATT_GROUP)])
            bias_ref[h] = jnp.where(valid, -slope * dist_f, -jnp.inf)

    kcat = jnp.concatenate([kp_ref[...], kc_ref[...]], axis=0).astype(BF16)
    vcat = jnp.concatenate([vp_ref[...], vc_ref[...]], axis=0)
    group_row = lax.broadcasted_iota(jnp.int32, (1, cols), 1) // blk
    half = z0_ref.shape[1]
    vt_pairs = [jnp.concatenate([vcat[:blk, c * LANES:(c + 1) * LANES].T, vcat[blk:, c * LANES:(c + 1) * LANES].T],
                                axis=1) for c in range(ATT_KV_WIDTH // LANES)]
    for h in range(ATT_KV_HEADS):
        pair, odd = divmod(h, 2)
        lanes = slice(pair * LANES, (pair + 1) * LANES)
        qt = [(q_ref[:, (2 * h + t) * LANES:(2 * h + t + 1) * LANES] * (hd ** -0.5)).T for t in range(2)]
        qt = jnp.concatenate([qt[0][:hd], qt[0][hd:], qt[1][:hd], qt[1][hd:]], axis=1).astype(BF16)
        zero = jnp.zeros_like(qt)
        rhs = jnp.concatenate([zero, qt] if odd else [qt, zero], axis=0)
        s = jnp.dot(kcat[:, lanes], rhs, preferred_element_type=F32) + bias_ref[h]
        sink = _group_select(group_row, [sink_ref[ATT_GROUP * h + g] for g in range(ATT_GROUP)])
        mx = jnp.maximum(jnp.max(s, axis=0, keepdims=True), sink)
        p = jnp.exp(s - mx).astype(BF16)
        vt = vt_pairs[pair][hd:] if odd else vt_pairs[pair][:hd]
        lhs = jnp.concatenate([vt, jnp.ones_like(vt)], axis=0).astype(BF16)
        oa = jnp.dot(lhs, p, preferred_element_type=F32)
        den = oa[hd:hd + 1] + jnp.exp(sink - mx)
        on = oa[:hd] * (1.0 / den)
        for t in range(2):
            ot = jnp.concatenate([on[:, (2 * t) * blk:(2 * t + 1) * blk],
                                  on[:, (2 * t + 1) * blk:(2 * t + 2) * blk]], axis=0).T
            c0 = (2 * h + t) * LANES
            z_ref, zc = (z0_ref, c0) if c0 < half else (z1_ref, c0 - half)
            o_ref[:, c0:c0 + LANES] = (ot * _silu(z_ref[:, zc:zc + LANES])).astype(o_ref.dtype)


def _attn_prompt(proj, sinks, batch, seq):
    blk = WINDOW
    nb = seq // blk
    inner = ATT_Q_HEADS * ATT_HEAD_DIM
    kvw = ATT_KV_WIDTH
    k_col = inner // kvw
    v_col = k_col + 1
    half = inner // 2
    z_col = (inner + 2 * kvw) // half
    cur = lambda b, i: b * nb + i
    prev = lambda b, i: b * nb + jnp.maximum(i - 1, 0)
    return pl.pallas_call(
        _attn_prompt_kernel,
        grid=(batch, nb),
        in_specs=[pl.BlockSpec(memory_space=pltpu.SMEM),
                  pl.BlockSpec((blk, inner), lambda b, i: (cur(b, i), 0)),
                  pl.BlockSpec((blk, kvw), lambda b, i: (cur(b, i), k_col)),
                  pl.BlockSpec((blk, kvw), lambda b, i: (prev(b, i), k_col)),
                  pl.BlockSpec((blk, kvw), lambda b, i: (cur(b, i), v_col)),
                  pl.BlockSpec((blk, kvw), lambda b, i: (prev(b, i), v_col)),
                  pl.BlockSpec((blk, half), lambda b, i: (cur(b, i), z_col)),
                  pl.BlockSpec((blk, half), lambda b, i: (cur(b, i), z_col + 1))],
        out_specs=pl.BlockSpec((blk, inner), lambda b, i: (cur(b, i), 0)),
        out_shape=jax.ShapeDtypeStruct((batch * seq, inner), BF16),
        scratch_shapes=[pltpu.VMEM((ATT_KV_HEADS, 2 * blk, ATT_GROUP * blk), F32)],
        compiler_params=_params(("arbitrary", "arbitrary")),
        name="attn_prompt",
    )(sinks, proj, proj, proj, proj, proj, proj, proj)


def _attn_sample_kernel(sink_ref, q_ref, kn_ref, vn_ref, z0_ref, z1_ref, kc_ref, vc_ref, o_ref, bias_ref, *, n_new):
    hd = ATT_HEAD_DIM
    tp = q_ref.shape[0]
    w = kc_ref.shape[0]
    rows = ATT_GROUP * tp

    @pl.when(pl.program_id(0) == 0)
    def _():
        t_row = lax.broadcasted_iota(jnp.int32, (rows, 2 * w), 0) % tp
        key = lax.broadcasted_iota(jnp.int32, (rows, 2 * w), 1)
        dist = t_row + w - key
        valid = (dist >= 0) & (dist < WINDOW) & (key < w + n_new)
        dist_f = dist.astype(F32)
        group = lax.broadcasted_iota(jnp.int32, (rows, 2 * w), 0) // tp
        for h in range(ATT_KV_HEADS):
            slope = _group_select(group, [_alibi_slope(ATT_GROUP * h + g) for g in range(ATT_GROUP)])
            bias_ref[h] = jnp.where(valid, -slope * dist_f, -jnp.inf)

    pad = jnp.zeros((w - tp, kc_ref.shape[1]), F32)
    kcat = jnp.concatenate([kc_ref[...], kn_ref[...], pad], axis=0).astype(BF16)
    vcat = jnp.concatenate([vc_ref[...], vn_ref[...], pad], axis=0).astype(BF16)
    group_col = lax.broadcasted_iota(jnp.int32, (rows, 1), 0) // tp
    half = z0_ref.shape[1]
    for h in range(ATT_KV_HEADS):
        heads = [ATT_GROUP * h + g for g in range(ATT_GROUP)]
        qs = jnp.concatenate([q_ref[:, j * hd:(j + 1) * hd] for j in heads], axis=0) * (hd ** -0.5)
        s = lax.dot_general(qs.astype(BF16), kcat[:, h * hd:(h + 1) * hd], (((1,), (1,)), ((), ())),
                            preferred_element_type=F32) + bias_ref[h]
        sink = _group_select(group_col, [sink_ref[j] for j in heads])
        mx = jnp.maximum(jnp.max(s, axis=-1, keepdims=True), sink)
        p = jnp.exp(s - mx)
        den = jnp.sum(p, axis=-1, keepdims=True) + jnp.exp(sink - mx)
        o = jnp.dot(p.astype(BF16), vcat[:, h * hd:(h + 1) * hd], preferred_element_type=F32) * (1.0 / den)
        for g, j in enumerate(heads):
            c0 = j * hd
            z_ref, zc = (z0_ref, c0) if c0 < half else (z1_ref, c0 - half)
            z = z_ref[:, zc:zc + hd]
            o_ref[:, c0:c0 + hd] = (o[g * tp:(g + 1) * tp] * _silu(z)).astype(o_ref.dtype)


def _attn_sample(proj3, sinks, cache_k4, cache_v4, layer, n_new):
    nbatch, tp, _ = proj3.shape
    w = cache_k4.shape[2]
    inner = ATT_Q_HEADS * ATT_HEAD_DIM
    kvw = ATT_KV_WIDTH
    k_col = inner // kvw
    half = inner // 2
    z_col = (inner + 2 * kvw) // half
    cache_spec = pl.BlockSpec((None, None, w, kvw), lambda b: (layer, b, 0, 0))
    return pl.pallas_call(
        functools.partial(_attn_sample_kernel, n_new=n_new),
        grid=(nbatch,),
        in_specs=[pl.BlockSpec(memory_space=pltpu.SMEM),
                  pl.BlockSpec((None, tp, inner), lambda b: (b, 0, 0)),
                  pl.BlockSpec((None, tp, kvw), lambda b: (b, 0, k_col)),
                  pl.BlockSpec((None, tp, kvw), lambda b: (b, 0, k_col + 1)),
                  pl.BlockSpec((None, tp, half), lambda b: (b, 0, z_col)),
                  pl.BlockSpec((None, tp, half), lambda b: (b, 0, z_col + 1)),
                  cache_spec, cache_spec],
        out_specs=pl.BlockSpec((None, tp, inner), lambda b: (b, 0, 0)),
        out_shape=jax.ShapeDtypeStruct((nbatch, tp, inner), BF16),
        scratch_shapes=[pltpu.VMEM((ATT_KV_HEADS, ATT_GROUP * tp, 2 * w), F32)],
        compiler_params=_params(("arbitrary",)),
        name="attn_sample",
    )(sinks, proj3, proj3, proj3, proj3, proj3, cache_k4, cache_v4)


def _lane_scan(x, op, lane):
    n = x.shape[-1]
    shift = 1
    while shift < n:
        x = jnp.where(lane >= shift, op(x, pltpu.roll(x, shift, axis=x.ndim - 1)), x)
        shift *= 2
    return x


def _gate_prep_kernel(bias_ref, g_ref, a_ref, em_ref, wt_ref, u_ref, w_ref, mn_ref, *, chunks_per_seq):
    rows, length = g_ref.shape[2], g_ref.shape[3]
    lane = lax.broadcasted_iota(jnp.int32, (rows, length), 1)
    chunk = lax.broadcasted_iota(jnp.int32, (rows, length), 0) % chunks_per_seq
    for h in range(M_HEADS):
        li = g_ref[0, h] + bias_ref[h]
        lf = _log_sigmoid(g_ref[1, h] + bias_ref[M_HEADS + h])
        bcum = _lane_scan(lf, jnp.add, lane)
        w = li - bcum
        cmax = _lane_scan(w, jnp.maximum, lane)
        e = jnp.broadcast_to(bcum[:, length - 1:length], (rows, length))
        y = e + jnp.broadcast_to(cmax[:, length - 1:length], (rows, length))
        shift = 1
        while shift < chunks_per_seq:
            e_prev = pltpu.roll(e, shift, axis=0)
            y_prev = pltpu.roll(y, shift, axis=0)
            take = chunk >= shift
            y = jnp.where(take, jnp.maximum(y_prev + e, y), y)
            e = jnp.where(take, e_prev + e, e)
            shift *= 2
        m_incl = jnp.maximum(e, y)
        m_prev = jnp.where(chunk >= 1, pltpu.roll(m_incl, 1, axis=0), 0.0)
        u = -jnp.maximum(m_prev, cmax)
        u_last = jnp.broadcast_to(u[:, length - 1:length], (rows, length))
        a_ref[h] = jnp.exp(m_prev + u)
        em_ref[h] = jnp.exp(u - bcum)
        wt_ref[h] = jnp.exp(w + u_last)
        u_ref[h] = u
        w_ref[h] = w
        mn_ref[h] = m_incl


def _gate_prep(gates_t, b_if, chunks_per_seq):
    shape = gates_t.shape[1:]
    out = jax.ShapeDtypeStruct(shape, F32)
    return pl.pallas_call(
        functools.partial(_gate_prep_kernel, chunks_per_seq=chunks_per_seq),
        in_specs=[pl.BlockSpec(memory_space=pltpu.SMEM), pl.BlockSpec(memory_space=pltpu.VMEM)],
        out_specs=[pl.BlockSpec(memory_space=pltpu.VMEM)] * 6,
        out_shape=[out] * 6,
        name="mlstm_gate_prep",
    )(b_if, gates_t)


def _head_norm_gate(hout, g_row, o, z):
    hn = hout * lax.rsqrt(jnp.mean(hout * hout, axis=-1, keepdims=True) + NORM_EPS) * g_row
    return hn * _sigmoid(o) * _silu(z)


def _mlstm_prompt_kernel(q_ref, k_ref, v_ref, o_ref, z_ref, col_ref, wrow_ref, gh_ref,
                         out_ref, s_out_ref, n_out_ref, s_ref, n_ref):
    c = pl.program_id(1)
    length = q_ref.shape[0]

    @pl.when(c == 0)
    def _():
        s_ref[...] = jnp.zeros_like(s_ref)
        n_ref[...] = jnp.zeros_like(n_ref)

    t_idx = lax.broadcasted_iota(jnp.int32, (length, length), 0)
    s_idx = lax.broadcasted_iota(jnp.int32, (length, length), 1)
    causal = s_idx <= t_idx
    col = col_ref[...]
    for h in range(M_HEADS):
        a_col = col[:, h:h + 1]
        e_col = col[:, M_HEADS + h:M_HEADS + h + 1]
        wt_col = col[:, 2 * M_HEADS + h:2 * M_HEADS + h + 1]
        u_col = col[:, 3 * M_HEADS + h:3 * M_HEADS + h + 1]
        a_last = a_col[length - 1:length, :]
        qf = q_ref[:, h * M_DK:(h + 1) * M_DK] * (M_DK ** -0.5)
        qb = qf.astype(BF16)
        kf = k_ref[:, h * M_DK:(h + 1) * M_DK]
        vb = v_ref[:, h * M_DV:(h + 1) * M_DV].astype(BF16)
        dmat = jnp.where(causal, jnp.exp(u_col + wrow_ref[h:h + 1, :]), 0.0)
        qk = lax.dot_general(qb, kf.astype(BF16), (((1,), (1,)), ((), ())), preferred_element_type=F32) * dmat
        s_old = s_ref[h]
        n_old = n_ref[h:h + 1, :]
        num = a_col * jnp.dot(qb, s_old.astype(BF16), preferred_element_type=F32) \
            + jnp.dot(qk.astype(BF16), vb, preferred_element_type=F32)
        den = a_col * jnp.sum(qf * n_old, axis=-1, keepdims=True) + jnp.sum(qk, axis=-1, keepdims=True)
        hout = num / jnp.maximum(jnp.abs(den), e_col)
        kw = kf * wt_col
        s_ref[h] = a_last * s_old + jnp.dot(kw.T.astype(BF16), vb, preferred_element_type=F32)
        n_ref[h:h + 1, :] = a_last * n_old + jnp.sum(kw, axis=0, keepdims=True)
        sl = slice(h * M_DV, (h + 1) * M_DV)
        out_ref[:, sl] = _head_norm_gate(hout, gh_ref[:, sl], o_ref[:, sl], z_ref[:, sl]).astype(out_ref.dtype)

    @pl.when(c == pl.num_programs(1) - 1)
    def _():
        s_out_ref[...] = s_ref[...]
        n_out_ref[...] = n_ref[...]


def _mlstm_prompt(proj, col, wrow, g_head, batch, seq):
    length = M_CHUNK
    nc = seq // length
    inner = M_HEADS * M_DV
    qkw = M_QK_WIDTH
    v_col = 2 * qkw // inner
    row = lambda b, c: b * nc + c
    return pl.pallas_call(
        _mlstm_prompt_kernel,
        grid=(batch, nc),
        in_specs=[pl.BlockSpec((length, qkw), lambda b, c: (row(b, c), 0)),
                  pl.BlockSpec((length, qkw), lambda b, c: (row(b, c), 1)),
                  pl.BlockSpec((length, inner), lambda b, c: (row(b, c), v_col)),
                  pl.BlockSpec((length, inner), lambda b, c: (row(b, c), v_col + 1)),
                  pl.BlockSpec((length, inner), lambda b, c: (row(b, c), v_col + 2)),
                  pl.BlockSpec((length, 4 * M_HEADS), lambda b, c: (row(b, c), 0)),
                  pl.BlockSpec((None, M_HEADS, length), lambda b, c: (row(b, c), 0, 0)),
                  pl.BlockSpec((1, inner), lambda b, c: (0, 0))],
        out_specs=[pl.BlockSpec((length, inner), lambda b, c: (row(b, c), 0)),
                   pl.BlockSpec((None, M_HEADS, M_DK, M_DV), lambda b, c: (b, 0, 0, 0)),
                   pl.BlockSpec((None, M_HEADS, M_DK), lambda b, c: (b, 0, 0))],
        out_shape=[jax.ShapeDtypeStruct((batch * seq, inner), BF16),
                   jax.ShapeDtypeStruct((batch, M_HEADS, M_DK, M_DV), F32),
                   jax.ShapeDtypeStruct((batch, M_HEADS, M_DK), F32)],
        scratch_shapes=[pltpu.VMEM((M_HEADS, M_DK, M_DV), F32), pltpu.VMEM((M_HEADS, M_DK), F32)],
        compiler_params=_params(("arbitrary", "arbitrary")),
        name="mlstm_prompt_scan",
    )(proj, proj, proj, proj, proj, col, wrow, g_head.reshape(1, inner))


def _prefix_scan(x, op, axis, n):
    idx = lax.broadcasted_iota(jnp.int32, x.shape, axis)
    take = (lambda t: x[t:t + 1, :]) if axis == 0 else (lambda t: x[:, t:t + 1])
    run = take(0)
    out = jnp.broadcast_to(run, x.shape)
    for t in range(1, n):
        run = op(run, take(t))
        out = jnp.where(idx >= t, run, out)
    return out


def _mlstm_sample_kernel(*refs, n_new, has_acc):
    (q_ref, k_ref, v_ref, o_ref, z_ref, g_ref, gt_ref, brow_ref, bcol_ref, mrow_ref, gh_ref,
     s_in_ref, n_in_ref) = refs[:13]
    out_ref, s_out_ref, n_out_ref, m_out_ref = refs[13 + has_acc:]
    tp = q_ref.shape[0]
    last = n_new - 1
    hs = M_HEADS
    g = g_ref[...] + brow_ref[...]
    bcum_c = _prefix_scan(_log_sigmoid(g[:, hs:]), jnp.add, 0, n_new)
    w_c = g[:, :hs] - bcum_c
    m_prev_c = mrow_ref[...]
    u_c = -jnp.maximum(m_prev_c, _prefix_scan(w_c, jnp.maximum, 0, n_new))
    a_c = jnp.exp(m_prev_c + u_c)
    e_c = jnp.exp(u_c - bcum_c)
    real_c = lax.broadcasted_iota(jnp.int32, (tp, hs), 0) < n_new
    wt_c = jnp.where(real_c, jnp.exp(w_c + u_c[last:last + 1, :]), 0.0)
    m_out_ref[...] = bcum_c[last:last + 1, :] - u_c[last:last + 1, :]
    gt = gt_ref[...] + bcol_ref[...]
    w_r = gt[:hs, :] - _prefix_scan(_log_sigmoid(gt[hs:, :]), jnp.add, 1, n_new)
    keys = w_r.shape[1]
    t_idx = lax.broadcasted_iota(jnp.int32, (tp, keys), 0)
    s_idx = lax.broadcasted_iota(jnp.int32, (tp, keys), 1)
    causal = s_idx <= t_idx
    k_pad = jnp.concatenate([k_ref[...], jnp.zeros((keys - tp, k_ref.shape[1]), F32)], axis=0)
    v_pad = jnp.concatenate([v_ref[...], jnp.zeros((keys - tp, v_ref.shape[1]), F32)], axis=0).astype(BF16)
    wt_pad = jnp.concatenate([wt_c, jnp.zeros((keys - tp, hs), F32)], axis=0)
    for h in range(hs):
        a_col = a_c[:, h:h + 1]
        a_last = a_col[last:last + 1, :]
        qf = q_ref[:, h * M_DK:(h + 1) * M_DK] * (M_DK ** -0.5)
        qb = qf.astype(BF16)
        kf = k_pad[:, h * M_DK:(h + 1) * M_DK]
        vb = v_pad[:, h * M_DV:(h + 1) * M_DV]
        dmat = jnp.where(causal, jnp.exp(u_c[:, h:h + 1] + w_r[h:h + 1, :]), 0.0)
        qk = lax.dot_general(qb, kf.astype(BF16), (((1,), (1,)), ((), ())), preferred_element_type=F32) * dmat
        s_old = s_in_ref[h]
        n_old = n_in_ref[h:h + 1, :]
        num = a_col * jnp.dot(qb, s_old.astype(BF16), preferred_element_type=F32) \
            + jnp.dot(qk.astype(BF16), vb, preferred_element_type=F32)
        den = a_col * jnp.sum(qf * n_old, axis=-1, keepdims=True) + jnp.sum(qk, axis=-1, keepdims=True)
        hout = num / jnp.maximum(jnp.abs(den), e_c[:, h:h + 1])
        kw = kf * wt_pad[:, h:h + 1]
        s_out_ref[h] = a_last * s_old + jnp.dot(kw.T.astype(BF16), vb, preferred_element_type=F32)
        n_out_ref[h:h + 1, :] = a_last * n_old + jnp.sum(kw, axis=0, keepdims=True)
        sl = slice(h * M_DV, (h + 1) * M_DV)
        out_ref[:, sl] = _head_norm_gate(hout, gh_ref[:, sl], o_ref[:, sl], z_ref[:, sl]).astype(out_ref.dtype)


def _mlstm_sample(proj3, gates3, gates3_t, b_if, state_c, state_n, state_m, g_head, layer, n_new, c_acc):
    nbatch, tp, _ = proj3.shape
    hs = M_HEADS
    inner = hs * M_DV
    qkw = M_QK_WIDTH
    v_col = 2 * qkw // inner
    keys = gates3_t.shape[2]
    m_row = state_m[layer].reshape(nbatch, 1, hs)
    has_acc = c_acc is not None
    in_specs = [pl.BlockSpec((None, tp, qkw), lambda b: (b, 0, 0)),
                pl.BlockSpec((None, tp, qkw), lambda b: (b, 0, 1)),
                pl.BlockSpec((None, tp, inner), lambda b: (b, 0, v_col)),
                pl.BlockSpec((None, tp, inner), lambda b: (b, 0, v_col + 1)),
                pl.BlockSpec((None, tp, inner), lambda b: (b, 0, v_col + 2)),
                pl.BlockSpec((None, tp, 2 * hs), lambda b: (b, 0, 0)),
                pl.BlockSpec((None, 2 * hs, keys), lambda b: (b, 0, 0)),
                pl.BlockSpec((1, 2 * hs), lambda b: (0, 0)),
                pl.BlockSpec((2 * hs, 1), lambda b: (0, 0)),
                pl.BlockSpec((None, 1, hs), lambda b: (b, 0, 0)),
                pl.BlockSpec((1, inner), lambda b: (0, 0)),
                pl.BlockSpec((None, None, hs, M_DK, M_DV), lambda b: (layer, b, 0, 0, 0)),
                pl.BlockSpec((None, None, hs, M_DK), lambda b: (layer, b, 0, 0))]
    args = [proj3, proj3, proj3, proj3, proj3, gates3, gates3_t, b_if.reshape(1, 2 * hs), b_if.reshape(2 * hs, 1),
            m_row, g_head.reshape(1, inner), state_c, state_n]
    if has_acc:
        in_specs.append(pl.BlockSpec(memory_space=pl.ANY))
        args.append(c_acc)
    return pl.pallas_call(
        functools.partial(_mlstm_sample_kernel, n_new=n_new, has_acc=has_acc),
        grid=(nbatch,),
        in_specs=in_specs,
        out_specs=[pl.BlockSpec((None, tp, inner), lambda b: (b, 0, 0)),
                   pl.BlockSpec((None, None, hs, M_DK, M_DV), lambda b: (layer, b, 0, 0, 0)),
                   pl.BlockSpec((None, hs, M_DK), lambda b: (b, 0, 0)),
                   pl.BlockSpec((None, 1, hs), lambda b: (b, 0, 0))],
        out_shape=[jax.ShapeDtypeStruct((nbatch, tp, inner), BF16),
                   jax.ShapeDtypeStruct(state_c.shape, F32),
                   jax.ShapeDtypeStruct((nbatch, hs, M_DK), F32),
                   jax.ShapeDtypeStruct((nbatch, 1, hs), F32)],
        input_output_aliases={len(args) - 1: 1} if has_acc else {},
        compiler_params=_params(("arbitrary",)),
        name="mlstm_sample_step",
    )(*args)


def kernel(x_prompt, x_sample, c_prompt, c_sample, cache_k, cache_v, state_C, state_n, state_m, w_ada, b_ada,
           g_pre, g_post, w_in_attn, sinks, w_out_attn, w_in_mlstm, b_if_mlstm, g_head_mlstm, w_out_mlstm):
    batch, seq, d = x_prompt.shape
    dec_batch, dec_seq, _ = x_sample.shape
    depth = w_ada.shape[0]
    tp = SAMPLE_ROWS
    w_buf = cache_k.shape[2]
    rows_p = batch * seq
    rows_s = dec_batch * tp
    tm = 1024
    att_inner = ATT_Q_HEADS * ATT_HEAD_DIM
    att_cols = 2 * att_inner + 2 * ATT_KV_WIDTH
    m_cols = 2 * M_QK_WIDTH + 3 * M_HEADS * M_DV

    c_rows = batch + dec_batch
    c_pad = -c_rows % 8
    c_all = jnp.concatenate([c_prompt, c_sample, jnp.zeros((c_pad, d), F32)], axis=0)
    mod = _ada_all_layers(c_all, w_ada, b_ada)

    xp = x_prompt.reshape(rows_p, d)
    xs = jnp.pad(x_sample, ((0, 0), (0, tp - dec_seq), (0, 0))).reshape(rows_s, d)
    cache_k4 = cache_k.reshape(cache_k.shape[:3] + (ATT_KV_WIDTH,))
    cache_v4 = cache_v.reshape(cache_v.shape[:3] + (ATT_KV_WIDTH,))
    w_out_attn_b = w_out_attn.astype(BF16)
    w_out_mlstm_b = w_out_mlstm.astype(BF16)

    kp_l, vp_l, ks_l, vs_l = [], [], [], []
    cp_l, np_l, mp_l, ns_l, ms_l = [], [], [], [], []
    c_sample_new = None
    for l in range(depth):
        j = l // 2
        mod_p = mod[l, :batch].reshape(batch, 1, 3 * d)
        mod_s = jnp.repeat(mod[l, batch:c_rows], tp, axis=0)
        sh_p, sc_p, gt_p = mod_p[..., :d], mod_p[..., d:2 * d], mod_p[..., 2 * d:]
        sh_s, sc_s, gt_s = mod_s[None, :, :d], mod_s[None, :, d:2 * d], mod_s[:, 2 * d:]
        hp = _prenorm(xp, sh_p, sc_p, g_pre[l], tm)
        hs = _prenorm(xs, sh_s, sc_s, g_pre[l], rows_s)
        if l % 2 == 0:
            pp, ps = _proj(hp, hs, w_in_attn, j, att_cols, tm, 1024)
            ap = _attn_prompt(pp, sinks[j], batch, seq)
            ps3 = ps.reshape(dec_batch, tp, att_cols)
            a_s = _attn_sample(ps3, sinks[j], cache_k4, cache_v4, j, dec_seq).reshape(rows_s, att_inner)
            k0, v0 = att_inner, att_inner + ATT_KV_WIDTH
            pp3 = pp.reshape(batch, seq, att_cols)
            kv_shape = (ATT_KV_HEADS, ATT_HEAD_DIM)
            kp_l.append(pp3[:, seq - WINDOW:, k0:v0].reshape((batch, WINDOW) + kv_shape))
            vp_l.append(pp3[:, seq - WINDOW:, v0:v0 + ATT_KV_WIDTH].reshape((batch, WINDOW) + kv_shape))
            k_new = ps3[:, :dec_seq, k0:v0].reshape((dec_batch, dec_seq) + kv_shape)
            v_new = ps3[:, :dec_seq, v0:v0 + ATT_KV_WIDTH].reshape((dec_batch, dec_seq) + kv_shape)
            ks_l.append(jnp.concatenate([cache_k[j], k_new], axis=1)[:, -w_buf:])
            vs_l.append(jnp.concatenate([cache_v[j], v_new], axis=1)[:, -w_buf:])
            w_out = w_out_attn_b
        else:
            pp, ps = _proj(hp, hs, w_in_mlstm, j, m_cols, tm, 1024)
            gp, gs = _gate_proj(hp, hs, w_in_mlstm, j, m_cols, tm)
            gp, gs = gp[:, :2 * M_HEADS], gs[:, :2 * M_HEADS]
            nc = seq // M_CHUNK
            gates_t = gp.T.reshape(2, M_HEADS, batch * nc, M_CHUNK)
            a_q, em_q, wt_q, u_q, w_q, mn_q = _gate_prep(gates_t, b_if_mlstm[j], nc)
            col = jnp.stack([a_q, em_q, wt_q, u_q]).transpose(2, 3, 0, 1).reshape(rows_p, 4 * M_HEADS)
            wrow = w_q.transpose(1, 0, 2)
            ap, c_new, n_new = _mlstm_prompt(pp, col, wrow, g_head_mlstm[j], batch, seq)
            cp_l.append(c_new)
            np_l.append(n_new)
            mp_l.append(mn_q.reshape(M_HEADS, batch, nc, M_CHUNK)[:, :, nc - 1, 0].T)
            ps3 = ps.reshape(dec_batch, tp, m_cols)
            gs3 = gs.reshape(dec_batch, tp, 2 * M_HEADS)
            gs3_t = jnp.pad(gs3.transpose(0, 2, 1), ((0, 0), (0, 0), (0, LANES - tp)))
            a_s, c_sample_new, n_new, m_new = _mlstm_sample(ps3, gs3, gs3_t, b_if_mlstm[j], state_C, state_n, state_m,
                                                            g_head_mlstm[j], j, dec_seq, c_sample_new)
            a_s = a_s.reshape(rows_s, M_HEADS * M_DV)
            ns_l.append(n_new)
            ms_l.append(m_new.reshape(dec_batch, M_HEADS))
            w_out = w_out_mlstm_b
        xp, xs = _out_proj(ap, a_s, w_out, j, xp, xs, gt_p, gt_s, g_post[l], 512)

    y_prompt = xp.reshape(batch, seq, d)
    y_sample = xs.reshape(dec_batch, tp, d)[:, :dec_seq]
    return (y_prompt, y_sample, jnp.stack(kp_l), jnp.stack(vp_l), jnp.stack(ks_l), jnp.stack(vs_l),
            jnp.stack(cp_l), jnp.stack(np_l), jnp.stack(mp_l), c_sample_new, jnp.stack(ns_l), jnp.stack(ms_l))
```

```python
import functools

import jax
import jax.numpy as jnp
from jax import lax
from jax.experimental import pallas as pl
from jax.experimental.pallas import tpu as pltpu

F32 = jnp.float32
BF16 = jnp.bfloat16

NORM_EPS = 1e-6
WINDOW = 128
ATT_HEAD_DIM = 64
ATT_KV_HEADS = 8
ATT_GROUP = 4
ATT_Q_HEADS = ATT_KV_HEADS * ATT_GROUP
ATT_KV_WIDTH = ATT_KV_HEADS * ATT_HEAD_DIM
M_HEADS = 8
M_DK = 128
M_DV = 256
M_QK_WIDTH = M_HEADS * M_DK
M_CHUNK = 128
SAMPLE_ROWS = 8
LANES = 128
LOG2E = 1.4426950408889634

V7X_VMEM_LIMIT = 56 * 1024 * 1024


def _params(sem, vmem=V7X_VMEM_LIMIT):
    return pltpu.CompilerParams(dimension_semantics=sem, vmem_limit_bytes=vmem)


def _sigmoid(x):
    return 1.0 / (1.0 + jnp.exp(-x))


def _silu(x):
    return x * _sigmoid(x)


def _log_sigmoid(x):
    return jnp.minimum(x, 0.0) - jnp.log1p(jnp.exp(-jnp.abs(x)))


def _alibi_slope(head):
    return float(2.0 ** (-8.0 * (head + 1) / ATT_Q_HEADS))


def _ada_kernel(c_ref, w_ref, b_ref, o_ref):
    s = _silu(c_ref[...]).astype(BF16)
    o_ref[...] = jnp.dot(s, w_ref[...].astype(BF16), preferred_element_type=F32) + b_ref[...]


def _ada_all_layers(c_all, w_ada, b_ada, tn=1024):
    depth, d, n = w_ada.shape
    r = c_all.shape[0]
    return pl.pallas_call(
        _ada_kernel,
        grid=(depth, n // tn),
        in_specs=[
            pl.BlockSpec((r, d), lambda l, j: (0, 0)),
            pl.BlockSpec((None, d, tn), lambda l, j: (l, 0, j)),
            pl.BlockSpec((None, 1, tn), lambda l, j: (l, 0, j)),
        ],
        out_specs=pl.BlockSpec((None, r, tn), lambda l, j: (l, 0, j)),
        out_shape=jax.ShapeDtypeStruct((depth, r, n), F32),
        compiler_params=_params(("arbitrary", "arbitrary")),
        name="adaln_mod",
    )(c_all, w_ada, b_ada.reshape(depth, 1, n))


def _pre_norm_mod(x, shift, scale, g):
    y = x * lax.rsqrt(jnp.mean(x * x, axis=-1, keepdims=True) + NORM_EPS) * g
    return y * (1.0 + scale) + shift


def _prenorm_kernel(x_ref, sh_ref, sc_ref, g_ref, h_ref):
    h_ref[...] = _pre_norm_mod(x_ref[...], sh_ref[...], sc_ref[...], g_ref[...]).astype(h_ref.dtype)


def _prenorm(x, shift, scale, g, tm):
    m, d = x.shape
    groups, r, _ = shift.shape
    tiles_per_group = m // tm // groups
    mod_spec = pl.BlockSpec((None, r, d), lambda i: (i // tiles_per_group, 0, 0))
    return pl.pallas_call(
        _prenorm_kernel,
        grid=(m // tm,),
        in_specs=[pl.BlockSpec((tm, d), lambda i: (i, 0)), mod_spec, mod_spec,
                  pl.BlockSpec((1, d), lambda i: (0, 0))],
        out_specs=pl.BlockSpec((tm, d), lambda i: (i, 0)),
        out_shape=jax.ShapeDtypeStruct((m, d), BF16),
        compiler_params=_params(("arbitrary",)),
        name="prenorm_mod",
    )(x, shift, scale, g.reshape(1, d))


def _matmul(a, w, w_is_transposed):
    dims = (((1,), (1,)), ((), ())) if w_is_transposed else (((1,), (0,)), ((), ()))
    return lax.dot_general(a, w, dims, preferred_element_type=F32)


def _proj_kernel(h_ref, hs_ref, w_ref, o_ref, os_ref, wb_ref, *, w_is_transposed):
    @pl.when(pl.program_id(1) == 0)
    def _():
        wb_ref[...] = w_ref[...].astype(BF16)
        os_ref[...] = _matmul(hs_ref[...], wb_ref[...], w_is_transposed)

    o_ref[...] = _matmul(h_ref[...], wb_ref[...], w_is_transposed)


def _proj(h, hs, w_stack, layer, n_cols, tm, tn, w_is_transposed=False):
    m, d = h.shape
    ms = hs.shape[0]
    if w_is_transposed:
        w_spec = pl.BlockSpec((None, tn, d), lambda j, i: (layer, j, 0))
        w_tile = (tn, d)
    else:
        w_spec = pl.BlockSpec((None, d, tn), lambda j, i: (layer, 0, j))
        w_tile = (d, tn)
    return pl.pallas_call(
        functools.partial(_proj_kernel, w_is_transposed=w_is_transposed),
        grid=(n_cols // tn, m // tm),
        in_specs=[pl.BlockSpec((tm, d), lambda j, i: (i, 0)),
                  pl.BlockSpec((ms, d), lambda j, i: (0, 0)),
                  w_spec],
        out_specs=[pl.BlockSpec((tm, tn), lambda j, i: (i, j)),
                   pl.BlockSpec((ms, tn), lambda j, i: (0, j))],
        out_shape=[jax.ShapeDtypeStruct((m, n_cols), F32), jax.ShapeDtypeStruct((ms, n_cols), F32)],
        scratch_shapes=[pltpu.VMEM(w_tile, BF16)],
        compiler_params=_params(("arbitrary", "arbitrary")),
        name="in_proj",
    )(h, hs, w_stack)


def _gate_proj_kernel(h_ref, hs_ref, w_ref, o_ref, os_ref):
    wb = w_ref[...].astype(BF16)

    @pl.when(pl.program_id(0) == 0)
    def _():
        os_ref[...] = _matmul(hs_ref[...], wb, True)

    o_ref[...] = _matmul(h_ref[...], wb, True)


def _gate_proj(h, hs, wt_stack, layer, row0, n, tm):
    m, d = h.shape
    ms = hs.shape[0]
    return pl.pallas_call(
        _gate_proj_kernel,
        grid=(m // tm,),
        in_specs=[pl.BlockSpec((tm, d), lambda i: (i, 0)),
                  pl.BlockSpec((ms, d), lambda i: (0, 0)),
                  pl.BlockSpec((None, n, d), lambda i: (layer, row0 // n, 0))],
        out_specs=[pl.BlockSpec((tm, n), lambda i: (i, 0)), pl.BlockSpec((ms, n), lambda i: (0, 0))],
        out_shape=[jax.ShapeDtypeStruct((m, n), F32), jax.ShapeDtypeStruct((ms, n), F32)],
        compiler_params=_params(("arbitrary",)),
        name="mlstm_gate_proj",
    )(h, hs, wt_stack)


def _post_norm_residual(y, x, gate, g):
    return x + gate * (y * lax.rsqrt(jnp.mean(y * y, axis=-1, keepdims=True) + NORM_EPS) * g)


def _out_kernel(*refs, emit_next):
    a_ref, as_ref, w_ref, x_ref, xs_ref, gt_ref, gts_ref, g_ref, gn_ref = refs[:9]
    o_ref, os_ref = refs[9:11]

    def finish(a, x, mod, x_out, h_out):
        y = jnp.dot(a, w_ref[...], preferred_element_type=F32)
        x_new = _post_norm_residual(y, x, mod[0], g_ref[...])
        x_out[...] = x_new
        if emit_next:
            h_out[...] = _pre_norm_mod(x_new, mod[1], mod[2], gn_ref[...]).astype(h_out.dtype)

    @pl.when(pl.program_id(0) == 0)
    def _():
        finish(as_ref[...], xs_ref[...], gts_ref, os_ref, refs[12] if emit_next else None)

    finish(a_ref[...], x_ref[...], gt_ref, o_ref, refs[11] if emit_next else None)


def _out_proj(a, a_s, w_stack, layer, x, xs, mod, mod_s, g_post, g_pre_next, tm, emit_next):
    m, d_in = a.shape
    ms = a_s.shape[0]
    d = x.shape[1]
    tiles_per_seq = m // tm // mod.shape[0]
    once = pl.Buffered(1)
    row_spec = lambda width: pl.BlockSpec((tm, width), lambda i: (i, 0))
    fixed_spec = lambda width: pl.BlockSpec((ms, width), lambda i: (0, 0))
    out_specs = [row_spec(d), fixed_spec(d)]
    out_shape = [jax.ShapeDtypeStruct((m, d), F32), jax.ShapeDtypeStruct((ms, d), F32)]
    if emit_next:
        out_specs += [row_spec(d), fixed_spec(d)]
        out_shape += [jax.ShapeDtypeStruct((m, d), BF16), jax.ShapeDtypeStruct((ms, d), BF16)]
    return pl.pallas_call(
        functools.partial(_out_kernel, emit_next=emit_next),
        grid=(m // tm,),
        in_specs=[row_spec(d_in),
                  pl.BlockSpec((ms, d_in), lambda i: (0, 0), pipeline_mode=once),
                  pl.BlockSpec((None, d_in, d), lambda i: (layer, 0, 0), pipeline_mode=once),
                  row_spec(d),
                  pl.BlockSpec((ms, d), lambda i: (0, 0), pipeline_mode=once),
                  pl.BlockSpec((None, 3, 1, d), lambda i: (i // tiles_per_seq, 0, 0, 0)),
                  pl.BlockSpec((3, ms, d), lambda i: (0, 0, 0), pipeline_mode=once),
                  pl.BlockSpec((1, d), lambda i: (0, 0)),
                  pl.BlockSpec((1, d), lambda i: (0, 0))],
        out_specs=out_specs,
        out_shape=out_shape,
        compiler_params=_params(("arbitrary",)),
        name="out_proj_postnorm",
    )(a, a_s, w_stack, x, xs, mod, mod_s, g_post.reshape(1, d), g_pre_next.reshape(1, d))


def _group_select(group_col, values):
    out = values[ATT_GROUP - 1]
    for g in range(ATT_GROUP - 2, -1, -1):
        out = jnp.where(group_col == g, values[g], out)
    return out


def _attn_prompt_kernel(sink_ref, q_ref, kc_ref, kp_ref, vc_ref, vp_ref, z0_ref, z1_ref, o_ref, bias_ref):
    blk = WINDOW
    hd = ATT_HEAD_DIM
    i = pl.program_id(1)
    cols = ATT_GROUP * blk

    @pl.when(i <= 1)
    def _():
        key = lax.broadcasted_iota(jnp.int32, (2 * blk, cols), 0)
        qcol = lax.broadcasted_iota(jnp.int32, (2 * blk, cols), 1)
        dist = (qcol % blk) + blk - key
        valid = (dist >= 0) & (dist < WINDOW) & ((key >= blk) | (i > 0))
        dist_f = dist.astype(F32)
        group = qcol // blk
        for h in range(ATT_KV_HEADS):
            slope = _group_select(group, [_alibi_slope(ATT_GROUP * h + g) for g in range(ATT_GROUP)])
            bias_ref[h] = jnp.where(valid, (-slope * dist_f) * LOG2E, -jnp.inf)

    kcat = jnp.concatenate([kp_ref[...], kc_ref[...]], axis=0).astype(BF16)
    vcat = jnp.concatenate([vp_ref[...], vc_ref[...]], axis=0)
    group_row = lax.broadcasted_iota(jnp.int32, (1, cols), 1) // blk
    half = z0_ref.shape[1]
    vt_pairs = [jnp.concatenate([vcat[:blk, c * LANES:(c + 1) * LANES].T, vcat[blk:, c * LANES:(c + 1) * LANES].T],
                                axis=1) for c in range(ATT_KV_WIDTH // LANES)]
    for h in range(ATT_KV_HEADS):
        pair, odd = divmod(h, 2)
        lanes = slice(pair * LANES, (pair + 1) * LANES)
        qt = [(q_ref[:, (2 * h + t) * LANES:(2 * h + t + 1) * LANES] * (hd ** -0.5 * LOG2E)).T for t in range(2)]
        qt = jnp.concatenate([qt[0][:hd], qt[0][hd:], qt[1][:hd], qt[1][hd:]], axis=1).astype(BF16)
        zero = jnp.zeros_like(qt)
        rhs = jnp.concatenate([zero, qt] if odd else [qt, zero], axis=0)
        s = jnp.dot(kcat[:, lanes], rhs, preferred_element_type=F32) + bias_ref[h]
        sink = _group_select(group_row, [sink_ref[ATT_GROUP * h + g] for g in range(ATT_GROUP)]) * LOG2E
        mx = jnp.maximum(jnp.max(s, axis=0, keepdims=True), sink)
        p = jnp.exp2(s - mx).astype(BF16)
        vt = vt_pairs[pair][hd:] if odd else vt_pairs[pair][:hd]
        lhs = jnp.concatenate([vt, jnp.ones_like(vt)], axis=0).astype(BF16)
        oa = jnp.dot(lhs, p, preferred_element_type=F32)
        den = oa[hd:hd + 1] + jnp.exp2(sink - mx)
        on = oa[:hd] * (1.0 / den)
        for t in range(2):
            ot = jnp.concatenate([on[:, (2 * t) * blk:(2 * t + 1) * blk],
                                  on[:, (2 * t + 1) * blk:(2 * t + 2) * blk]], axis=0).T
            c0 = (2 * h + t) * LANES
            z_ref, zc = (z0_ref, c0) if c0 < half else (z1_ref, c0 - half)
            o_ref[:, c0:c0 + LANES] = (ot * _silu(z_ref[:, zc:zc + LANES])).astype(o_ref.dtype)


def _attn_prompt(proj, sinks, batch, seq):
    blk = WINDOW
    nb = seq // blk
    inner = ATT_Q_HEADS * ATT_HEAD_DIM
    kvw = ATT_KV_WIDTH
    k_col = inner // kvw
    v_col = k_col + 1
    half = inner // 2
    z_col = (inner + 2 * kvw) // half
    cur = lambda b, i: b * nb + i
    prev = lambda b, i: b * nb + jnp.maximum(i - 1, 0)
    return pl.pallas_call(
        _attn_prompt_kernel,
        grid=(batch, nb),
        in_specs=[pl.BlockSpec(memory_space=pltpu.SMEM),
                  pl.BlockSpec((blk, inner), lambda b, i: (cur(b, i), 0)),
                  pl.BlockSpec((blk, kvw), lambda b, i: (cur(b, i), k_col)),
                  pl.BlockSpec((blk, kvw), lambda b, i: (prev(b, i), k_col)),
                  pl.BlockSpec((blk, kvw), lambda b, i: (cur(b, i), v_col)),
                  pl.BlockSpec((blk, kvw), lambda b, i: (prev(b, i), v_col)),
                  pl.BlockSpec((blk, half), lambda b, i: (cur(b, i), z_col)),
                  pl.BlockSpec((blk, half), lambda b, i: (cur(b, i), z_col + 1))],
        out_specs=pl.BlockSpec((blk, inner), lambda b, i: (cur(b, i), 0)),
        out_shape=jax.ShapeDtypeStruct((batch * seq, inner), BF16),
        scratch_shapes=[pltpu.VMEM((ATT_KV_HEADS, 2 * blk, ATT_GROUP * blk), F32)],
        compiler_params=_params(("arbitrary", "arbitrary")),
        name="attn_prompt",
    )(sinks, proj, proj, proj, proj, proj, proj, proj)


def _attn_sample_kernel(*refs, n_new, has_acc):
    sink_ref, q_ref, kn_ref, vn_ref, z0_ref, z1_ref, kc_ref, vc_ref = refs[:8]
    o_ref, ko_ref, vo_ref, bias_ref = refs[8 + 2 * has_acc:]
    hd = ATT_HEAD_DIM
    tp = q_ref.shape[0]
    w = kc_ref.shape[2]
    rows = ATT_GROUP * tp

    @pl.when(pl.program_id(0) == 0)
    def _():
        t_row = lax.broadcasted_iota(jnp.int32, (rows, 2 * w), 0) % tp
        key = lax.broadcasted_iota(jnp.int32, (rows, 2 * w), 1)
        dist = t_row + w - key
        valid = (dist >= 0) & (dist < WINDOW) & (key < w + n_new)
        dist_f = dist.astype(F32)
        group = lax.broadcasted_iota(jnp.int32, (rows, 2 * w), 0) // tp
        for h in range(ATT_KV_HEADS):
            slope = _group_select(group, [_alibi_slope(ATT_GROUP * h + g) for g in range(ATT_GROUP)])
            bias_ref[h] = jnp.where(valid, -slope * dist_f, -jnp.inf)

    nt_dims = (((1,), (1,)), ((), ()))
    pad = jnp.zeros((w - tp, kn_ref.shape[1]), F32)
    kn_pad = jnp.concatenate([kn_ref[...], pad], axis=0)
    vn_pad = jnp.concatenate([vn_ref[...], pad], axis=0)
    kn_b = kn_pad.astype(BF16)
    vn_b = vn_pad.astype(BF16)
    n_pairs = ATT_KV_WIDTH // LANES
    knt_pairs = [kn_pad[:, c * LANES:(c + 1) * LANES].T for c in range(n_pairs)]
    vnt_pairs = [vn_pad[:, c * LANES:(c + 1) * LANES].T for c in range(n_pairs)]
    is_new = lax.broadcasted_iota(jnp.int32, (hd, w), 1) < n_new
    group_col = lax.broadcasted_iota(jnp.int32, (rows, 1), 0) // tp
    half = z0_ref.shape[1]
    for h in range(ATT_KV_HEADS):
        heads = [ATT_GROUP * h + g for g in range(ATT_GROUP)]
        pair, odd = divmod(h, 2)
        kt = kc_ref[h]
        vt = vc_ref[h]
        qs = (jnp.concatenate([q_ref[:, j * hd:(j + 1) * hd] for j in heads], axis=0) * (hd ** -0.5)).astype(BF16)
        s_old = jnp.dot(qs, kt.astype(BF16), preferred_element_type=F32)
        s_new = lax.dot_general(qs, kn_b[:, h * hd:(h + 1) * hd], nt_dims, preferred_element_type=F32)
        s = jnp.concatenate([s_old, s_new], axis=1) + bias_ref[h]
        sink = _group_select(group_col, [sink_ref[j] for j in heads])
        mx = jnp.maximum(jnp.max(s, axis=-1, keepdims=True), sink)
        p = jnp.exp(s - mx)
        den = jnp.sum(p, axis=-1, keepdims=True) + jnp.exp(sink - mx)
        pb = p.astype(BF16)
        o = lax.dot_general(pb[:, :w], vt.astype(BF16), nt_dims, preferred_element_type=F32) \
            + jnp.dot(pb[:, w:], vn_b[:, h * hd:(h + 1) * hd], preferred_element_type=F32)
        o = o * (1.0 / den)
        for g, j in enumerate(heads):
            c0 = j * hd
            z_ref, zc = (z0_ref, c0) if c0 < half else (z1_ref, c0 - half)
            z = z_ref[:, zc:zc + hd]
            o_ref[:, c0:c0 + hd] = (o[g * tp:(g + 1) * tp] * _silu(z)).astype(o_ref.dtype)
        knt = knt_pairs[pair][odd * hd:(odd + 1) * hd]
        vnt = vnt_pairs[pair][odd * hd:(odd + 1) * hd]
        ko_ref[h] = pltpu.roll(jnp.where(is_new, knt, kt), w - n_new, axis=1)
        vo_ref[h] = pltpu.roll(jnp.where(is_new, vnt, vt), w - n_new, axis=1)


def _attn_sample(proj3, sinks, cache_kt, cache_vt, layer, n_new, k_acc, v_acc):
    nbatch, tp, _ = proj3.shape
    w = cache_kt.shape[4]
    inner = ATT_Q_HEADS * ATT_HEAD_DIM
    kvw = ATT_KV_WIDTH
    k_col = inner // kvw
    half = inner // 2
    z_col = (inner + 2 * kvw) // half
    cache_spec = pl.BlockSpec((None, None, ATT_KV_HEADS, ATT_HEAD_DIM, w), lambda b: (layer, b, 0, 0, 0))
    has_acc = k_acc is not None
    in_specs = [pl.BlockSpec(memory_space=pltpu.SMEM),
                pl.BlockSpec((None, tp, inner), lambda b: (b, 0, 0)),
                pl.BlockSpec((None, tp, kvw), lambda b: (b, 0, k_col)),
                pl.BlockSpec((None, tp, kvw), lambda b: (b, 0, k_col + 1)),
                pl.BlockSpec((None, tp, half), lambda b: (b, 0, z_col)),
                pl.BlockSpec((None, tp, half), lambda b: (b, 0, z_col + 1)),
                cache_spec, cache_spec]
    args = [sinks, proj3, proj3, proj3, proj3, proj3, cache_kt, cache_vt]
    if has_acc:
        in_specs += [pl.BlockSpec(memory_space=pl.ANY)] * 2
        args += [k_acc, v_acc]
    return pl.pallas_call(
        functools.partial(_attn_sample_kernel, n_new=n_new, has_acc=has_acc),
        grid=(nbatch,),
        in_specs=in_specs,
        out_specs=[pl.BlockSpec((None, tp, inner), lambda b: (b, 0, 0)), cache_spec, cache_spec],
        out_shape=[jax.ShapeDtypeStruct((nbatch, tp, inner), BF16),
                   jax.ShapeDtypeStruct(cache_kt.shape, F32), jax.ShapeDtypeStruct(cache_vt.shape, F32)],
        scratch_shapes=[pltpu.VMEM((ATT_KV_HEADS, ATT_GROUP * tp, 2 * w), F32)],
        input_output_aliases={len(args) - 2: 1, len(args) - 1: 2} if has_acc else {},
        compiler_params=_params(("arbitrary",)),
        name="attn_sample",
    )(*args)


def _lane_scan(x, op, lane):
    n = x.shape[-1]
    shift = 1
    while shift < n:
        x = jnp.where(lane >= shift, op(x, pltpu.roll(x, shift, axis=x.ndim - 1)), x)
        shift *= 2
    return x


def _gate_prep_kernel(bias_ref, g_ref, a_ref, em_ref, wt_ref, u_ref, w_ref, mn_ref, *, chunks_per_seq):
    rows, length = g_ref.shape[2], g_ref.shape[3]
    lane = lax.broadcasted_iota(jnp.int32, (rows, length), 1)
    chunk = lax.broadcasted_iota(jnp.int32, (rows, length), 0) % chunks_per_seq
    for h in range(M_HEADS):
        li = g_ref[0, h] + bias_ref[h]
        lf = _log_sigmoid(g_ref[1, h] + bias_ref[M_HEADS + h])
        bcum = _lane_scan(lf, jnp.add, lane)
        w = li - bcum
        cmax = _lane_scan(w, jnp.maximum, lane)
        e = jnp.broadcast_to(bcum[:, length - 1:length], (rows, length))
        y = e + jnp.broadcast_to(cmax[:, length - 1:length], (rows, length))
        shift = 1
        while shift < chunks_per_seq:
            e_prev = pltpu.roll(e, shift, axis=0)
            y_prev = pltpu.roll(y, shift, axis=0)
            take = chunk >= shift
            y = jnp.where(take, jnp.maximum(y_prev + e, y), y)
            e = jnp.where(take, e_prev + e, e)
            shift *= 2
        m_incl = jnp.maximum(e, y)
        m_prev = jnp.where(chunk >= 1, pltpu.roll(m_incl, 1, axis=0), 0.0)
        u = -jnp.maximum(m_prev, cmax)
        u_last = jnp.broadcast_to(u[:, length - 1:length], (rows, length))
        a_ref[h] = jnp.exp(m_prev + u)
        em_ref[h] = jnp.exp(u - bcum)
        wt_ref[h] = jnp.exp(w + u_last)
        u_ref[h] = u
        w_ref[h] = w
        mn_ref[h] = m_incl


def _gate_prep(gates_t, b_if, chunks_per_seq):
    shape = gates_t.shape[1:]
    out = jax.ShapeDtypeStruct(shape, F32)
    return pl.pallas_call(
        functools.partial(_gate_prep_kernel, chunks_per_seq=chunks_per_seq),
        in_specs=[pl.BlockSpec(memory_space=pltpu.SMEM), pl.BlockSpec(memory_space=pltpu.VMEM)],
        out_specs=[pl.BlockSpec(memory_space=pltpu.VMEM)] * 6,
        out_shape=[out] * 6,
        name="mlstm_gate_prep",
    )(b_if, gates_t)


def _head_norm_gate(hout, g_row, o, z):
    hn = hout * lax.rsqrt(jnp.mean(hout * hout, axis=-1, keepdims=True) + NORM_EPS) * g_row
    return hn * _sigmoid(o) * _silu(z)


def _mlstm_prompt_kernel(q_ref, k_ref, v_ref, o_ref, z_ref, col_ref, wrow_ref, gh_ref,
                         out_ref, s_out_ref, n_out_ref, s_ref, n_ref):
    c = pl.program_id(1)
    length = q_ref.shape[0]

    @pl.when(c == 0)
    def _():
        s_ref[...] = jnp.zeros_like(s_ref)
        n_ref[...] = jnp.zeros_like(n_ref)

    t_idx = lax.broadcasted_iota(jnp.int32, (length, length), 0)
    s_idx = lax.broadcasted_iota(jnp.int32, (length, length), 1)
    causal = s_idx <= t_idx
    col = col_ref[...]
    for h in range(M_HEADS):
        a_col = col[:, h:h + 1]
        e_col = col[:, M_HEADS + h:M_HEADS + h + 1]
        wt_col = col[:, 2 * M_HEADS + h:2 * M_HEADS + h + 1]
        u_col = col[:, 3 * M_HEADS + h:3 * M_HEADS + h + 1]
        a_last = a_col[length - 1:length, :]
        qf = q_ref[:, h * M_DK:(h + 1) * M_DK] * (M_DK ** -0.5)
        qb = qf.astype(BF16)
        kf = k_ref[:, h * M_DK:(h + 1) * M_DK]
        vb = v_ref[:, h * M_DV:(h + 1) * M_DV].astype(BF16)
        dmat = jnp.where(causal, jnp.exp(u_col + wrow_ref[h:h + 1, :]), 0.0)
        qk = lax.dot_general(qb, kf.astype(BF16), (((1,), (1,)), ((), ())), preferred_element_type=F32) * dmat
        s_old = s_ref[h]
        n_old = n_ref[h:h + 1, :]
        num = a_col * jnp.dot(qb, s_old.astype(BF16), preferred_element_type=F32) \
            + jnp.dot(qk.astype(BF16), vb, preferred_element_type=F32)
        den = a_col * jnp.sum(qf * n_old, axis=-1, keepdims=True) + jnp.sum(qk, axis=-1, keepdims=True)
        hout = num / jnp.maximum(jnp.abs(den), e_col)
        kw = kf * wt_col
        s_ref[h] = a_last * s_old + jnp.dot(kw.T.astype(BF16), vb, preferred_element_type=F32)
        n_ref[h:h + 1, :] = a_last * n_old + jnp.sum(kw, axis=0, keepdims=True)
        sl = slice(h * M_DV, (h + 1) * M_DV)
        out_ref[:, sl] = _head_norm_gate(hout, gh_ref[:, sl], o_ref[:, sl], z_ref[:, sl]).astype(out_ref.dtype)

    @pl.when(c == pl.num_programs(1) - 1)
    def _():
        s_out_ref[...] = s_ref[...]
        n_out_ref[...] = n_ref[...]


def _mlstm_prompt(proj, col, wrow, g_head, batch, seq):
    length = M_CHUNK
    nc = seq // length
    inner = M_HEADS * M_DV
    qkw = M_QK_WIDTH
    v_col = 2 * qkw // inner
    row = lambda b, c: b * nc + c
    return pl.pallas_call(
        _mlstm_prompt_kernel,
        grid=(batch, nc),
        in_specs=[pl.BlockSpec((length, qkw), lambda b, c: (row(b, c), 0)),
                  pl.BlockSpec((length, qkw), lambda b, c: (row(b, c), 1)),
                  pl.BlockSpec((length, inner), lambda b, c: (row(b, c), v_col)),
                  pl.BlockSpec((length, inner), lambda b, c: (row(b, c), v_col + 1)),
                  pl.BlockSpec((length, inner), lambda b, c: (row(b, c), v_col + 2)),
                  pl.BlockSpec((length, 4 * M_HEADS), lambda b, c: (row(b, c), 0)),
                  pl.BlockSpec((None, M_HEADS, length), lambda b, c: (row(b, c), 0, 0)),
                  pl.BlockSpec((1, inner), lambda b, c: (0, 0))],
        out_specs=[pl.BlockSpec((length, inner), lambda b, c: (row(b, c), 0)),
                   pl.BlockSpec((None, M_HEADS, M_DK, M_DV), lambda b, c: (b, 0, 0, 0)),
                   pl.BlockSpec((None, M_HEADS, M_DK), lambda b, c: (b, 0, 0))],
        out_shape=[jax.ShapeDtypeStruct((batch * seq, inner), BF16),
                   jax.ShapeDtypeStruct((batch, M_HEADS, M_DK, M_DV), F32),
                   jax.ShapeDtypeStruct((batch, M_HEADS, M_DK), F32)],
        scratch_shapes=[pltpu.VMEM((M_HEADS, M_DK, M_DV), F32), pltpu.VMEM((M_HEADS, M_DK), F32)],
        compiler_params=_params(("arbitrary", "arbitrary")),
        name="mlstm_prompt_scan",
    )(proj, proj, proj, proj, proj, col, wrow, g_head.reshape(1, inner))


def _prefix_scan(x, op, axis, n):
    idx = lax.broadcasted_iota(jnp.int32, x.shape, axis)
    take = (lambda t: x[t:t + 1, :]) if axis == 0 else (lambda t: x[:, t:t + 1])
    run = take(0)
    out = jnp.broadcast_to(run, x.shape)
    for t in range(1, n):
        run = op(run, take(t))
        out = jnp.where(idx >= t, run, out)
    return out


def _mlstm_sample_kernel(*refs, n_new, has_acc):
    (q_ref, k_ref, v_ref, o_ref, z_ref, g_ref, gt_ref, brow_ref, bcol_ref, mrow_ref, gh_ref,
     s_in_ref, n_in_ref) = refs[:13]
    out_ref, s_out_ref, n_out_ref, m_out_ref = refs[13 + has_acc:]
    tp = q_ref.shape[0]
    last = n_new - 1
    hs = M_HEADS
    g = g_ref[...] + brow_ref[...]
    bcum_c = _prefix_scan(_log_sigmoid(g[:, hs:]), jnp.add, 0, n_new)
    w_c = g[:, :hs] - bcum_c
    m_prev_c = mrow_ref[...]
    u_c = -jnp.maximum(m_prev_c, _prefix_scan(w_c, jnp.maximum, 0, n_new))
    a_c = jnp.exp(m_prev_c + u_c)
    e_c = jnp.exp(u_c - bcum_c)
    real_c = lax.broadcasted_iota(jnp.int32, (tp, hs), 0) < n_new
    wt_c = jnp.where(real_c, jnp.exp(w_c + u_c[last:last + 1, :]), 0.0)
    m_out_ref[...] = bcum_c[last:last + 1, :] - u_c[last:last + 1, :]
    gt = gt_ref[...] + bcol_ref[...]
    w_r = gt[:hs, :] - _prefix_scan(_log_sigmoid(gt[hs:, :]), jnp.add, 1, n_new)
    keys = w_r.shape[1]
    t_idx = lax.broadcasted_iota(jnp.int32, (tp, keys), 0)
    s_idx = lax.broadcasted_iota(jnp.int32, (tp, keys), 1)
    causal = s_idx <= t_idx
    k_pad = jnp.concatenate([k_ref[...], jnp.zeros((keys - tp, k_ref.shape[1]), F32)], axis=0)
    v_pad = jnp.concatenate([v_ref[...], jnp.zeros((keys - tp, v_ref.shape[1]), F32)], axis=0).astype(BF16)
    wt_pad = jnp.concatenate([wt_c, jnp.zeros((keys - tp, hs), F32)], axis=0)
    for h in range(hs):
        a_col = a_c[:, h:h + 1]
        a_last = a_col[last:last + 1, :]
        qf = q_ref[:, h * M_DK:(h + 1) * M_DK] * (M_DK ** -0.5)
        qb = qf.astype(BF16)
        kf = k_pad[:, h * M_DK:(h + 1) * M_DK]
        vb = v_pad[:, h * M_DV:(h + 1) * M_DV]
        dmat = jnp.where(causal, jnp.exp(u_c[:, h:h + 1] + w_r[h:h + 1, :]), 0.0)
        qk = lax.dot_general(qb, kf.astype(BF16), (((1,), (1,)), ((), ())), preferred_element_type=F32) * dmat
        s_old = s_in_ref[h]
        n_old = n_in_ref[h:h + 1, :]
        num = a_col * jnp.dot(qb, s_old.astype(BF16), preferred_element_type=F32) \
            + jnp.dot(qk.astype(BF16), vb, preferred_element_type=F32)
        den = a_col * jnp.sum(qf * n_old, axis=-1, keepdims=True) + jnp.sum(qk, axis=-1, keepdims=True)
        hout = num / jnp.maximum(jnp.abs(den), e_c[:, h:h + 1])
        kw = kf * wt_pad[:, h:h + 1]
        s_out_ref[h] = a_last * s_old + jnp.dot(kw.T.astype(BF16), vb, preferred_element_type=F32)
        n_out_ref[h:h + 1, :] = a_last * n_old + jnp.sum(kw, axis=0, keepdims=True)
        sl = slice(h * M_DV, (h + 1) * M_DV)
        out_ref[:, sl] = _head_norm_gate(hout, gh_ref[:, sl], o_ref[:, sl], z_ref[:, sl]).astype(out_ref.dtype)


def _mlstm_sample(proj3, gates3, gates3_t, b_if, state_c, state_n, state_m, g_head, layer, n_new, c_acc):
    nbatch, tp, _ = proj3.shape
    hs = M_HEADS
    inner = hs * M_DV
    qkw = M_QK_WIDTH
    v_col = 2 * qkw // inner
    keys = gates3_t.shape[2]
    m_row = state_m[layer].reshape(nbatch, 1, hs)
    has_acc = c_acc is not None
    in_specs = [pl.BlockSpec((None, tp, qkw), lambda b: (b, 0, 0)),
                pl.BlockSpec((None, tp, qkw), lambda b: (b, 0, 1)),
                pl.BlockSpec((None, tp, inner), lambda b: (b, 0, v_col)),
                pl.BlockSpec((None, tp, inner), lambda b: (b, 0, v_col + 1)),
                pl.BlockSpec((None, tp, inner), lambda b: (b, 0, v_col + 2)),
                pl.BlockSpec((None, tp, 2 * hs), lambda b: (b, 0, 0)),
                pl.BlockSpec((None, 2 * hs, keys), lambda b: (b, 0, 0)),
                pl.BlockSpec((1, 2 * hs), lambda b: (0, 0)),
                pl.BlockSpec((2 * hs, 1), lambda b: (0, 0)),
                pl.BlockSpec((None, 1, hs), lambda b: (b, 0, 0)),
                pl.BlockSpec((1, inner), lambda b: (0, 0)),
                pl.BlockSpec((None, None, hs, M_DK, M_DV), lambda b: (layer, b, 0, 0, 0)),
                pl.BlockSpec((None, None, hs, M_DK), lambda b: (layer, b, 0, 0))]
    args = [proj3, proj3, proj3, proj3, proj3, gates3, gates3_t, b_if.reshape(1, 2 * hs), b_if.reshape(2 * hs, 1),
            m_row, g_head.reshape(1, inner), state_c, state_n]
    if has_acc:
        in_specs.append(pl.BlockSpec(memory_space=pl.ANY))
        args.append(c_acc)
    return pl.pallas_call(
        functools.partial(_mlstm_sample_kernel, n_new=n_new, has_acc=has_acc),
        grid=(nbatch,),
        in_specs=in_specs,
        out_specs=[pl.BlockSpec((None, tp, inner), lambda b: (b, 0, 0)),
                   pl.BlockSpec((None, None, hs, M_DK, M_DV), lambda b: (layer, b, 0, 0, 0)),
                   pl.BlockSpec((None, hs, M_DK), lambda b: (b, 0, 0)),
                   pl.BlockSpec((None, 1, hs), lambda b: (b, 0, 0))],
        out_shape=[jax.ShapeDtypeStruct((nbatch, tp, inner), BF16),
                   jax.ShapeDtypeStruct(state_c.shape, F32),
                   jax.ShapeDtypeStruct((nbatch, hs, M_DK), F32),
                   jax.ShapeDtypeStruct((nbatch, 1, hs), F32)],
        input_output_aliases={len(args) - 1: 1} if has_acc else {},
        compiler_params=_params(("arbitrary",)),
        name="mlstm_sample_step",
    )(*args)


def kernel(x_prompt, x_sample, c_prompt, c_sample, cache_k, cache_v, state_C, state_n, state_m, w_ada, b_ada,
           g_pre, g_post, w_in_attn, sinks, w_out_attn, w_in_mlstm, b_if_mlstm, g_head_mlstm, w_out_mlstm):
    batch, seq, d = x_prompt.shape
    dec_batch, dec_seq, _ = x_sample.shape
    depth = w_ada.shape[0]
    tp = SAMPLE_ROWS
    rows_p = batch * seq
    rows_s = dec_batch * tp
    tm = 1024
    att_inner = ATT_Q_HEADS * ATT_HEAD_DIM
    att_cols = 2 * att_inner + 2 * ATT_KV_WIDTH
    m_cols = 2 * M_QK_WIDTH + 3 * M_HEADS * M_DV

    c_rows = batch + dec_batch
    c_pad = -c_rows % 8
    c_all = jnp.concatenate([c_prompt, c_sample, jnp.zeros((c_pad, d), F32)], axis=0)
    mod = _ada_all_layers(c_all, w_ada, b_ada)

    xp = x_prompt.reshape(rows_p, d)
    xs = jnp.pad(x_sample, ((0, 0), (0, tp - dec_seq), (0, 0))).reshape(rows_s, d)
    cache_kt = cache_k.transpose(0, 1, 3, 4, 2)
    cache_vt = cache_v.transpose(0, 1, 3, 4, 2)
    w_out_attn_b = w_out_attn.astype(BF16)
    w_out_mlstm_b = w_out_mlstm.astype(BF16)
    w_in_mlstm_t = w_in_mlstm.transpose(0, 2, 1)

    split3 = lambda m: (m[..., :d], m[..., d:2 * d], m[..., 2 * d:])
    mods_p = [split3(mod[l, :batch].reshape(batch, 1, 3 * d)) for l in range(depth)]
    mods_s = [split3(jnp.repeat(mod[l, batch:c_rows], tp, axis=0)) for l in range(depth)]

    hp = _prenorm(xp, mods_p[0][0], mods_p[0][1], g_pre[0], tm)
    hs = _prenorm(xs, mods_s[0][0][None], mods_s[0][1][None], g_pre[0], rows_s)
    kp_l, vp_l = [], []
    cp_l, np_l, mp_l, ns_l, ms_l = [], [], [], [], []
    c_sample_new = k_sample_new = v_sample_new = None
    for l in range(depth):
        j = l // 2
        if l % 2 == 0:
            pp, ps = _proj(hp, hs, w_in_attn, j, att_cols, tm, 1024)
            ap = _attn_prompt(pp, sinks[j], batch, seq)
            ps3 = ps.reshape(dec_batch, tp, att_cols)
            a_s, k_sample_new, v_sample_new = _attn_sample(ps3, sinks[j], cache_kt, cache_vt, j, dec_seq,
                                                           k_sample_new, v_sample_new)
            a_s = a_s.reshape(rows_s, att_inner)
            k0, v0 = att_inner, att_inner + ATT_KV_WIDTH
            pp3 = pp.reshape(batch, seq, att_cols)
            kv_shape = (ATT_KV_HEADS, ATT_HEAD_DIM)
            kp_l.append(pp3[:, seq - WINDOW:, k0:v0].reshape((batch, WINDOW) + kv_shape))
            vp_l.append(pp3[:, seq - WINDOW:, v0:v0 + ATT_KV_WIDTH].reshape((batch, WINDOW) + kv_shape))
            w_out = w_out_attn_b
        else:
            pp, ps = _proj(hp, hs, w_in_mlstm_t, j, m_cols, tm, 1024, w_is_transposed=True)
            gp, gs = _gate_proj(hp, hs, w_in_mlstm_t, j, m_cols, 2 * M_HEADS, tm)
            nc = seq // M_CHUNK
            gates_t = gp.T.reshape(2, M_HEADS, batch * nc, M_CHUNK)
            a_q, em_q, wt_q, u_q, w_q, mn_q = _gate_prep(gates_t, b_if_mlstm[j], nc)
            col = jnp.stack([a_q, em_q, wt_q, u_q]).transpose(2, 3, 0, 1).reshape(rows_p, 4 * M_HEADS)
            wrow = w_q.transpose(1, 0, 2)
            ap, c_new, n_new = _mlstm_prompt(pp, col, wrow, g_head_mlstm[j], batch, seq)
            cp_l.append(c_new)
            np_l.append(n_new)
            mp_l.append(mn_q.reshape(M_HEADS, batch, nc, M_CHUNK)[:, :, nc - 1, 0].T)
            ps3 = ps.reshape(dec_batch, tp, m_cols)
            gs3 = gs.reshape(dec_batch, tp, 2 * M_HEADS)
            gs3_t = jnp.pad(gs3.transpose(0, 2, 1), ((0, 0), (0, 0), (0, LANES - tp)))
            a_s, c_sample_new, n_new, m_new = _mlstm_sample(ps3, gs3, gs3_t, b_if_mlstm[j], state_C, state_n, state_m,
                                                            g_head_mlstm[j], j, dec_seq, c_sample_new)
            a_s = a_s.reshape(rows_s, M_HEADS * M_DV)
            ns_l.append(n_new)
            ms_l.append(m_new.reshape(dec_batch, M_HEADS))
            w_out = w_out_mlstm_b
        nxt = min(l + 1, depth - 1)
        mod_p = jnp.stack([mods_p[l][2], mods_p[nxt][0], mods_p[nxt][1]], axis=1)
        mod_s = jnp.stack([mods_s[l][2], mods_s[nxt][0], mods_s[nxt][1]], axis=0)
        outs = _out_proj(ap, a_s, w_out, j, xp, xs, mod_p, mod_s, g_post[l], g_pre[nxt], 512, emit_next=l + 1 < depth)
        xp, xs = outs[:2]
        if l + 1 < depth:
            hp, hs = outs[2:]

    y_prompt = xp.reshape(batch, seq, d)
    y_sample = xs.reshape(dec_batch, tp, d)[:, :dec_seq]
    to_cache_layout = lambda c: c.transpose(0, 1, 4, 2, 3)
    return (y_prompt, y_sample, jnp.stack(kp_l), jnp.stack(vp_l), to_cache_layout(k_sample_new),
            to_cache_layout(v_sample_new), jnp.stack(cp_l), jnp.stack(np_l), jnp.stack(mp_l), c_sample_new,
            jnp.stack(ns_l), jnp.stack(ms_l))
```

```python
import functools

import jax
import jax.numpy as jnp
from jax import lax
from jax.experimental import pallas as pl
from jax.experimental.pallas import tpu as pltpu

F32 = jnp.float32
BF16 = jnp.bfloat16

NORM_EPS = 1e-6
WINDOW = 128
ATT_HEAD_DIM = 64
ATT_KV_HEADS = 8
ATT_GROUP = 4
ATT_Q_HEADS = ATT_KV_HEADS * ATT_GROUP
ATT_KV_WIDTH = ATT_KV_HEADS * ATT_HEAD_DIM
M_HEADS = 8
M_DK = 128
M_DV = 256
M_QK_WIDTH = M_HEADS * M_DK
M_CHUNK = 128
SAMPLE_ROWS = 8
LANES = 128
LOG2E = 1.4426950408889634
OUT_PROJ_SUB_ROWS = 128

V7X_VMEM_LIMIT = 56 * 1024 * 1024


def _params(sem, vmem=V7X_VMEM_LIMIT):
    return pltpu.CompilerParams(dimension_semantics=sem, vmem_limit_bytes=vmem)


def _sigmoid(x):
    return 1.0 / (1.0 + jnp.exp(-x))


def _silu(x):
    return x * _sigmoid(x)


def _log_sigmoid(x):
    return jnp.minimum(x, 0.0) - jnp.log1p(jnp.exp(-jnp.abs(x)))


def _alibi_slope(head):
    return float(2.0 ** (-8.0 * (head + 1) / ATT_Q_HEADS))


def _ada_kernel(c_ref, w_ref, b_ref, o_ref):
    s = _silu(c_ref[...]).astype(BF16)
    o_ref[...] = jnp.dot(s, w_ref[...].astype(BF16), preferred_element_type=F32) + b_ref[...]


def _ada_all_layers(c_all, w_ada, b_ada, tn=1024):
    depth, d, n = w_ada.shape
    r = c_all.shape[0]
    return pl.pallas_call(
        _ada_kernel,
        grid=(depth, n // tn),
        in_specs=[
            pl.BlockSpec((r, d), lambda l, j: (0, 0)),
            pl.BlockSpec((None, d, tn), lambda l, j: (l, 0, j)),
            pl.BlockSpec((None, 1, tn), lambda l, j: (l, 0, j)),
        ],
        out_specs=pl.BlockSpec((None, r, tn), lambda l, j: (l, 0, j)),
        out_shape=jax.ShapeDtypeStruct((depth, r, n), F32),
        compiler_params=_params(("arbitrary", "arbitrary")),
        name="adaln_mod",
    )(c_all, w_ada, b_ada.reshape(depth, 1, n))


def _pre_norm_mod(x, shift, scale, g):
    y = x * lax.rsqrt(jnp.mean(x * x, axis=-1, keepdims=True) + NORM_EPS) * g
    return y * (1.0 + scale) + shift


def _prenorm_kernel(x_ref, sh_ref, sc_ref, g_ref, h_ref):
    h_ref[...] = _pre_norm_mod(x_ref[...], sh_ref[...], sc_ref[...], g_ref[...]).astype(h_ref.dtype)


def _prenorm(x, shift, scale, g, tm):
    m, d = x.shape
    groups, r, _ = shift.shape
    tiles_per_group = m // tm // groups
    mod_spec = pl.BlockSpec((None, r, d), lambda i: (i // tiles_per_group, 0, 0))
    return pl.pallas_call(
        _prenorm_kernel,
        grid=(m // tm,),
        in_specs=[pl.BlockSpec((tm, d), lambda i: (i, 0)), mod_spec, mod_spec,
                  pl.BlockSpec((1, d), lambda i: (0, 0))],
        out_specs=pl.BlockSpec((tm, d), lambda i: (i, 0)),
        out_shape=jax.ShapeDtypeStruct((m, d), BF16),
        compiler_params=_params(("arbitrary",)),
        name="prenorm_mod",
    )(x, shift, scale, g.reshape(1, d))


def _matmul(a, w, w_is_transposed):
    dims = (((1,), (1,)), ((), ())) if w_is_transposed else (((1,), (0,)), ((), ()))
    return lax.dot_general(a, w, dims, preferred_element_type=F32)


def _proj_kernel(h_ref, hs_ref, w_ref, o_ref, os_ref, wb_ref, *, w_is_transposed):
    @pl.when(pl.program_id(1) == 0)
    def _():
        wb_ref[...] = w_ref[...].astype(BF16)
        os_ref[...] = _matmul(hs_ref[...], wb_ref[...], w_is_transposed)

    o_ref[...] = _matmul(h_ref[...], wb_ref[...], w_is_transposed)


def _proj(h, hs, w_stack, layer, n_cols, tm, tn, w_is_transposed=False):
    m, d = h.shape
    ms = hs.shape[0]
    if w_is_transposed:
        w_spec = pl.BlockSpec((None, tn, d), lambda j, i: (layer, j, 0))
        w_tile = (tn, d)
    else:
        w_spec = pl.BlockSpec((None, d, tn), lambda j, i: (layer, 0, j))
        w_tile = (d, tn)
    return pl.pallas_call(
        functools.partial(_proj_kernel, w_is_transposed=w_is_transposed),
        grid=(n_cols // tn, m // tm),
        in_specs=[pl.BlockSpec((tm, d), lambda j, i: (i, 0)),
                  pl.BlockSpec((ms, d), lambda j, i: (0, 0)),
                  w_spec],
        out_specs=[pl.BlockSpec((tm, tn), lambda j, i: (i, j)),
                   pl.BlockSpec((ms, tn), lambda j, i: (0, j))],
        out_shape=[jax.ShapeDtypeStruct((m, n_cols), F32), jax.ShapeDtypeStruct((ms, n_cols), F32)],
        scratch_shapes=[pltpu.VMEM(w_tile, BF16)],
        compiler_params=_params(("arbitrary", "arbitrary")),
        name="in_proj",
    )(h, hs, w_stack)


def _gate_proj_kernel(h_ref, hs_ref, w_ref, o_ref, os_ref):
    wb = w_ref[...].astype(BF16)

    @pl.when(pl.program_id(0) == 0)
    def _():
        os_ref[...] = _matmul(hs_ref[...], wb, True)

    o_ref[...] = _matmul(h_ref[...], wb, True)


def _gate_proj(h, hs, wt_stack, layer, row0, n, tm):
    m, d = h.shape
    ms = hs.shape[0]
    return pl.pallas_call(
        _gate_proj_kernel,
        grid=(m // tm,),
        in_specs=[pl.BlockSpec((tm, d), lambda i: (i, 0)),
                  pl.BlockSpec((ms, d), lambda i: (0, 0)),
                  pl.BlockSpec((None, n, d), lambda i: (layer, row0 // n, 0))],
        out_specs=[pl.BlockSpec((tm, n), lambda i: (i, 0)), pl.BlockSpec((ms, n), lambda i: (0, 0))],
        out_shape=[jax.ShapeDtypeStruct((m, n), F32), jax.ShapeDtypeStruct((ms, n), F32)],
        compiler_params=_params(("arbitrary",)),
        name="mlstm_gate_proj",
    )(h, hs, wt_stack)


def _post_norm_residual(y, x, gate, g):
    return x + gate * (y * lax.rsqrt(jnp.mean(y * y, axis=-1, keepdims=True) + NORM_EPS) * g)


def _out_kernel(*refs, emit_next):
    a_ref, as_ref, w_ref, x_ref, xs_ref, gt_ref, gts_ref, g_ref, gn_ref = refs[:9]
    o_ref, os_ref = refs[9:11]

    def finish(a_in, x_in, mod, x_out, h_out):
        for r in range(0, a_in.shape[0], OUT_PROJ_SUB_ROWS):
            rows = slice(r, r + OUT_PROJ_SUB_ROWS)
            per_row = mod.shape[1] > 1
            y = jnp.dot(a_in[rows, :], w_ref[...], preferred_element_type=F32)
            x_new = _post_norm_residual(y, x_in[rows, :], mod[0, rows, :] if per_row else mod[0], g_ref[...])
            x_out[rows, :] = x_new
            if emit_next:
                shift, scale = (mod[1, rows, :], mod[2, rows, :]) if per_row else (mod[1], mod[2])
                h_out[rows, :] = _pre_norm_mod(x_new, shift, scale, gn_ref[...]).astype(h_out.dtype)

    @pl.when(pl.program_id(0) == 0)
    def _():
        finish(as_ref, xs_ref, gts_ref, os_ref, refs[12] if emit_next else None)

    finish(a_ref, x_ref, gt_ref, o_ref, refs[11] if emit_next else None)


def _out_proj(a, a_s, w_stack, layer, x, xs, mod, mod_s, g_post, g_pre_next, tm, emit_next):
    m, d_in = a.shape
    ms = a_s.shape[0]
    d = x.shape[1]
    tiles_per_seq = m // tm // mod.shape[0]
    once = pl.Buffered(1)
    row_spec = lambda width: pl.BlockSpec((tm, width), lambda i: (i, 0))
    fixed_spec = lambda width: pl.BlockSpec((ms, width), lambda i: (0, 0))
    out_specs = [row_spec(d), fixed_spec(d)]
    out_shape = [jax.ShapeDtypeStruct((m, d), F32), jax.ShapeDtypeStruct((ms, d), F32)]
    if emit_next:
        out_specs += [row_spec(d), fixed_spec(d)]
        out_shape += [jax.ShapeDtypeStruct((m, d), BF16), jax.ShapeDtypeStruct((ms, d), BF16)]
    return pl.pallas_call(
        functools.partial(_out_kernel, emit_next=emit_next),
        grid=(m // tm,),
        in_specs=[row_spec(d_in),
                  pl.BlockSpec((ms, d_in), lambda i: (0, 0), pipeline_mode=once),
                  pl.BlockSpec((None, d_in, d), lambda i: (layer, 0, 0), pipeline_mode=once),
                  row_spec(d),
                  pl.BlockSpec((ms, d), lambda i: (0, 0), pipeline_mode=once),
                  pl.BlockSpec((None, 3, 1, d), lambda i: (i // tiles_per_seq, 0, 0, 0)),
                  pl.BlockSpec((3, ms, d), lambda i: (0, 0, 0), pipeline_mode=once),
                  pl.BlockSpec((1, d), lambda i: (0, 0)),
                  pl.BlockSpec((1, d), lambda i: (0, 0))],
        out_specs=out_specs,
        out_shape=out_shape,
        compiler_params=_params(("arbitrary",)),
        name="out_proj_postnorm",
    )(a, a_s, w_stack, x, xs, mod, mod_s, g_post.reshape(1, d), g_pre_next.reshape(1, d))


def _group_select(group_col, values):
    out = values[ATT_GROUP - 1]
    for g in range(ATT_GROUP - 2, -1, -1):
        out = jnp.where(group_col == g, values[g], out)
    return out


def _attn_prompt_kernel(sink_ref, q_ref, kc_ref, kp_ref, vc_ref, vp_ref, z0_ref, z1_ref, o_ref, bias_ref):
    blk = WINDOW
    hd = ATT_HEAD_DIM
    i = pl.program_id(1)
    cols = ATT_GROUP * blk

    @pl.when(i <= 1)
    def _():
        key = lax.broadcasted_iota(jnp.int32, (2 * blk, cols), 0)
        qcol = lax.broadcasted_iota(jnp.int32, (2 * blk, cols), 1)
        dist = (qcol % blk) + blk - key
        valid = (dist >= 0) & (dist < WINDOW) & ((key >= blk) | (i > 0))
        dist_f = dist.astype(F32)
        group = qcol // blk
        for h in range(ATT_KV_HEADS):
            slope = _group_select(group, [_alibi_slope(ATT_GROUP * h + g) for g in range(ATT_GROUP)])
            bias_ref[h] = jnp.where(valid, (-slope * dist_f) * LOG2E, -jnp.inf)

    kcat = jnp.concatenate([kp_ref[...], kc_ref[...]], axis=0).astype(BF16)
    vcat = jnp.concatenate([vp_ref[...], vc_ref[...]], axis=0)
    group_row = lax.broadcasted_iota(jnp.int32, (1, cols), 1) // blk
    half = z0_ref.shape[1]
    vt_pairs = [jnp.concatenate([vcat[:blk, c * LANES:(c + 1) * LANES].T, vcat[blk:, c * LANES:(c + 1) * LANES].T],
                                axis=1) for c in range(ATT_KV_WIDTH // LANES)]
    for h in range(ATT_KV_HEADS):
        pair, odd = divmod(h, 2)
        lanes = slice(pair * LANES, (pair + 1) * LANES)
        qt = [(q_ref[:, (2 * h + t) * LANES:(2 * h + t + 1) * LANES] * (hd ** -0.5 * LOG2E)).T for t in range(2)]
        qt = jnp.concatenate([qt[0][:hd], qt[0][hd:], qt[1][:hd], qt[1][hd:]], axis=1).astype(BF16)
        zero = jnp.zeros_like(qt)
        rhs = jnp.concatenate([zero, qt] if odd else [qt, zero], axis=0)
        s = jnp.dot(kcat[:, lanes], rhs, preferred_element_type=F32) + bias_ref[h]
        sink = _group_select(group_row, [sink_ref[ATT_GROUP * h + g] for g in range(ATT_GROUP)]) * LOG2E
        mx = jnp.maximum(jnp.max(s, axis=0, keepdims=True), sink)
        p = jnp.exp2(s - mx).astype(BF16)
        vt = vt_pairs[pair][hd:] if odd else vt_pairs[pair][:hd]
        lhs = jnp.concatenate([vt, jnp.ones_like(vt)], axis=0).astype(BF16)
        oa = jnp.dot(lhs, p, preferred_element_type=F32)
        den = oa[hd:hd + 1] + jnp.exp2(sink - mx)
        on = oa[:hd] * (1.0 / den)
        for t in range(2):
            ot = jnp.concatenate([on[:, (2 * t) * blk:(2 * t + 1) * blk],
                                  on[:, (2 * t + 1) * blk:(2 * t + 2) * blk]], axis=0).T
            c0 = (2 * h + t) * LANES
            z_ref, zc = (z0_ref, c0) if c0 < half else (z1_ref, c0 - half)
            o_ref[:, c0:c0 + LANES] = (ot * _silu(z_ref[:, zc:zc + LANES])).astype(o_ref.dtype)


def _attn_prompt(proj, sinks, batch, seq):
    blk = WINDOW
    nb = seq // blk
    inner = ATT_Q_HEADS * ATT_HEAD_DIM
    kvw = ATT_KV_WIDTH
    k_col = inner // kvw
    v_col = k_col + 1
    half = inner // 2
    z_col = (inner + 2 * kvw) // half
    cur = lambda b, i: b * nb + i
    prev = lambda b, i: b * nb + jnp.maximum(i - 1, 0)
    return pl.pallas_call(
        _attn_prompt_kernel,
        grid=(batch, nb),
        in_specs=[pl.BlockSpec(memory_space=pltpu.SMEM),
                  pl.BlockSpec((blk, inner), lambda b, i: (cur(b, i), 0)),
                  pl.BlockSpec((blk, kvw), lambda b, i: (cur(b, i), k_col)),
                  pl.BlockSpec((blk, kvw), lambda b, i: (prev(b, i), k_col)),
                  pl.BlockSpec((blk, kvw), lambda b, i: (cur(b, i), v_col)),
                  pl.BlockSpec((blk, kvw), lambda b, i: (prev(b, i), v_col)),
                  pl.BlockSpec((blk, half), lambda b, i: (cur(b, i), z_col)),
                  pl.BlockSpec((blk, half), lambda b, i: (cur(b, i), z_col + 1))],
        out_specs=pl.BlockSpec((blk, inner), lambda b, i: (cur(b, i), 0)),
        out_shape=jax.ShapeDtypeStruct((batch * seq, inner), BF16),
        scratch_shapes=[pltpu.VMEM((ATT_KV_HEADS, 2 * blk, ATT_GROUP * blk), F32)],
        compiler_params=_params(("arbitrary", "arbitrary")),
        name="attn_prompt",
    )(sinks, proj, proj, proj, proj, proj, proj, proj)


def _attn_sample_kernel(*refs, n_new, has_acc):
    sink_ref, q_ref, kn_ref, vn_ref, z0_ref, z1_ref, kc_ref, vc_ref = refs[:8]
    o_ref, ko_ref, vo_ref, bias_ref = refs[8 + 2 * has_acc:]
    hd = ATT_HEAD_DIM
    tp = q_ref.shape[0]
    w = kc_ref.shape[2]
    rows = ATT_GROUP * tp

    @pl.when(pl.program_id(0) == 0)
    def _():
        t_row = lax.broadcasted_iota(jnp.int32, (rows, 2 * w), 0) % tp
        key = lax.broadcasted_iota(jnp.int32, (rows, 2 * w), 1)
        dist = t_row + w - key
        valid = (dist >= 0) & (dist < WINDOW) & (key < w + n_new)
        dist_f = dist.astype(F32)
        group = lax.broadcasted_iota(jnp.int32, (rows, 2 * w), 0) // tp
        for h in range(ATT_KV_HEADS):
            slope = _group_select(group, [_alibi_slope(ATT_GROUP * h + g) for g in range(ATT_GROUP)])
            bias_ref[h] = jnp.where(valid, -slope * dist_f, -jnp.inf)

    nt_dims = (((1,), (1,)), ((), ()))
    pad = jnp.zeros((w - tp, kn_ref.shape[1]), F32)
    kn_pad = jnp.concatenate([kn_ref[...], pad], axis=0)
    vn_pad = jnp.concatenate([vn_ref[...], pad], axis=0)
    kn_b = kn_pad.astype(BF16)
    vn_b = vn_pad.astype(BF16)
    n_pairs = ATT_KV_WIDTH // LANES
    knt_pairs = [kn_pad[:, c * LANES:(c + 1) * LANES].T for c in range(n_pairs)]
    vnt_pairs = [vn_pad[:, c * LANES:(c + 1) * LANES].T for c in range(n_pairs)]
    is_new = lax.broadcasted_iota(jnp.int32, (hd, w), 1) < n_new
    group_col = lax.broadcasted_iota(jnp.int32, (rows, 1), 0) // tp
    half = z0_ref.shape[1]
    for h in range(ATT_KV_HEADS):
        heads = [ATT_GROUP * h + g for g in range(ATT_GROUP)]
        pair, odd = divmod(h, 2)
        kt = kc_ref[h]
        vt = vc_ref[h]
        qs = (jnp.concatenate([q_ref[:, j * hd:(j + 1) * hd] for j in heads], axis=0) * (hd ** -0.5)).astype(BF16)
        s_old = jnp.dot(qs, kt.astype(BF16), preferred_element_type=F32)
        s_new = lax.dot_general(qs, kn_b[:, h * hd:(h + 1) * hd], nt_dims, preferred_element_type=F32)
        s = jnp.concatenate([s_old, s_new], axis=1) + bias_ref[h]
        sink = _group_select(group_col, [sink_ref[j] for j in heads])
        mx = jnp.maximum(jnp.max(s, axis=-1, keepdims=True), sink)
        p = jnp.exp(s - mx)
        den = jnp.sum(p, axis=-1, keepdims=True) + jnp.exp(sink - mx)
        pb = p.astype(BF16)
        o = lax.dot_general(pb[:, :w], vt.astype(BF16), nt_dims, preferred_element_type=F32) \
            + jnp.dot(pb[:, w:], vn_b[:, h * hd:(h + 1) * hd], preferred_element_type=F32)
        o = o * (1.0 / den)
        for g, j in enumerate(heads):
            c0 = j * hd
            z_ref, zc = (z0_ref, c0) if c0 < half else (z1_ref, c0 - half)
            z = z_ref[:, zc:zc + hd]
            o_ref[:, c0:c0 + hd] = (o[g * tp:(g + 1) * tp] * _silu(z)).astype(o_ref.dtype)
        knt = knt_pairs[pair][odd * hd:(odd + 1) * hd]
        vnt = vnt_pairs[pair][odd * hd:(odd + 1) * hd]
        ko_ref[h] = pltpu.roll(jnp.where(is_new, knt, kt), w - n_new, axis=1)
        vo_ref[h] = pltpu.roll(jnp.where(is_new, vnt, vt), w - n_new, axis=1)


def _attn_sample(proj3, sinks, cache_kt, cache_vt, layer, n_new, k_acc, v_acc):
    nbatch, tp, _ = proj3.shape
    w = cache_kt.shape[4]
    inner = ATT_Q_HEADS * ATT_HEAD_DIM
    kvw = ATT_KV_WIDTH
    k_col = inner // kvw
    half = inner // 2
    z_col = (inner + 2 * kvw) // half
    cache_spec = pl.BlockSpec((None, None, ATT_KV_HEADS, ATT_HEAD_DIM, w), lambda b: (layer, b, 0, 0, 0))
    has_acc = k_acc is not None
    in_specs = [pl.BlockSpec(memory_space=pltpu.SMEM),
                pl.BlockSpec((None, tp, inner), lambda b: (b, 0, 0)),
                pl.BlockSpec((None, tp, kvw), lambda b: (b, 0, k_col)),
                pl.BlockSpec((None, tp, kvw), lambda b: (b, 0, k_col + 1)),
                pl.BlockSpec((None, tp, half), lambda b: (b, 0, z_col)),
                pl.BlockSpec((None, tp, half), lambda b: (b, 0, z_col + 1)),
                cache_spec, cache_spec]
    args = [sinks, proj3, proj3, proj3, proj3, proj3, cache_kt, cache_vt]
    if has_acc:
        in_specs += [pl.BlockSpec(memory_space=pl.ANY)] * 2
        args += [k_acc, v_acc]
    return pl.pallas_call(
        functools.partial(_attn_sample_kernel, n_new=n_new, has_acc=has_acc),
        grid=(nbatch,),
        in_specs=in_specs,
        out_specs=[pl.BlockSpec((None, tp, inner), lambda b: (b, 0, 0)), cache_spec, cache_spec],
        out_shape=[jax.ShapeDtypeStruct((nbatch, tp, inner), BF16),
                   jax.ShapeDtypeStruct(cache_kt.shape, F32), jax.ShapeDtypeStruct(cache_vt.shape, F32)],
        scratch_shapes=[pltpu.VMEM((ATT_KV_HEADS, ATT_GROUP * tp, 2 * w), F32)],
        input_output_aliases={len(args) - 2: 1, len(args) - 1: 2} if has_acc else {},
        compiler_params=_params(("arbitrary",)),
        name="attn_sample",
    )(*args)


def _lane_scan(x, op, lane):
    n = x.shape[-1]
    shift = 1
    while shift < n:
        x = jnp.where(lane >= shift, op(x, pltpu.roll(x, shift, axis=x.ndim - 1)), x)
        shift *= 2
    return x


def _gate_prep_kernel(bias_ref, g_ref, a_ref, em_ref, wt_ref, u_ref, w_ref, mn_ref, *, chunks_per_seq):
    rows, length = g_ref.shape[2], g_ref.shape[3]
    lane = lax.broadcasted_iota(jnp.int32, (rows, length), 1)
    chunk = lax.broadcasted_iota(jnp.int32, (rows, length), 0) % chunks_per_seq
    for h in range(M_HEADS):
        li = g_ref[0, h] + bias_ref[h]
        lf = _log_sigmoid(g_ref[1, h] + bias_ref[M_HEADS + h])
        bcum = _lane_scan(lf, jnp.add, lane)
        w = li - bcum
        cmax = _lane_scan(w, jnp.maximum, lane)
        e = jnp.broadcast_to(bcum[:, length - 1:length], (rows, length))
        y = e + jnp.broadcast_to(cmax[:, length - 1:length], (rows, length))
        shift = 1
        while shift < chunks_per_seq:
            e_prev = pltpu.roll(e, shift, axis=0)
            y_prev = pltpu.roll(y, shift, axis=0)
            take = chunk >= shift
            y = jnp.where(take, jnp.maximum(y_prev + e, y), y)
            e = jnp.where(take, e_prev + e, e)
            shift *= 2
        m_incl = jnp.maximum(e, y)
        m_prev = jnp.where(chunk >= 1, pltpu.roll(m_incl, 1, axis=0), 0.0)
        u = -jnp.maximum(m_prev, cmax)
        u_last = jnp.broadcast_to(u[:, length - 1:length], (rows, length))
        a_ref[h] = jnp.exp(m_prev + u)
        em_ref[h] = jnp.exp(u - bcum)
        wt_ref[h] = jnp.exp(w + u_last)
        u_ref[h] = u
        w_ref[h] = w
        mn_ref[h] = m_incl


def _gate_prep(gates_t, b_if, chunks_per_seq):
    shape = gates_t.shape[1:]
    out = jax.ShapeDtypeStruct(shape, F32)
    return pl.pallas_call(
        functools.partial(_gate_prep_kernel, chunks_per_seq=chunks_per_seq),
        in_specs=[pl.BlockSpec(memory_space=pltpu.SMEM), pl.BlockSpec(memory_space=pltpu.VMEM)],
        out_specs=[pl.BlockSpec(memory_space=pltpu.VMEM)] * 6,
        out_shape=[out] * 6,
        name="mlstm_gate_prep",
    )(b_if, gates_t)


def _head_norm_gate(hout, g_row, o, z):
    hn = hout * lax.rsqrt(jnp.mean(hout * hout, axis=-1, keepdims=True) + NORM_EPS) * g_row
    return hn * (z / ((1.0 + jnp.exp(-o)) * (1.0 + jnp.exp(-z))))


def _mlstm_prompt_kernel(q_ref, k_ref, v_ref, o_ref, z_ref, col_ref, row_ref, gh_ref,
                         out_ref, s_out_ref, n_out_ref, sn_ref):
    c = pl.program_id(1)
    length = q_ref.shape[0]
    hs = M_HEADS

    @pl.when(c == 0)
    def _():
        sn_ref[...] = jnp.zeros_like(sn_ref)

    t_idx = lax.broadcasted_iota(jnp.int32, (length, length), 0)
    s_idx = lax.broadcasted_iota(jnp.int32, (length, length), 1)
    causal = s_idx <= t_idx
    col = col_ref[...]
    ones = jnp.ones((length, LANES), BF16)
    nt_dims = (((1,), (1,)), ((), ()))
    for h in range(hs):
        a_b = jnp.broadcast_to(col[:, h:h + 1], (length, LANES))
        e_b = jnp.broadcast_to(col[:, hs + h:hs + h + 1], (length, LANES))
        u_b = jnp.broadcast_to(col[:, 2 * hs + h:2 * hs + h + 1], (length, length))
        a_last = a_b[length - 1:length, :1]
        qb = (q_ref[:, h * M_DK:(h + 1) * M_DK] * (M_DK ** -0.5)).astype(BF16)
        kf = k_ref[:, h * M_DK:(h + 1) * M_DK]
        v1 = jnp.concatenate([v_ref[:, h * M_DV:(h + 1) * M_DV].astype(BF16), ones], axis=1)
        dmat = jnp.where(causal, jnp.exp(u_b + row_ref[h:h + 1, :]), 0.0)
        qk = lax.dot_general(qb, kf.astype(BF16), nt_dims, preferred_element_type=F32) * dmat
        sn_old = sn_ref[h]
        inter = jnp.dot(qb, sn_old.astype(BF16), preferred_element_type=F32)
        intra = jnp.dot(qk.astype(BF16), v1, preferred_element_type=F32)
        a_b3 = jnp.concatenate([a_b] * (1 + M_DV // LANES), axis=1)
        tot = a_b3 * inter + intra
        inv = 1.0 / jnp.maximum(jnp.abs(tot[:, M_DV:]), e_b)
        hout = tot[:, :M_DV] * jnp.concatenate([inv] * (M_DV // LANES), axis=1)
        kwt = (kf.T * row_ref[hs + h:hs + h + 1, :]).astype(BF16)
        sn_ref[h] = a_last * sn_old + jnp.dot(kwt, v1, preferred_element_type=F32)
        sl = slice(h * M_DV, (h + 1) * M_DV)
        out_ref[:, sl] = _head_norm_gate(hout, gh_ref[:, sl], o_ref[:, sl], z_ref[:, sl]).astype(out_ref.dtype)

    @pl.when(c == pl.num_programs(1) - 1)
    def _():
        for h in range(hs):
            s_out_ref[h] = sn_ref[h, :, :M_DV]
            n_out_ref[h:h + 1, :] = sn_ref[h, :, M_DV:].T[:1, :]


def _mlstm_prompt(proj, col, rows, g_head, batch, seq):
    length = M_CHUNK
    nc = seq // length
    inner = M_HEADS * M_DV
    qkw = M_QK_WIDTH
    v_col = 2 * qkw // inner
    row = lambda b, c: b * nc + c
    return pl.pallas_call(
        _mlstm_prompt_kernel,
        grid=(batch, nc),
        in_specs=[pl.BlockSpec((length, qkw), lambda b, c: (row(b, c), 0)),
                  pl.BlockSpec((length, qkw), lambda b, c: (row(b, c), 1)),
                  pl.BlockSpec((length, inner), lambda b, c: (row(b, c), v_col)),
                  pl.BlockSpec((length, inner), lambda b, c: (row(b, c), v_col + 1)),
                  pl.BlockSpec((length, inner), lambda b, c: (row(b, c), v_col + 2)),
                  pl.BlockSpec((length, 3 * M_HEADS), lambda b, c: (row(b, c), 0)),
                  pl.BlockSpec((None, 2 * M_HEADS, length), lambda b, c: (row(b, c), 0, 0)),
                  pl.BlockSpec((1, inner), lambda b, c: (0, 0))],
        out_specs=[pl.BlockSpec((length, inner), lambda b, c: (row(b, c), 0)),
                   pl.BlockSpec((None, M_HEADS, M_DK, M_DV), lambda b, c: (b, 0, 0, 0)),
                   pl.BlockSpec((None, M_HEADS, M_DK), lambda b, c: (b, 0, 0))],
        out_shape=[jax.ShapeDtypeStruct((batch * seq, inner), BF16),
                   jax.ShapeDtypeStruct((batch, M_HEADS, M_DK, M_DV), F32),
                   jax.ShapeDtypeStruct((batch, M_HEADS, M_DK), F32)],
        scratch_shapes=[pltpu.VMEM((M_HEADS, M_DK, M_DV + LANES), F32)],
        compiler_params=_params(("arbitrary", "arbitrary")),
        name="mlstm_prompt_scan",
    )(proj, proj, proj, proj, proj, col, rows, g_head.reshape(1, inner))


def _prefix_scan(x, op, axis, n):
    idx = lax.broadcasted_iota(jnp.int32, x.shape, axis)
    take = (lambda t: x[t:t + 1, :]) if axis == 0 else (lambda t: x[:, t:t + 1])
    run = take(0)
    out = jnp.broadcast_to(run, x.shape)
    for t in range(1, n):
        run = op(run, take(t))
        out = jnp.where(idx >= t, run, out)
    return out


def _mlstm_sample_kernel(*refs, n_new, has_acc):
    (q_ref, k_ref, v_ref, o_ref, z_ref, g_ref, gt_ref, brow_ref, bcol_ref, mrow_ref, gh_ref,
     s_in_ref, n_in_ref) = refs[:13]
    out_ref, s_out_ref, n_out_ref, m_out_ref = refs[13 + has_acc:]
    tp = q_ref.shape[0]
    last = n_new - 1
    hs = M_HEADS
    g = g_ref[...] + brow_ref[...]
    bcum_c = _prefix_scan(_log_sigmoid(g[:, hs:]), jnp.add, 0, n_new)
    w_c = g[:, :hs] - bcum_c
    m_prev_c = mrow_ref[...]
    u_c = -jnp.maximum(m_prev_c, _prefix_scan(w_c, jnp.maximum, 0, n_new))
    a_c = jnp.exp(m_prev_c + u_c)
    e_c = jnp.exp(u_c - bcum_c)
    real_c = lax.broadcasted_iota(jnp.int32, (tp, hs), 0) < n_new
    wt_c = jnp.where(real_c, jnp.exp(w_c + u_c[last:last + 1, :]), 0.0)
    m_out_ref[...] = bcum_c[last:last + 1, :] - u_c[last:last + 1, :]
    gt = gt_ref[...] + bcol_ref[...]
    w_r = gt[:hs, :] - _prefix_scan(_log_sigmoid(gt[hs:, :]), jnp.add, 1, n_new)
    keys = w_r.shape[1]
    t_idx = lax.broadcasted_iota(jnp.int32, (tp, keys), 0)
    s_idx = lax.broadcasted_iota(jnp.int32, (tp, keys), 1)
    causal = s_idx <= t_idx
    k_pad = jnp.concatenate([k_ref[...], jnp.zeros((keys - tp, k_ref.shape[1]), F32)], axis=0)
    v_pad = jnp.concatenate([v_ref[...], jnp.zeros((keys - tp, v_ref.shape[1]), F32)], axis=0).astype(BF16)
    wt_pad = jnp.concatenate([wt_c, jnp.zeros((keys - tp, hs), F32)], axis=0)
    for h in range(hs):
        a_col = a_c[:, h:h + 1]
        a_last = a_col[last:last + 1, :]
        qf = q_ref[:, h * M_DK:(h + 1) * M_DK] * (M_DK ** -0.5)
        qb = qf.astype(BF16)
        kf = k_pad[:, h * M_DK:(h + 1) * M_DK]
        vb = v_pad[:, h * M_DV:(h + 1) * M_DV]
        dmat = jnp.where(causal, jnp.exp(u_c[:, h:h + 1] + w_r[h:h + 1, :]), 0.0)
        qk = lax.dot_general(qb, kf.astype(BF16), (((1,), (1,)), ((), ())), preferred_element_type=F32) * dmat
        s_old = s_in_ref[h]
        n_old = n_in_ref[h:h + 1, :]
        num = a_col * jnp.dot(qb, s_old.astype(BF16), preferred_element_type=F32) \
            + jnp.dot(qk.astype(BF16), vb, preferred_element_type=F32)
        den = a_col * jnp.sum(qf * n_old, axis=-1, keepdims=True) + jnp.sum(qk, axis=-1, keepdims=True)
        hout = num / jnp.maximum(jnp.abs(den), e_c[:, h:h + 1])
        kw = kf * wt_pad[:, h:h + 1]
        s_out_ref[h] = a_last * s_old + jnp.dot(kw.T.astype(BF16), vb, preferred_element_type=F32)
        n_out_ref[h:h + 1, :] = a_last * n_old + jnp.sum(kw, axis=0, keepdims=True)
        sl = slice(h * M_DV, (h + 1) * M_DV)
        out_ref[:, sl] = _head_norm_gate(hout, gh_ref[:, sl], o_ref[:, sl], z_ref[:, sl]).astype(out_ref.dtype)


def _mlstm_sample(proj3, gates3, gates3_t, b_if, state_c, state_n, state_m, g_head, layer, n_new, c_acc):
    nbatch, tp, _ = proj3.shape
    hs = M_HEADS
    inner = hs * M_DV
    qkw = M_QK_WIDTH
    v_col = 2 * qkw // inner
    keys = gates3_t.shape[2]
    m_row = state_m[layer].reshape(nbatch, 1, hs)
    has_acc = c_acc is not None
    in_specs = [pl.BlockSpec((None, tp, qkw), lambda b: (b, 0, 0)),
                pl.BlockSpec((None, tp, qkw), lambda b: (b, 0, 1)),
                pl.BlockSpec((None, tp, inner), lambda b: (b, 0, v_col)),
                pl.BlockSpec((None, tp, inner), lambda b: (b, 0, v_col + 1)),
                pl.BlockSpec((None, tp, inner), lambda b: (b, 0, v_col + 2)),
                pl.BlockSpec((None, tp, 2 * hs), lambda b: (b, 0, 0)),
                pl.BlockSpec((None, 2 * hs, keys), lambda b: (b, 0, 0)),
                pl.BlockSpec((1, 2 * hs), lambda b: (0, 0)),
                pl.BlockSpec((2 * hs, 1), lambda b: (0, 0)),
                pl.BlockSpec((None, 1, hs), lambda b: (b, 0, 0)),
                pl.BlockSpec((1, inner), lambda b: (0, 0)),
                pl.BlockSpec((None, None, hs, M_DK, M_DV), lambda b: (layer, b, 0, 0, 0)),
                pl.BlockSpec((None, None, hs, M_DK), lambda b: (layer, b, 0, 0))]
    args = [proj3, proj3, proj3, proj3, proj3, gates3, gates3_t, b_if.reshape(1, 2 * hs), b_if.reshape(2 * hs, 1),
            m_row, g_head.reshape(1, inner), state_c, state_n]
    if has_acc:
        in_specs.append(pl.BlockSpec(memory_space=pl.ANY))
        args.append(c_acc)
    return pl.pallas_call(
        functools.partial(_mlstm_sample_kernel, n_new=n_new, has_acc=has_acc),
        grid=(nbatch,),
        in_specs=in_specs,
        out_specs=[pl.BlockSpec((None, tp, inner), lambda b: (b, 0, 0)),
                   pl.BlockSpec((None, None, hs, M_DK, M_DV), lambda b: (layer, b, 0, 0, 0)),
                   pl.BlockSpec((None, hs, M_DK), lambda b: (b, 0, 0)),
                   pl.BlockSpec((None, 1, hs), lambda b: (b, 0, 0))],
        out_shape=[jax.ShapeDtypeStruct((nbatch, tp, inner), BF16),
                   jax.ShapeDtypeStruct(state_c.shape, F32),
                   jax.ShapeDtypeStruct((nbatch, hs, M_DK), F32),
                   jax.ShapeDtypeStruct((nbatch, 1, hs), F32)],
        input_output_aliases={len(args) - 1: 1} if has_acc else {},
        compiler_params=_params(("arbitrary",)),
        name="mlstm_sample_step",
    )(*args)


def kernel(x_prompt, x_sample, c_prompt, c_sample, cache_k, cache_v, state_C, state_n, state_m, w_ada, b_ada,
           g_pre, g_post, w_in_attn, sinks, w_out_attn, w_in_mlstm, b_if_mlstm, g_head_mlstm, w_out_mlstm):
    batch, seq, d = x_prompt.shape
    dec_batch, dec_seq, _ = x_sample.shape
    depth = w_ada.shape[0]
    tp = SAMPLE_ROWS
    rows_p = batch * seq
    rows_s = dec_batch * tp
    tm = 1024
    att_inner = ATT_Q_HEADS * ATT_HEAD_DIM
    att_cols = 2 * att_inner + 2 * ATT_KV_WIDTH
    m_cols = 2 * M_QK_WIDTH + 3 * M_HEADS * M_DV

    c_rows = batch + dec_batch
    c_pad = -c_rows % 8
    c_all = jnp.concatenate([c_prompt, c_sample, jnp.zeros((c_pad, d), F32)], axis=0)
    mod = _ada_all_layers(c_all, w_ada, b_ada)

    xp = x_prompt.reshape(rows_p, d)
    xs = jnp.pad(x_sample, ((0, 0), (0, tp - dec_seq), (0, 0))).reshape(rows_s, d)
    cache_kt = cache_k.transpose(0, 1, 3, 4, 2)
    cache_vt = cache_v.transpose(0, 1, 3, 4, 2)
    w_out_attn_b = w_out_attn.astype(BF16)
    w_out_mlstm_b = w_out_mlstm.astype(BF16)
    w_in_mlstm_t = w_in_mlstm.transpose(0, 2, 1)

    split3 = lambda m: (m[..., :d], m[..., d:2 * d], m[..., 2 * d:])
    mods_p = [split3(mod[l, :batch].reshape(batch, 1, 3 * d)) for l in range(depth)]
    mods_s = [split3(jnp.repeat(mod[l, batch:c_rows], tp, axis=0)) for l in range(depth)]

    hp = _prenorm(xp, mods_p[0][0], mods_p[0][1], g_pre[0], tm)
    hs = _prenorm(xs, mods_s[0][0][None], mods_s[0][1][None], g_pre[0], rows_s)
    kp_l, vp_l = [], []
    cp_l, np_l, mp_l, ns_l, ms_l = [], [], [], [], []
    c_sample_new = k_sample_new = v_sample_new = None
    for l in range(depth):
        j = l // 2
        if l % 2 == 0:
            pp, ps = _proj(hp, hs, w_in_attn, j, att_cols, tm, 1024)
            ap = _attn_prompt(pp, sinks[j], batch, seq)
            ps3 = ps.reshape(dec_batch, tp, att_cols)
            a_s, k_sample_new, v_sample_new = _attn_sample(ps3, sinks[j], cache_kt, cache_vt, j, dec_seq,
                                                           k_sample_new, v_sample_new)
            a_s = a_s.reshape(rows_s, att_inner)
            k0, v0 = att_inner, att_inner + ATT_KV_WIDTH
            pp3 = pp.reshape(batch, seq, att_cols)
            kv_shape = (ATT_KV_HEADS, ATT_HEAD_DIM)
            kp_l.append(pp3[:, seq - WINDOW:, k0:v0].reshape((batch, WINDOW) + kv_shape))
            vp_l.append(pp3[:, seq - WINDOW:, v0:v0 + ATT_KV_WIDTH].reshape((batch, WINDOW) + kv_shape))
            w_out = w_out_attn_b
        else:
            pp, ps = _proj(hp, hs, w_in_mlstm_t, j, m_cols, tm, 1024, w_is_transposed=True)
            gp, gs = _gate_proj(hp, hs, w_in_mlstm_t, j, m_cols, 2 * M_HEADS, tm)
            nc = seq // M_CHUNK
            gates_t = gp.T.reshape(2, M_HEADS, batch * nc, M_CHUNK)
            a_q, em_q, wt_q, u_q, w_q, mn_q = _gate_prep(gates_t, b_if_mlstm[j], nc)
            col = jnp.stack([a_q, em_q, u_q]).transpose(2, 3, 0, 1).reshape(rows_p, 3 * M_HEADS)
            rows = jnp.concatenate([w_q, wt_q], axis=0).transpose(1, 0, 2)
            ap, c_new, n_new = _mlstm_prompt(pp, col, rows, g_head_mlstm[j], batch, seq)
            cp_l.append(c_new)
            np_l.append(n_new)
            mp_l.append(mn_q.reshape(M_HEADS, batch, nc, M_CHUNK)[:, :, nc - 1, 0].T)
            ps3 = ps.reshape(dec_batch, tp, m_cols)
            gs3 = gs.reshape(dec_batch, tp, 2 * M_HEADS)
            gs3_t = jnp.pad(gs3.transpose(0, 2, 1), ((0, 0), (0, 0), (0, LANES - tp)))
            a_s, c_sample_new, n_new, m_new = _mlstm_sample(ps3, gs3, gs3_t, b_if_mlstm[j], state_C, state_n, state_m,
                                                            g_head_mlstm[j], j, dec_seq, c_sample_new)
            a_s = a_s.reshape(rows_s, M_HEADS * M_DV)
            ns_l.append(n_new)
            ms_l.append(m_new.reshape(dec_batch, M_HEADS))
            w_out = w_out_mlstm_b
        nxt = min(l + 1, depth - 1)
        mod_p = jnp.stack([mods_p[l][2], mods_p[nxt][0], mods_p[nxt][1]], axis=1)
        mod_s = jnp.stack([mods_s[l][2], mods_s[nxt][0], mods_s[nxt][1]], axis=0)
        outs = _out_proj(ap, a_s, w_out, j, xp, xs, mod_p, mod_s, g_post[l], g_pre[nxt], 512, emit_next=l + 1 < depth)
        xp, xs = outs[:2]
        if l + 1 < depth:
            hp, hs = outs[2:]

    y_prompt = xp.reshape(batch, seq, d)
    y_sample = xs.reshape(dec_batch, tp, d)[:, :dec_seq]
    to_cache_layout = lambda c: c.transpose(0, 1, 4, 2, 3)
    return (y_prompt, y_sample, jnp.stack(kp_l), jnp.stack(vp_l), to_cache_layout(k_sample_new),
            to_cache_layout(v_sample_new), jnp.stack(cp_l), jnp.stack(np_l), jnp.stack(mp_l), c_sample_new,
            jnp.stack(ns_l), jnp.stack(ms_l))
```

```python
import functools

import jax
import jax.numpy as jnp
from jax import lax
from jax.experimental import pallas as pl
from jax.experimental.pallas import tpu as pltpu

F32 = jnp.float32
BF16 = jnp.bfloat16

NORM_EPS = 1e-6
WINDOW = 128
ATT_HEAD_DIM = 64
ATT_KV_HEADS = 8
ATT_GROUP = 4
ATT_Q_HEADS = ATT_KV_HEADS * ATT_GROUP
ATT_KV_WIDTH = ATT_KV_HEADS * ATT_HEAD_DIM
M_HEADS = 8
M_DK = 128
M_DV = 256
M_QK_WIDTH = M_HEADS * M_DK
M_CHUNK = 128
SAMPLE_ROWS = 8
LANES = 128
LOG2E = 1.4426950408889634
OUT_PROJ_SUB_ROWS = 128
ATTN_SAMPLE_SEQS_PER_STEP = 1
MLSTM_SAMPLE_SEQS_PER_STEP = 2

V7X_VMEM_LIMIT = 56 * 1024 * 1024


def _params(sem, vmem=V7X_VMEM_LIMIT):
    return pltpu.CompilerParams(dimension_semantics=sem, vmem_limit_bytes=vmem)


def _sigmoid(x):
    return 1.0 / (1.0 + jnp.exp2(x * -LOG2E))


def _silu(x):
    return x * _sigmoid(x)


def _log_sigmoid(x):
    return jnp.minimum(x, 0.0) - jnp.log1p(jnp.exp(-jnp.abs(x)))


def _alibi_slope(head):
    return float(2.0 ** (-8.0 * (head + 1) / ATT_Q_HEADS))


def _ada_kernel(c_ref, w_ref, b_ref, o_ref):
    s = _silu(c_ref[...]).astype(BF16)
    o_ref[...] = jnp.dot(s, w_ref[...].astype(BF16), preferred_element_type=F32) + b_ref[...]


def _ada_all_layers(c_all, w_ada, b_ada, tn=1024):
    depth, d, n = w_ada.shape
    r = c_all.shape[0]
    return pl.pallas_call(
        _ada_kernel,
        grid=(depth, n // tn),
        in_specs=[
            pl.BlockSpec((r, d), lambda l, j: (0, 0)),
            pl.BlockSpec((None, d, tn), lambda l, j: (l, 0, j)),
            pl.BlockSpec((None, 1, tn), lambda l, j: (l, 0, j)),
        ],
        out_specs=pl.BlockSpec((None, r, tn), lambda l, j: (l, 0, j)),
        out_shape=jax.ShapeDtypeStruct((depth, r, n), F32),
        compiler_params=_params(("arbitrary", "arbitrary")),
        name="adaln_mod",
    )(c_all, w_ada, b_ada.reshape(depth, 1, n))


def _pre_norm_mod(x, shift, scale, g):
    y = x * lax.rsqrt(jnp.mean(x * x, axis=-1, keepdims=True) + NORM_EPS) * g
    return y * (1.0 + scale) + shift


def _prenorm_kernel(x_ref, sh_ref, sc_ref, g_ref, h_ref):
    h_ref[...] = _pre_norm_mod(x_ref[...], sh_ref[...], sc_ref[...], g_ref[...]).astype(h_ref.dtype)


def _prenorm(x, shift, scale, g, tm):
    m, d = x.shape
    groups, r, _ = shift.shape
    tiles_per_group = m // tm // groups
    mod_spec = pl.BlockSpec((None, r, d), lambda i: (i // tiles_per_group, 0, 0))
    return pl.pallas_call(
        _prenorm_kernel,
        grid=(m // tm,),
        in_specs=[pl.BlockSpec((tm, d), lambda i: (i, 0)), mod_spec, mod_spec,
                  pl.BlockSpec((1, d), lambda i: (0, 0))],
        out_specs=pl.BlockSpec((tm, d), lambda i: (i, 0)),
        out_shape=jax.ShapeDtypeStruct((m, d), BF16),
        compiler_params=_params(("arbitrary",)),
        name="prenorm_mod",
    )(x, shift, scale, g.reshape(1, d))


def _matmul(a, w, w_is_transposed):
    dims = (((1,), (1,)), ((), ())) if w_is_transposed else (((1,), (0,)), ((), ()))
    return lax.dot_general(a, w, dims, preferred_element_type=F32)


def _proj_kernel(h_ref, hs_ref, w_ref, o_ref, os_ref, wb_ref, *, w_is_transposed):
    @pl.when(pl.program_id(1) == 0)
    def _():
        wb_ref[...] = w_ref[...].astype(BF16)
        os_ref[...] = _matmul(hs_ref[...], wb_ref[...], w_is_transposed)

    o_ref[...] = _matmul(h_ref[...], wb_ref[...], w_is_transposed)


def _proj(h, hs, w_stack, layer, n_cols, tm, tn, w_is_transposed=False):
    m, d = h.shape
    ms = hs.shape[0]
    if w_is_transposed:
        w_spec = pl.BlockSpec((None, tn, d), lambda j, i: (layer, j, 0))
        w_tile = (tn, d)
    else:
        w_spec = pl.BlockSpec((None, d, tn), lambda j, i: (layer, 0, j))
        w_tile = (d, tn)
    return pl.pallas_call(
        functools.partial(_proj_kernel, w_is_transposed=w_is_transposed),
        grid=(n_cols // tn, m // tm),
        in_specs=[pl.BlockSpec((tm, d), lambda j, i: (i, 0)),
                  pl.BlockSpec((ms, d), lambda j, i: (0, 0)),
                  w_spec],
        out_specs=[pl.BlockSpec((tm, tn), lambda j, i: (i, j)),
                   pl.BlockSpec((ms, tn), lambda j, i: (0, j))],
        out_shape=[jax.ShapeDtypeStruct((m, n_cols), F32), jax.ShapeDtypeStruct((ms, n_cols), F32)],
        scratch_shapes=[pltpu.VMEM(w_tile, BF16)],
        compiler_params=_params(("arbitrary", "arbitrary")),
        name="in_proj",
    )(h, hs, w_stack)


def _gate_proj_kernel(h_ref, hs_ref, w_ref, o_ref, os_ref):
    wb = w_ref[...].astype(BF16)

    @pl.when(pl.program_id(0) == 0)
    def _():
        os_ref[...] = _matmul(hs_ref[...], wb, True)

    o_ref[...] = _matmul(h_ref[...], wb, True)


def _gate_proj(h, hs, wt_stack, layer, row0, n, tm):
    m, d = h.shape
    ms = hs.shape[0]
    return pl.pallas_call(
        _gate_proj_kernel,
        grid=(m // tm,),
        in_specs=[pl.BlockSpec((tm, d), lambda i: (i, 0)),
                  pl.BlockSpec((ms, d), lambda i: (0, 0)),
                  pl.BlockSpec((None, n, d), lambda i: (layer, row0 // n, 0))],
        out_specs=[pl.BlockSpec((tm, n), lambda i: (i, 0)), pl.BlockSpec((ms, n), lambda i: (0, 0))],
        out_shape=[jax.ShapeDtypeStruct((m, n), F32), jax.ShapeDtypeStruct((ms, n), F32)],
        compiler_params=_params(("arbitrary",)),
        name="mlstm_gate_proj",
    )(h, hs, wt_stack)


def _post_norm_residual(y, x, gate, g):
    return x + gate * (y * lax.rsqrt(jnp.mean(y * y, axis=-1, keepdims=True) + NORM_EPS) * g)


def _out_kernel(*refs, emit_next):
    a_ref, as_ref, w_ref, x_ref, xs_ref, gt_ref, gts_ref, g_ref, gn_ref = refs[:9]
    o_ref, os_ref = refs[9:11]

    def finish(a_in, x_in, mod, x_out, h_out):
        for r in range(0, a_in.shape[0], OUT_PROJ_SUB_ROWS):
            rows = slice(r, r + OUT_PROJ_SUB_ROWS)
            per_row = mod.shape[1] > 1
            y = jnp.dot(a_in[rows, :], w_ref[...], preferred_element_type=F32)
            x_new = _post_norm_residual(y, x_in[rows, :], mod[0, rows, :] if per_row else mod[0], g_ref[...])
            x_out[rows, :] = x_new
            if emit_next:
                shift, scale = (mod[1, rows, :], mod[2, rows, :]) if per_row else (mod[1], mod[2])
                h_out[rows, :] = _pre_norm_mod(x_new, shift, scale, gn_ref[...]).astype(h_out.dtype)

    @pl.when(pl.program_id(0) == 0)
    def _():
        finish(as_ref, xs_ref, gts_ref, os_ref, refs[12] if emit_next else None)

    finish(a_ref, x_ref, gt_ref, o_ref, refs[11] if emit_next else None)


def _out_proj(a, a_s, w_stack, layer, x, xs, mod, mod_s, g_post, g_pre_next, tm, emit_next):
    m, d_in = a.shape
    ms = a_s.shape[0]
    d = x.shape[1]
    tiles_per_seq = m // tm // mod.shape[0]
    once = pl.Buffered(1)
    row_spec = lambda width: pl.BlockSpec((tm, width), lambda i: (i, 0))
    fixed_spec = lambda width: pl.BlockSpec((ms, width), lambda i: (0, 0))
    out_specs = [row_spec(d), fixed_spec(d)]
    out_shape = [jax.ShapeDtypeStruct((m, d), F32), jax.ShapeDtypeStruct((ms, d), F32)]
    if emit_next:
        out_specs += [row_spec(d), fixed_spec(d)]
        out_shape += [jax.ShapeDtypeStruct((m, d), BF16), jax.ShapeDtypeStruct((ms, d), BF16)]
    return pl.pallas_call(
        functools.partial(_out_kernel, emit_next=emit_next),
        grid=(m // tm,),
        in_specs=[row_spec(d_in),
                  pl.BlockSpec((ms, d_in), lambda i: (0, 0), pipeline_mode=once),
                  pl.BlockSpec((None, d_in, d), lambda i: (layer, 0, 0), pipeline_mode=once),
                  row_spec(d),
                  pl.BlockSpec((ms, d), lambda i: (0, 0), pipeline_mode=once),
                  pl.BlockSpec((None, 3, 1, d), lambda i: (i // tiles_per_seq, 0, 0, 0)),
                  pl.BlockSpec((3, ms, d), lambda i: (0, 0, 0), pipeline_mode=once),
                  pl.BlockSpec((1, d), lambda i: (0, 0)),
                  pl.BlockSpec((1, d), lambda i: (0, 0))],
        out_specs=out_specs,
        out_shape=out_shape,
        compiler_params=_params(("arbitrary",)),
        name="out_proj_postnorm",
    )(a, a_s, w_stack, x, xs, mod, mod_s, g_post.reshape(1, d), g_pre_next.reshape(1, d))


def _group_select(group_col, values):
    out = values[ATT_GROUP - 1]
    for g in range(ATT_GROUP - 2, -1, -1):
        out = jnp.where(group_col == g, values[g], out)
    return out


def _attn_prompt_kernel(sink_ref, q_ref, kc_ref, kp_ref, vc_ref, vp_ref, z0_ref, z1_ref, o_ref, bias_ref):
    blk = WINDOW
    hd = ATT_HEAD_DIM
    i = pl.program_id(1)
    cols = ATT_GROUP * blk

    @pl.when(i <= 1)
    def _():
        key = lax.broadcasted_iota(jnp.int32, (2 * blk, cols), 0)
        qcol = lax.broadcasted_iota(jnp.int32, (2 * blk, cols), 1)
        dist = (qcol % blk) + blk - key
        valid = (dist >= 0) & (dist < WINDOW) & ((key >= blk) | (i > 0))
        dist_f = dist.astype(F32)
        group = qcol // blk
        for h in range(ATT_KV_HEADS):
            slope = _group_select(group, [_alibi_slope(ATT_GROUP * h + g) for g in range(ATT_GROUP)])
            bias_ref[h] = jnp.where(valid, (-slope * dist_f) * LOG2E, -jnp.inf)

    kcat = jnp.concatenate([kp_ref[...], kc_ref[...]], axis=0).astype(BF16)
    vcat = jnp.concatenate([vp_ref[...], vc_ref[...]], axis=0)
    half = z0_ref.shape[1]
    vt_pairs = [jnp.concatenate([vcat[:blk, c * LANES:(c + 1) * LANES].T, vcat[blk:, c * LANES:(c + 1) * LANES].T],
                                axis=1) for c in range(ATT_KV_WIDTH // LANES)]
    zero = jnp.zeros((hd, blk), BF16)
    for h in range(ATT_KV_HEADS):
        pair, odd = divmod(h, 2)
        k_pair = kcat[:, pair * LANES:(pair + 1) * LANES]
        vt = vt_pairs[pair][hd:] if odd else vt_pairs[pair][:hd]
        lhs = jnp.concatenate([vt, jnp.ones_like(vt)], axis=0).astype(BF16)
        for t in range(2):
            c0 = (2 * h + t) * LANES
            qt_pair = (q_ref[:, c0:c0 + LANES] * (hd ** -0.5 * LOG2E)).T
            outs = []
            for e in range(2):
                g = 2 * t + e
                qt = qt_pair[e * hd:(e + 1) * hd].astype(BF16)
                rhs = jnp.concatenate([zero, qt] if odd else [qt, zero], axis=0)
                s = jnp.dot(k_pair, rhs, preferred_element_type=F32) + bias_ref[h, :, g * blk:(g + 1) * blk]
                sink = sink_ref[ATT_GROUP * h + g] * LOG2E
                mx = jnp.maximum(jnp.max(s, axis=0, keepdims=True), sink)
                p = jnp.exp2(s - mx).astype(BF16)
                oa = jnp.dot(lhs, p, preferred_element_type=F32)
                den = oa[hd:hd + 1] + jnp.exp2(sink - mx)
                outs.append(oa[:hd] * (1.0 / den))
            ot = jnp.concatenate(outs, axis=0).T
            z_ref, zc = (z0_ref, c0) if c0 < half else (z1_ref, c0 - half)
            o_ref[:, c0:c0 + LANES] = (ot * _silu(z_ref[:, zc:zc + LANES])).astype(o_ref.dtype)


def _attn_prompt(proj, sinks, batch, seq):
    blk = WINDOW
    nb = seq // blk
    inner = ATT_Q_HEADS * ATT_HEAD_DIM
    kvw = ATT_KV_WIDTH
    k_col = inner // kvw
    v_col = k_col + 1
    half = inner // 2
    z_col = (inner + 2 * kvw) // half
    cur = lambda b, i: b * nb + i
    prev = lambda b, i: b * nb + jnp.maximum(i - 1, 0)
    return pl.pallas_call(
        _attn_prompt_kernel,
        grid=(batch, nb),
        in_specs=[pl.BlockSpec(memory_space=pltpu.SMEM),
                  pl.BlockSpec((blk, inner), lambda b, i: (cur(b, i), 0)),
                  pl.BlockSpec((blk, kvw), lambda b, i: (cur(b, i), k_col)),
                  pl.BlockSpec((blk, kvw), lambda b, i: (prev(b, i), k_col)),
                  pl.BlockSpec((blk, kvw), lambda b, i: (cur(b, i), v_col)),
                  pl.BlockSpec((blk, kvw), lambda b, i: (prev(b, i), v_col)),
                  pl.BlockSpec((blk, half), lambda b, i: (cur(b, i), z_col)),
                  pl.BlockSpec((blk, half), lambda b, i: (cur(b, i), z_col + 1))],
        out_specs=pl.BlockSpec((blk, inner), lambda b, i: (cur(b, i), 0)),
        out_shape=jax.ShapeDtypeStruct((batch * seq, inner), BF16),
        scratch_shapes=[pltpu.VMEM((ATT_KV_HEADS, 2 * blk, ATT_GROUP * blk), F32)],
        compiler_params=_params(("arbitrary", "arbitrary")),
        name="attn_prompt",
    )(sinks, proj, proj, proj, proj, proj, proj, proj)


def _attn_sample_kernel(*refs, n_new, has_acc):
    sink_ref, bias_ref = refs[0], refs[-1]
    seq_refs = refs[1:8] + refs[8 + 2 * has_acc:-1]
    n_seq, tp = refs[1].shape[:2]
    w = refs[6].shape[3]
    rows = ATT_GROUP * tp

    @pl.when(pl.program_id(0) == 0)
    def _():
        t_row = lax.broadcasted_iota(jnp.int32, (rows, 2 * w), 0) % tp
        key = lax.broadcasted_iota(jnp.int32, (rows, 2 * w), 1)
        dist = t_row + w - key
        valid = (dist >= 0) & (dist < WINDOW) & (key < w + n_new)
        dist_f = dist.astype(F32)
        group = lax.broadcasted_iota(jnp.int32, (rows, 2 * w), 0) // tp
        for h in range(ATT_KV_HEADS):
            slope = _group_select(group, [_alibi_slope(ATT_GROUP * h + g) for g in range(ATT_GROUP)])
            bias_ref[h] = jnp.where(valid, -slope * dist_f, -jnp.inf)

    for sq in range(n_seq):
        _attn_sample_seq(sink_ref, bias_ref, *[r.at[sq] for r in seq_refs], n_new=n_new)


def _attn_sample_seq(sink_ref, bias_ref, q_ref, kn_ref, vn_ref, z0_ref, z1_ref, kc_ref, vc_ref, o_ref, ko_ref, vo_ref,
                     *, n_new):
    hd = ATT_HEAD_DIM
    tp = q_ref.shape[0]
    w = kc_ref.shape[2]
    rows = ATT_GROUP * tp
    nt_dims = (((1,), (1,)), ((), ()))
    pad = jnp.zeros((w - tp, kn_ref.shape[1]), F32)
    kn_pad = jnp.concatenate([kn_ref[...], pad], axis=0)
    vn_pad = jnp.concatenate([vn_ref[...], pad], axis=0)
    kn_b = kn_pad.astype(BF16)
    vn_b = vn_pad.astype(BF16)
    n_pairs = ATT_KV_WIDTH // LANES
    knt_pairs = [kn_pad[:, c * LANES:(c + 1) * LANES].T for c in range(n_pairs)]
    vnt_pairs = [vn_pad[:, c * LANES:(c + 1) * LANES].T for c in range(n_pairs)]
    is_new = lax.broadcasted_iota(jnp.int32, (hd, w), 1) < n_new
    group_col = lax.broadcasted_iota(jnp.int32, (rows, 1), 0) // tp
    half = z0_ref.shape[1]
    for h in range(ATT_KV_HEADS):
        heads = [ATT_GROUP * h + g for g in range(ATT_GROUP)]
        pair, odd = divmod(h, 2)
        kt = kc_ref[h]
        vt = vc_ref[h]
        qs = (jnp.concatenate([q_ref[:, j * hd:(j + 1) * hd] for j in heads], axis=0) * (hd ** -0.5)).astype(BF16)
        s_old = jnp.dot(qs, kt.astype(BF16), preferred_element_type=F32)
        s_new = lax.dot_general(qs, kn_b[:, h * hd:(h + 1) * hd], nt_dims, preferred_element_type=F32)
        s = jnp.concatenate([s_old, s_new], axis=1) + bias_ref[h]
        sink = _group_select(group_col, [sink_ref[j] for j in heads])
        mx = jnp.maximum(jnp.max(s, axis=-1, keepdims=True), sink)
        p = jnp.exp(s - mx)
        den = jnp.sum(p, axis=-1, keepdims=True) + jnp.exp(sink - mx)
        pb = p.astype(BF16)
        o = lax.dot_general(pb[:, :w], vt.astype(BF16), nt_dims, preferred_element_type=F32) \
            + jnp.dot(pb[:, w:], vn_b[:, h * hd:(h + 1) * hd], preferred_element_type=F32)
        o = o * (1.0 / den)
        for g, j in enumerate(heads):
            c0 = j * hd
            z_ref, zc = (z0_ref, c0) if c0 < half else (z1_ref, c0 - half)
            z = z_ref[:, zc:zc + hd]
            o_ref[:, c0:c0 + hd] = (o[g * tp:(g + 1) * tp] * _silu(z)).astype(o_ref.dtype)
        knt = knt_pairs[pair][odd * hd:(odd + 1) * hd]
        vnt = vnt_pairs[pair][odd * hd:(odd + 1) * hd]
        ko_ref[h] = pltpu.roll(jnp.where(is_new, knt, kt), w - n_new, axis=1)
        vo_ref[h] = pltpu.roll(jnp.where(is_new, vnt, vt), w - n_new, axis=1)


def _attn_sample(proj3, sinks, cache_kt, cache_vt, layer, n_new, k_acc, v_acc):
    nbatch, tp, _ = proj3.shape
    w = cache_kt.shape[4]
    inner = ATT_Q_HEADS * ATT_HEAD_DIM
    kvw = ATT_KV_WIDTH
    k_col = inner // kvw
    half = inner // 2
    z_col = (inner + 2 * kvw) // half
    ns = ATTN_SAMPLE_SEQS_PER_STEP
    cache_spec = pl.BlockSpec((None, ns, ATT_KV_HEADS, ATT_HEAD_DIM, w), lambda b: (layer, b, 0, 0, 0))
    has_acc = k_acc is not None
    in_specs = [pl.BlockSpec(memory_space=pltpu.SMEM),
                pl.BlockSpec((ns, tp, inner), lambda b: (b, 0, 0)),
                pl.BlockSpec((ns, tp, kvw), lambda b: (b, 0, k_col)),
                pl.BlockSpec((ns, tp, kvw), lambda b: (b, 0, k_col + 1)),
                pl.BlockSpec((ns, tp, half), lambda b: (b, 0, z_col)),
                pl.BlockSpec((ns, tp, half), lambda b: (b, 0, z_col + 1)),
                cache_spec, cache_spec]
    args = [sinks, proj3, proj3, proj3, proj3, proj3, cache_kt, cache_vt]
    if has_acc:
        in_specs += [pl.BlockSpec(memory_space=pl.ANY)] * 2
        args += [k_acc, v_acc]
    return pl.pallas_call(
        functools.partial(_attn_sample_kernel, n_new=n_new, has_acc=has_acc),
        grid=(nbatch // ns,),
        in_specs=in_specs,
        out_specs=[pl.BlockSpec((ns, tp, inner), lambda b: (b, 0, 0)), cache_spec, cache_spec],
        out_shape=[jax.ShapeDtypeStruct((nbatch, tp, inner), BF16),
                   jax.ShapeDtypeStruct(cache_kt.shape, F32), jax.ShapeDtypeStruct(cache_vt.shape, F32)],
        scratch_shapes=[pltpu.VMEM((ATT_KV_HEADS, ATT_GROUP * tp, 2 * w), F32)],
        input_output_aliases={len(args) - 2: 1, len(args) - 1: 2} if has_acc else {},
        compiler_params=_params(("arbitrary",)),
        name="attn_sample",
    )(*args)


def _lane_scan(x, op, lane):
    n = x.shape[-1]
    shift = 1
    while shift < n:
        x = jnp.where(lane >= shift, op(x, pltpu.roll(x, shift, axis=x.ndim - 1)), x)
        shift *= 2
    return x


def _gate_prep_kernel(bias_ref, g_ref, a_ref, em_ref, wt_ref, u_ref, w_ref, mn_ref, *, chunks_per_seq):
    rows, length = g_ref.shape[2], g_ref.shape[3]
    lane = lax.broadcasted_iota(jnp.int32, (rows, length), 1)
    chunk = lax.broadcasted_iota(jnp.int32, (rows, length), 0) % chunks_per_seq
    for h in range(M_HEADS):
        li = g_ref[0, h] + bias_ref[h]
        lf = _log_sigmoid(g_ref[1, h] + bias_ref[M_HEADS + h])
        bcum = _lane_scan(lf, jnp.add, lane)
        w = li - bcum
        cmax = _lane_scan(w, jnp.maximum, lane)
        e = jnp.broadcast_to(bcum[:, length - 1:length], (rows, length))
        y = e + jnp.broadcast_to(cmax[:, length - 1:length], (rows, length))
        shift = 1
        while shift < chunks_per_seq:
            e_prev = pltpu.roll(e, shift, axis=0)
            y_prev = pltpu.roll(y, shift, axis=0)
            take = chunk >= shift
            y = jnp.where(take, jnp.maximum(y_prev + e, y), y)
            e = jnp.where(take, e_prev + e, e)
            shift *= 2
        m_incl = jnp.maximum(e, y)
        m_prev = jnp.where(chunk >= 1, pltpu.roll(m_incl, 1, axis=0), 0.0)
        u = -jnp.maximum(m_prev, cmax)
        u_last = jnp.broadcast_to(u[:, length - 1:length], (rows, length))
        a_ref[h] = jnp.exp(m_prev + u)
        em_ref[h] = jnp.exp(u - bcum)
        wt_ref[h] = jnp.exp(w + u_last)
        u_ref[h] = u
        w_ref[h] = w
        mn_ref[h] = m_incl


def _gate_prep(gates_t, b_if, chunks_per_seq):
    shape = gates_t.shape[1:]
    out = jax.ShapeDtypeStruct(shape, F32)
    return pl.pallas_call(
        functools.partial(_gate_prep_kernel, chunks_per_seq=chunks_per_seq),
        in_specs=[pl.BlockSpec(memory_space=pltpu.SMEM), pl.BlockSpec(memory_space=pltpu.VMEM)],
        out_specs=[pl.BlockSpec(memory_space=pltpu.VMEM)] * 6,
        out_shape=[out] * 6,
        name="mlstm_gate_prep",
    )(b_if, gates_t)


def _head_norm_gate(hout, g_row, o, z):
    hn = hout * lax.rsqrt(jnp.mean(hout * hout, axis=-1, keepdims=True) + NORM_EPS) * g_row
    return hn * (z / ((1.0 + jnp.exp2(o * -LOG2E)) * (1.0 + jnp.exp2(z * -LOG2E))))


def _mlstm_prompt_kernel(q_ref, k_ref, v_ref, o_ref, z_ref, col_ref, row_ref, gh_ref,
                         out_ref, s_out_ref, n_out_ref, sn_ref):
    c = pl.program_id(1)
    length = q_ref.shape[0]
    hs = M_HEADS

    @pl.when(c == 0)
    def _():
        sn_ref[...] = jnp.zeros_like(sn_ref)

    t_idx = lax.broadcasted_iota(jnp.int32, (length, length), 0)
    s_idx = lax.broadcasted_iota(jnp.int32, (length, length), 1)
    causal = s_idx <= t_idx
    col = col_ref[...]
    ones = jnp.ones((length, LANES), BF16)
    nt_dims = (((1,), (1,)), ((), ()))
    for h in range(hs):
        a_b = jnp.broadcast_to(col[:, h:h + 1], (length, LANES))
        e_b = jnp.broadcast_to(col[:, hs + h:hs + h + 1], (length, LANES))
        u_b = jnp.broadcast_to(col[:, 2 * hs + h:2 * hs + h + 1], (length, length))
        a_last = a_b[length - 1:length, :1]
        qb = (q_ref[:, h * M_DK:(h + 1) * M_DK] * (M_DK ** -0.5)).astype(BF16)
        kf = k_ref[:, h * M_DK:(h + 1) * M_DK]
        v1 = jnp.concatenate([v_ref[:, h * M_DV:(h + 1) * M_DV].astype(BF16), ones], axis=1)
        dmat = jnp.where(causal, jnp.exp(u_b + row_ref[h:h + 1, :]), 0.0)
        qk = lax.dot_general(qb, kf.astype(BF16), nt_dims, preferred_element_type=F32) * dmat
        sn_old = sn_ref[h]
        inter = jnp.dot(qb, sn_old.astype(BF16), preferred_element_type=F32)
        intra = jnp.dot(qk.astype(BF16), v1, preferred_element_type=F32)
        a_b3 = jnp.concatenate([a_b] * (1 + M_DV // LANES), axis=1)
        tot = a_b3 * inter + intra
        inv = 1.0 / jnp.maximum(jnp.abs(tot[:, M_DV:]), e_b)
        hout = tot[:, :M_DV] * jnp.concatenate([inv] * (M_DV // LANES), axis=1)
        kwt = (kf.T * row_ref[hs + h:hs + h + 1, :]).astype(BF16)
        sn_ref[h] = a_last * sn_old + jnp.dot(kwt, v1, preferred_element_type=F32)
        sl = slice(h * M_DV, (h + 1) * M_DV)
        out_ref[:, sl] = _head_norm_gate(hout, gh_ref[:, sl], o_ref[:, sl], z_ref[:, sl]).astype(out_ref.dtype)

    @pl.when(c == pl.num_programs(1) - 1)
    def _():
        for h in range(hs):
            s_out_ref[h] = sn_ref[h, :, :M_DV]
            n_out_ref[h:h + 1, :] = sn_ref[h, :, M_DV:].T[:1, :]


def _mlstm_prompt(proj, col, rows, g_head, batch, seq):
    length = M_CHUNK
    nc = seq // length
    inner = M_HEADS * M_DV
    qkw = M_QK_WIDTH
    v_col = 2 * qkw // inner
    row = lambda b, c: b * nc + c
    return pl.pallas_call(
        _mlstm_prompt_kernel,
        grid=(batch, nc),
        in_specs=[pl.BlockSpec((length, qkw), lambda b, c: (row(b, c), 0)),
                  pl.BlockSpec((length, qkw), lambda b, c: (row(b, c), 1)),
                  pl.BlockSpec((length, inner), lambda b, c: (row(b, c), v_col)),
                  pl.BlockSpec((length, inner), lambda b, c: (row(b, c), v_col + 1)),
                  pl.BlockSpec((length, inner), lambda b, c: (row(b, c), v_col + 2)),
                  pl.BlockSpec((length, 3 * M_HEADS), lambda b, c: (row(b, c), 0)),
                  pl.BlockSpec((None, 2 * M_HEADS, length), lambda b, c: (row(b, c), 0, 0)),
                  pl.BlockSpec((1, inner), lambda b, c: (0, 0))],
        out_specs=[pl.BlockSpec((length, inner), lambda b, c: (row(b, c), 0)),
                   pl.BlockSpec((None, M_HEADS, M_DK, M_DV), lambda b, c: (b, 0, 0, 0)),
                   pl.BlockSpec((None, M_HEADS, M_DK), lambda b, c: (b, 0, 0))],
        out_shape=[jax.ShapeDtypeStruct((batch * seq, inner), BF16),
                   jax.ShapeDtypeStruct((batch, M_HEADS, M_DK, M_DV), F32),
                   jax.ShapeDtypeStruct((batch, M_HEADS, M_DK), F32)],
        scratch_shapes=[pltpu.VMEM((M_HEADS, M_DK, M_DV + LANES), F32)],
        compiler_params=_params(("arbitrary", "arbitrary")),
        name="mlstm_prompt_scan",
    )(proj, proj, proj, proj, proj, col, rows, g_head.reshape(1, inner))


def _prefix_scan(x, op, axis, n):
    idx = lax.broadcasted_iota(jnp.int32, x.shape, axis)
    take = (lambda t: x[t:t + 1, :]) if axis == 0 else (lambda t: x[:, t:t + 1])
    run = take(0)
    out = jnp.broadcast_to(run, x.shape)
    for t in range(1, n):
        run = op(run, take(t))
        out = jnp.where(idx >= t, run, out)
    return out


def _mlstm_sample_kernel(*refs, n_new, has_acc):
    ins, outs = refs[:13], refs[13 + has_acc:]
    shared = (7, 8, 10)
    for sq in range(ins[0].shape[0]):
        seq_ins = [r if i in shared else r.at[sq] for i, r in enumerate(ins)]
        _mlstm_sample_seq(*seq_ins, *[r.at[sq] for r in outs], n_new=n_new)


def _mlstm_sample_seq(q_ref, k_ref, v_ref, o_ref, z_ref, g_ref, gt_ref, brow_ref, bcol_ref, mrow_ref, gh_ref,
                      s_in_ref, n_in_ref, out_ref, s_out_ref, n_out_ref, m_out_ref, *, n_new):
    tp = q_ref.shape[0]
    last = n_new - 1
    hs = M_HEADS
    g = g_ref[...] + brow_ref[...]
    bcum_c = _prefix_scan(_log_sigmoid(g[:, hs:]), jnp.add, 0, n_new)
    w_c = g[:, :hs] - bcum_c
    m_prev_c = mrow_ref[...]
    u_c = -jnp.maximum(m_prev_c, _prefix_scan(w_c, jnp.maximum, 0, n_new))
    a_c = jnp.exp(m_prev_c + u_c)
    e_c = jnp.exp(u_c - bcum_c)
    real_c = lax.broadcasted_iota(jnp.int32, (tp, hs), 0) < n_new
    wt_c = jnp.where(real_c, jnp.exp(w_c + u_c[last:last + 1, :]), 0.0)
    m_out_ref[...] = bcum_c[last:last + 1, :] - u_c[last:last + 1, :]
    gt = gt_ref[...] + bcol_ref[...]
    w_r = gt[:hs, :] - _prefix_scan(_log_sigmoid(gt[hs:, :]), jnp.add, 1, n_new)
    keys = w_r.shape[1]
    t_idx = lax.broadcasted_iota(jnp.int32, (tp, keys), 0)
    s_idx = lax.broadcasted_iota(jnp.int32, (tp, keys), 1)
    causal = s_idx <= t_idx
    k_pad = jnp.concatenate([k_ref[...], jnp.zeros((keys - tp, k_ref.shape[1]), F32)], axis=0)
    v_pad = jnp.concatenate([v_ref[...], jnp.zeros((keys - tp, v_ref.shape[1]), F32)], axis=0).astype(BF16)
    wt_pad = jnp.concatenate([wt_c, jnp.zeros((keys - tp, hs), F32)], axis=0)
    for h in range(hs):
        a_col = a_c[:, h:h + 1]
        a_last = a_col[last:last + 1, :]
        qf = q_ref[:, h * M_DK:(h + 1) * M_DK] * (M_DK ** -0.5)
        qb = qf.astype(BF16)
        kf = k_pad[:, h * M_DK:(h + 1) * M_DK]
        vb = v_pad[:, h * M_DV:(h + 1) * M_DV]
        dmat = jnp.where(causal, jnp.exp(u_c[:, h:h + 1] + w_r[h:h + 1, :]), 0.0)
        qk = lax.dot_general(qb, kf.astype(BF16), (((1,), (1,)), ((), ())), preferred_element_type=F32) * dmat
        s_old = s_in_ref[h]
        n_old = n_in_ref[h:h + 1, :]
        num = a_col * jnp.dot(qb, s_old.astype(BF16), preferred_element_type=F32) \
            + jnp.dot(qk.astype(BF16), vb, preferred_element_type=F32)
        den = a_col * jnp.sum(qf * n_old, axis=-1, keepdims=True) + jnp.sum(qk, axis=-1, keepdims=True)
        hout = num / jnp.maximum(jnp.abs(den), e_c[:, h:h + 1])
        kw = kf * wt_pad[:, h:h + 1]
        s_out_ref[h] = a_last * s_old + jnp.dot(kw.T.astype(BF16), vb, preferred_element_type=F32)
        n_out_ref[h:h + 1, :] = a_last * n_old + jnp.sum(kw, axis=0, keepdims=True)
        sl = slice(h * M_DV, (h + 1) * M_DV)
        out_ref[:, sl] = _head_norm_gate(hout, gh_ref[:, sl], o_ref[:, sl], z_ref[:, sl]).astype(out_ref.dtype)


def _mlstm_sample(proj3, gates3, gates3_t, b_if, state_c, state_n, state_m, g_head, layer, n_new, c_acc):
    nbatch, tp, _ = proj3.shape
    hs = M_HEADS
    inner = hs * M_DV
    qkw = M_QK_WIDTH
    v_col = 2 * qkw // inner
    keys = gates3_t.shape[2]
    m_row = state_m[layer].reshape(nbatch, 1, hs)
    has_acc = c_acc is not None
    ns = MLSTM_SAMPLE_SEQS_PER_STEP
    in_specs = [pl.BlockSpec((ns, tp, qkw), lambda b: (b, 0, 0)),
                pl.BlockSpec((ns, tp, qkw), lambda b: (b, 0, 1)),
                pl.BlockSpec((ns, tp, inner), lambda b: (b, 0, v_col)),
                pl.BlockSpec((ns, tp, inner), lambda b: (b, 0, v_col + 1)),
                pl.BlockSpec((ns, tp, inner), lambda b: (b, 0, v_col + 2)),
                pl.BlockSpec((ns, tp, 2 * hs), lambda b: (b, 0, 0)),
                pl.BlockSpec((ns, 2 * hs, keys), lambda b: (b, 0, 0)),
                pl.BlockSpec((1, 2 * hs), lambda b: (0, 0)),
                pl.BlockSpec((2 * hs, 1), lambda b: (0, 0)),
                pl.BlockSpec((ns, 1, hs), lambda b: (b, 0, 0)),
                pl.BlockSpec((1, inner), lambda b: (0, 0)),
                pl.BlockSpec((None, ns, hs, M_DK, M_DV), lambda b: (layer, b, 0, 0, 0)),
                pl.BlockSpec((None, ns, hs, M_DK), lambda b: (layer, b, 0, 0))]
    args = [proj3, proj3, proj3, proj3, proj3, gates3, gates3_t, b_if.reshape(1, 2 * hs), b_if.reshape(2 * hs, 1),
            m_row, g_head.reshape(1, inner), state_c, state_n]
    if has_acc:
        in_specs.append(pl.BlockSpec(memory_space=pl.ANY))
        args.append(c_acc)
    return pl.pallas_call(
        functools.partial(_mlstm_sample_kernel, n_new=n_new, has_acc=has_acc),
        grid=(nbatch // ns,),
        in_specs=in_specs,
        out_specs=[pl.BlockSpec((ns, tp, inner), lambda b: (b, 0, 0)),
                   pl.BlockSpec((None, ns, hs, M_DK, M_DV), lambda b: (layer, b, 0, 0, 0)),
                   pl.BlockSpec((ns, hs, M_DK), lambda b: (b, 0, 0)),
                   pl.BlockSpec((ns, 1, hs), lambda b: (b, 0, 0))],
        out_shape=[jax.ShapeDtypeStruct((nbatch, tp, inner), BF16),
                   jax.ShapeDtypeStruct(state_c.shape, F32),
                   jax.ShapeDtypeStruct((nbatch, hs, M_DK), F32),
                   jax.ShapeDtypeStruct((nbatch, 1, hs), F32)],
        input_output_aliases={len(args) - 1: 1} if has_acc else {},
        compiler_params=_params(("arbitrary",)),
        name="mlstm_sample_step",
    )(*args)


def kernel(x_prompt, x_sample, c_prompt, c_sample, cache_k, cache_v, state_C, state_n, state_m, w_ada, b_ada,
           g_pre, g_post, w_in_attn, sinks, w_out_attn, w_in_mlstm, b_if_mlstm, g_head_mlstm, w_out_mlstm):
    batch, seq, d = x_prompt.shape
    dec_batch, dec_seq, _ = x_sample.shape
    depth = w_ada.shape[0]
    tp = SAMPLE_ROWS
    rows_p = batch * seq
    rows_s = dec_batch * tp
    tm = 1024
    att_inner = ATT_Q_HEADS * ATT_HEAD_DIM
    att_cols = 2 * att_inner + 2 * ATT_KV_WIDTH
    m_cols = 2 * M_QK_WIDTH + 3 * M_HEADS * M_DV

    c_rows = batch + dec_batch
    c_pad = -c_rows % 8
    c_all = jnp.concatenate([c_prompt, c_sample, jnp.zeros((c_pad, d), F32)], axis=0)
    mod = _ada_all_layers(c_all, w_ada, b_ada)

    xp = x_prompt.reshape(rows_p, d)
    xs = jnp.pad(x_sample, ((0, 0), (0, tp - dec_seq), (0, 0))).reshape(rows_s, d)
    cache_kt = cache_k.transpose(0, 1, 3, 4, 2)
    cache_vt = cache_v.transpose(0, 1, 3, 4, 2)
    w_out_attn_b = w_out_attn.astype(BF16)
    w_out_mlstm_b = w_out_mlstm.astype(BF16)
    w_in_mlstm_t = w_in_mlstm.transpose(0, 2, 1)

    split3 = lambda m: (m[..., :d], m[..., d:2 * d], m[..., 2 * d:])
    mods_p = [split3(mod[l, :batch].reshape(batch, 1, 3 * d)) for l in range(depth)]
    mods_s = [split3(jnp.repeat(mod[l, batch:c_rows], tp, axis=0)) for l in range(depth)]

    hp = _prenorm(xp, mods_p[0][0], mods_p[0][1], g_pre[0], tm)
    hs = _prenorm(xs, mods_s[0][0][None], mods_s[0][1][None], g_pre[0], rows_s)
    kp_l, vp_l = [], []
    cp_l, np_l, mp_l, ns_l, ms_l = [], [], [], [], []
    c_sample_new = k_sample_new = v_sample_new = None
    for l in range(depth):
        j = l // 2
        if l % 2 == 0:
            pp, ps = _proj(hp, hs, w_in_attn, j, att_cols, tm, 1024)
            ap = _attn_prompt(pp, sinks[j], batch, seq)
            ps3 = ps.reshape(dec_batch, tp, att_cols)
            a_s, k_sample_new, v_sample_new = _attn_sample(ps3, sinks[j], cache_kt, cache_vt, j, dec_seq,
                                                           k_sample_new, v_sample_new)
            a_s = a_s.reshape(rows_s, att_inner)
            k0, v0 = att_inner, att_inner + ATT_KV_WIDTH
            pp3 = pp.reshape(batch, seq, att_cols)
            kv_shape = (ATT_KV_HEADS, ATT_HEAD_DIM)
            kp_l.append(pp3[:, seq - WINDOW:, k0:v0].reshape((batch, WINDOW) + kv_shape))
            vp_l.append(pp3[:, seq - WINDOW:, v0:v0 + ATT_KV_WIDTH].reshape((batch, WINDOW) + kv_shape))
            w_out = w_out_attn_b
        else:
            pp, ps = _proj(hp, hs, w_in_mlstm_t, j, m_cols, tm, 1024, w_is_transposed=True)
            gp, gs = _gate_proj(hp, hs, w_in_mlstm_t, j, m_cols, 2 * M_HEADS, tm)
            nc = seq // M_CHUNK
            gates_t = gp.T.reshape(2, M_HEADS, batch * nc, M_CHUNK)
            a_q, em_q, wt_q, u_q, w_q, mn_q = _gate_prep(gates_t, b_if_mlstm[j], nc)
            col = jnp.stack([a_q, em_q, u_q]).transpose(2, 3, 0, 1).reshape(rows_p, 3 * M_HEADS)
            rows = jnp.concatenate([w_q, wt_q], axis=0).transpose(1, 0, 2)
            ap, c_new, n_new = _mlstm_prompt(pp, col, rows, g_head_mlstm[j], batch, seq)
            cp_l.append(c_new)
            np_l.append(n_new)
            mp_l.append(mn_q.reshape(M_HEADS, batch, nc, M_CHUNK)[:, :, nc - 1, 0].T)
            ps3 = ps.reshape(dec_batch, tp, m_cols)
            gs3 = gs.reshape(dec_batch, tp, 2 * M_HEADS)
            gs3_t = jnp.pad(gs3.transpose(0, 2, 1), ((0, 0), (0, 0), (0, LANES - tp)))
            a_s, c_sample_new, n_new, m_new = _mlstm_sample(ps3, gs3, gs3_t, b_if_mlstm[j], state_C, state_n, state_m,
                                                            g_head_mlstm[j], j, dec_seq, c_sample_new)
            a_s = a_s.reshape(rows_s, M_HEADS * M_DV)
            ns_l.append(n_new)
            ms_l.append(m_new.reshape(dec_batch, M_HEADS))
            w_out = w_out_mlstm_b
        nxt = min(l + 1, depth - 1)
        mod_p = jnp.stack([mods_p[l][2], mods_p[nxt][0], mods_p[nxt][1]], axis=1)
        mod_s = jnp.stack([mods_s[l][2], mods_s[nxt][0], mods_s[nxt][1]], axis=0)
        outs = _out_proj(ap, a_s, w_out, j, xp, xs, mod_p, mod_s, g_post[l], g_pre[nxt], 512, emit_next=l + 1 < depth)
        xp, xs = outs[:2]
        if l + 1 < depth:
            hp, hs = outs[2:]

    y_prompt = xp.reshape(batch, seq, d)
    y_sample = xs.reshape(dec_batch, tp, d)[:, :dec_seq]
    to_cache_layout = lambda c: c.transpose(0, 1, 4, 2, 3)
    return (y_prompt, y_sample, jnp.stack(kp_l), jnp.stack(vp_l), to_cache_layout(k_sample_new),
            to_cache_layout(v_sample_new), jnp.stack(cp_l), jnp.stack(np_l), jnp.stack(mp_l), c_sample_new,
            jnp.stack(ns_l), jnp.stack(ms_l))
```

```python
import functools

import jax
import jax.numpy as jnp
from jax import lax
from jax.experimental import pallas as pl
from jax.experimental.pallas import tpu as pltpu

F32 = jnp.float32
BF16 = jnp.bfloat16

NORM_EPS = 1e-6
WINDOW = 128
ATT_HEAD_DIM = 64
ATT_KV_HEADS = 8
ATT_GROUP = 4
ATT_Q_HEADS = ATT_KV_HEADS * ATT_GROUP
ATT_KV_WIDTH = ATT_KV_HEADS * ATT_HEAD_DIM
M_HEADS = 8
M_DK = 128
M_DV = 256
M_QK_WIDTH = M_HEADS * M_DK
M_CHUNK = 128
SAMPLE_ROWS = 8
LANES = 128
LOG2E = 1.4426950408889634
OUT_PROJ_SUB_ROWS = 128
ATTN_SAMPLE_SEQS_PER_STEP = 1
MLSTM_SAMPLE_SEQS_PER_STEP = 2

V7X_VMEM_LIMIT = 56 * 1024 * 1024


def _params(sem, vmem=V7X_VMEM_LIMIT):
    return pltpu.CompilerParams(dimension_semantics=sem, vmem_limit_bytes=vmem)


def _sigmoid(x):
    return 1.0 / (1.0 + jnp.exp2(x * -LOG2E))


def _silu(x):
    return x * _sigmoid(x)


def _log_sigmoid(x):
    return jnp.minimum(x, 0.0) - jnp.log1p(jnp.exp(-jnp.abs(x)))


def _alibi_slope(head):
    return float(2.0 ** (-8.0 * (head + 1) / ATT_Q_HEADS))


def _ada_kernel(c_ref, w_ref, b_ref, o_ref):
    s = _silu(c_ref[...]).astype(BF16)
    o_ref[...] = jnp.dot(s, w_ref[...].astype(BF16), preferred_element_type=F32) + b_ref[...]


def _ada_all_layers(c_all, w_ada, b_ada, tn=1024):
    depth, d, n = w_ada.shape
    r = c_all.shape[0]
    return pl.pallas_call(
        _ada_kernel,
        grid=(depth, n // tn),
        in_specs=[
            pl.BlockSpec((r, d), lambda l, j: (0, 0)),
            pl.BlockSpec((None, d, tn), lambda l, j: (l, 0, j)),
            pl.BlockSpec((None, 1, tn), lambda l, j: (l, 0, j)),
        ],
        out_specs=pl.BlockSpec((None, r, tn), lambda l, j: (l, 0, j)),
        out_shape=jax.ShapeDtypeStruct((depth, r, n), F32),
        compiler_params=_params(("arbitrary", "arbitrary")),
        name="adaln_mod",
    )(c_all, w_ada, b_ada.reshape(depth, 1, n))


def _pre_norm_mod(x, shift, scale, g):
    y = x * lax.rsqrt(jnp.mean(x * x, axis=-1, keepdims=True) + NORM_EPS) * g
    return y * (1.0 + scale) + shift


def _prenorm_kernel(x_ref, sh_ref, sc_ref, g_ref, h_ref):
    h_ref[...] = _pre_norm_mod(x_ref[...], sh_ref[...], sc_ref[...], g_ref[...]).astype(h_ref.dtype)


def _prenorm(x, shift, scale, g, tm):
    m, d = x.shape
    groups, r, _ = shift.shape
    tiles_per_group = m // tm // groups
    mod_spec = pl.BlockSpec((None, r, d), lambda i: (i // tiles_per_group, 0, 0))
    return pl.pallas_call(
        _prenorm_kernel,
        grid=(m // tm,),
        in_specs=[pl.BlockSpec((tm, d), lambda i: (i, 0)), mod_spec, mod_spec,
                  pl.BlockSpec((1, d), lambda i: (0, 0))],
        out_specs=pl.BlockSpec((tm, d), lambda i: (i, 0)),
        out_shape=jax.ShapeDtypeStruct((m, d), BF16),
        compiler_params=_params(("arbitrary",)),
        name="prenorm_mod",
    )(x, shift, scale, g.reshape(1, d))


def _matmul(a, w, w_is_transposed):
    dims = (((1,), (1,)), ((), ())) if w_is_transposed else (((1,), (0,)), ((), ()))
    return lax.dot_general(a, w, dims, preferred_element_type=F32)


def _proj_kernel(h_ref, hs_ref, w_ref, o_ref, os_ref, wb_ref, *, w_is_transposed, q_tiles, q_scale):
    @pl.when(pl.program_id(1) == 0)
    def _():
        wb_ref[...] = w_ref[...].astype(BF16)
        os_ref[...] = _matmul(hs_ref[...], wb_ref[...], w_is_transposed)

    acc = _matmul(h_ref[...], wb_ref[...], w_is_transposed)
    if q_tiles:
        acc = acc * jnp.where(pl.program_id(0) < q_tiles, q_scale, 1.0)
    o_ref[...] = acc.astype(o_ref.dtype)


def _proj(h, hs, w_stack, layer, col0, n_cols, tm, tn, out_dtype, q_tiles=0, q_scale=1.0, w_is_transposed=False):
    m, d = h.shape
    ms = hs.shape[0]
    t0 = col0 // tn
    if w_is_transposed:
        w_spec = pl.BlockSpec((None, tn, d), lambda j, i: (layer, t0 + j, 0))
        w_tile = (tn, d)
    else:
        w_spec = pl.BlockSpec((None, d, tn), lambda j, i: (layer, 0, t0 + j))
        w_tile = (d, tn)
    return pl.pallas_call(
        functools.partial(_proj_kernel, w_is_transposed=w_is_transposed, q_tiles=q_tiles, q_scale=q_scale),
        grid=(n_cols // tn, m // tm),
        in_specs=[pl.BlockSpec((tm, d), lambda j, i: (i, 0)),
                  pl.BlockSpec((ms, d), lambda j, i: (0, 0)),
                  w_spec],
        out_specs=[pl.BlockSpec((tm, tn), lambda j, i: (i, j)),
                   pl.BlockSpec((ms, tn), lambda j, i: (0, j))],
        out_shape=[jax.ShapeDtypeStruct((m, n_cols), out_dtype), jax.ShapeDtypeStruct((ms, n_cols), F32)],
        scratch_shapes=[pltpu.VMEM(w_tile, BF16)],
        compiler_params=_params(("arbitrary", "arbitrary")),
        name="in_proj",
    )(h, hs, w_stack)


def _post_norm_residual(y, x, gate, g):
    return x + gate * (y * lax.rsqrt(jnp.mean(y * y, axis=-1, keepdims=True) + NORM_EPS) * g)


def _out_kernel(*refs, emit_next, emit_gates):
    a_ref, as_ref, w_ref, x_ref, xs_ref = refs[:5]
    mod_p, mod_s = refs[5:8], refs[8:11]
    g_ref, gn_ref = refs[11:13]
    wg_ref = refs[13] if emit_gates else None
    outs = refs[13 + emit_gates:]
    pick = lambda k: outs[k] if len(outs) > k else None

    def finish(a_in, x_in, mod, x_out, h_out, g_out):
        per_row = mod[0].shape[0] > 1
        for r in range(0, a_in.shape[0], OUT_PROJ_SUB_ROWS):
            rows = slice(r, r + OUT_PROJ_SUB_ROWS)
            gate, shift, scale = [m[rows, :] if per_row else m[...] for m in mod]
            y = jnp.dot(a_in[rows, :], w_ref[...], preferred_element_type=F32)
            x_new = _post_norm_residual(y, x_in[rows, :], gate, g_ref[...])
            x_out[rows, :] = x_new
            if emit_next:
                h_out[rows, :] = _pre_norm_mod(x_new, shift, scale, gn_ref[...]).astype(h_out.dtype)
        if emit_gates:
            g_out[...] = _matmul(h_out[...], wg_ref[...].astype(BF16), True)

    @pl.when(pl.program_id(0) == 0)
    def _():
        finish(as_ref, xs_ref, mod_s, outs[1], pick(3), pick(5))

    finish(a_ref, x_ref, mod_p, outs[0], pick(2), pick(4))


def _out_proj(a, a_s, w_stack, w_layer, x, xs, mod_p, mod_s, layer, g_post, g_pre_next, tm, emit_next, gate_w=None):
    m, d_in = a.shape
    ms = a_s.shape[0]
    d = x.shape[1]
    nxt = min(layer + 1, mod_p.shape[0] - 1)
    tiles_per_seq = m // tm // mod_p.shape[1]
    once = pl.Buffered(1)
    row_spec = lambda width: pl.BlockSpec((tm, width), lambda i: (i, 0))
    fixed_spec = lambda width: pl.BlockSpec((ms, width), lambda i: (0, 0))
    p_spec = lambda l, part: pl.BlockSpec((None, None, 1, d), lambda i: (l, i // tiles_per_seq, 0, part))
    s_spec = lambda l, part: pl.BlockSpec((None, ms, d), lambda i: (l, 0, part), pipeline_mode=once)
    shift, scale, gate = 0, 1, 2
    in_specs = [row_spec(d_in),
                pl.BlockSpec((ms, d_in), lambda i: (0, 0), pipeline_mode=once),
                pl.BlockSpec((None, d_in, d), lambda i: (w_layer, 0, 0), pipeline_mode=once),
                row_spec(d),
                pl.BlockSpec((ms, d), lambda i: (0, 0), pipeline_mode=once),
                p_spec(layer, gate), p_spec(nxt, shift), p_spec(nxt, scale),
                s_spec(layer, gate), s_spec(nxt, shift), s_spec(nxt, scale),
                pl.BlockSpec((1, d), lambda i: (0, 0)),
                pl.BlockSpec((1, d), lambda i: (0, 0))]
    args = [a, a_s, w_stack, x, xs, mod_p, mod_p, mod_p, mod_s, mod_s, mod_s,
            g_post.reshape(1, d), g_pre_next.reshape(1, d)]
    out_specs = [row_spec(d), fixed_spec(d)]
    out_shape = [jax.ShapeDtypeStruct((m, d), F32), jax.ShapeDtypeStruct((ms, d), F32)]
    if emit_next:
        out_specs += [row_spec(d), fixed_spec(d)]
        out_shape += [jax.ShapeDtypeStruct((m, d), BF16), jax.ShapeDtypeStruct((ms, d), BF16)]
    if gate_w is not None:
        wt_stack, g_layer, row0, n = gate_w
        in_specs.append(pl.BlockSpec((None, n, d), lambda i: (g_layer, row0 // n, 0)))
        args.append(wt_stack)
        out_specs += [row_spec(n), fixed_spec(n)]
        out_shape += [jax.ShapeDtypeStruct((m, n), F32), jax.ShapeDtypeStruct((ms, n), F32)]
    return pl.pallas_call(
        functools.partial(_out_kernel, emit_next=emit_next, emit_gates=gate_w is not None),
        grid=(m // tm,),
        in_specs=in_specs,
        out_specs=out_specs,
        out_shape=out_shape,
        compiler_params=_params(("arbitrary",)),
        name="out_proj_postnorm",
    )(*args)


def _group_select(group_col, values):
    out = values[ATT_GROUP - 1]
    for g in range(ATT_GROUP - 2, -1, -1):
        out = jnp.where(group_col == g, values[g], out)
    return out


def _attn_prompt_kernel(sink_ref, q_ref, kc_ref, kp_ref, vc_ref, vp_ref, z0_ref, z1_ref, o_ref, bias_ref):
    blk = WINDOW
    hd = ATT_HEAD_DIM
    i = pl.program_id(1)
    cols = ATT_GROUP * blk

    @pl.when(i <= 1)
    def _():
        key = lax.broadcasted_iota(jnp.int32, (2 * blk, cols), 0)
        qcol = lax.broadcasted_iota(jnp.int32, (2 * blk, cols), 1)
        dist = (qcol % blk) + blk - key
        valid = (dist >= 0) & (dist < WINDOW) & ((key >= blk) | (i > 0))
        dist_f = dist.astype(F32)
        group = qcol // blk
        for h in range(ATT_KV_HEADS):
            slope = _group_select(group, [_alibi_slope(ATT_GROUP * h + g) for g in range(ATT_GROUP)])
            bias_ref[h] = jnp.where(valid, (-slope * dist_f) * LOG2E, -jnp.inf)

    kcat = jnp.concatenate([kp_ref[...], kc_ref[...]], axis=0)
    vcat = jnp.concatenate([vp_ref[...], vc_ref[...]], axis=0)
    half = z0_ref.shape[1]
    vt_pairs = [jnp.concatenate([vcat[:blk, c * LANES:(c + 1) * LANES].T, vcat[blk:, c * LANES:(c + 1) * LANES].T],
                                axis=1) for c in range(ATT_KV_WIDTH // LANES)]
    zero = jnp.zeros((hd, blk), BF16)
    for h in range(ATT_KV_HEADS):
        pair, odd = divmod(h, 2)
        k_pair = kcat[:, pair * LANES:(pair + 1) * LANES]
        vt = vt_pairs[pair][hd:] if odd else vt_pairs[pair][:hd]
        lhs = jnp.concatenate([vt, jnp.ones_like(vt)], axis=0)
        for t in range(2):
            c0 = (2 * h + t) * LANES
            qt_pair = q_ref[:, c0:c0 + LANES].T
            outs = []
            for e in range(2):
                g = 2 * t + e
                qt = qt_pair[e * hd:(e + 1) * hd]
                rhs = jnp.concatenate([zero, qt] if odd else [qt, zero], axis=0)
                s = jnp.dot(k_pair, rhs, preferred_element_type=F32) + bias_ref[h, :, g * blk:(g + 1) * blk]
                sink = sink_ref[ATT_GROUP * h + g] * LOG2E
                mx = jnp.maximum(jnp.max(s, axis=0, keepdims=True), sink)
                p = jnp.exp2(s - mx).astype(BF16)
                oa = jnp.dot(lhs, p, preferred_element_type=F32)
                den = oa[hd:hd + 1] + jnp.exp2(sink - mx)
                outs.append(oa[:hd] * (1.0 / den))
            ot = jnp.concatenate(outs, axis=0).T
            z_ref, zc = (z0_ref, c0) if c0 < half else (z1_ref, c0 - half)
            o_ref[:, c0:c0 + LANES] = (ot * _silu(z_ref[:, zc:zc + LANES])).astype(o_ref.dtype)


def _attn_prompt(qkv, z, sinks, batch, seq):
    blk = WINDOW
    nb = seq // blk
    inner = ATT_Q_HEADS * ATT_HEAD_DIM
    kvw = ATT_KV_WIDTH
    k_col = inner // kvw
    v_col = k_col + 1
    half = inner // 2
    cur = lambda b, i: b * nb + i
    prev = lambda b, i: b * nb + jnp.maximum(i - 1, 0)
    return pl.pallas_call(
        _attn_prompt_kernel,
        grid=(batch, nb),
        in_specs=[pl.BlockSpec(memory_space=pltpu.SMEM),
                  pl.BlockSpec((blk, inner), lambda b, i: (cur(b, i), 0)),
                  pl.BlockSpec((blk, kvw), lambda b, i: (cur(b, i), k_col)),
                  pl.BlockSpec((blk, kvw), lambda b, i: (prev(b, i), k_col)),
                  pl.BlockSpec((blk, kvw), lambda b, i: (cur(b, i), v_col)),
                  pl.BlockSpec((blk, kvw), lambda b, i: (prev(b, i), v_col)),
                  pl.BlockSpec((blk, half), lambda b, i: (cur(b, i), 0)),
                  pl.BlockSpec((blk, half), lambda b, i: (cur(b, i), 1))],
        out_specs=pl.BlockSpec((blk, inner), lambda b, i: (cur(b, i), 0)),
        out_shape=jax.ShapeDtypeStruct((batch * seq, inner), BF16),
        scratch_shapes=[pltpu.VMEM((ATT_KV_HEADS, 2 * blk, ATT_GROUP * blk), F32)],
        compiler_params=_params(("arbitrary", "arbitrary")),
        name="attn_prompt",
    )(sinks, qkv, qkv, qkv, qkv, qkv, z, z)


def _attn_sample_kernel(*refs, n_new, has_acc):
    sink_ref, bias_ref = refs[0], refs[-1]
    seq_refs = refs[1:8] + refs[8 + 2 * has_acc:-1]
    n_seq, tp = refs[1].shape[:2]
    w = refs[6].shape[3]
    rows = ATT_GROUP * tp

    @pl.when(pl.program_id(0) == 0)
    def _():
        t_row = lax.broadcasted_iota(jnp.int32, (rows, 2 * w), 0) % tp
        key = lax.broadcasted_iota(jnp.int32, (rows, 2 * w), 1)
        dist = t_row + w - key
        valid = (dist >= 0) & (dist < WINDOW) & (key < w + n_new)
        dist_f = dist.astype(F32)
        group = lax.broadcasted_iota(jnp.int32, (rows, 2 * w), 0) // tp
        for h in range(ATT_KV_HEADS):
            slope = _group_select(group, [_alibi_slope(ATT_GROUP * h + g) for g in range(ATT_GROUP)])
            bias_ref[h] = jnp.where(valid, -slope * dist_f, -jnp.inf)

    for sq in range(n_seq):
        _attn_sample_seq(sink_ref, bias_ref, *[r.at[sq] for r in seq_refs], n_new=n_new)


def _attn_sample_seq(sink_ref, bias_ref, q_ref, kn_ref, vn_ref, z0_ref, z1_ref, kc_ref, vc_ref, o_ref, ko_ref, vo_ref,
                     *, n_new):
    hd = ATT_HEAD_DIM
    tp = q_ref.shape[0]
    w = kc_ref.shape[2]
    rows = ATT_GROUP * tp
    nt_dims = (((1,), (1,)), ((), ()))
    pad = jnp.zeros((w - tp, kn_ref.shape[1]), F32)
    kn_pad = jnp.concatenate([kn_ref[...], pad], axis=0)
    vn_pad = jnp.concatenate([vn_ref[...], pad], axis=0)
    kn_b = kn_pad.astype(BF16)
    vn_b = vn_pad.astype(BF16)
    n_pairs = ATT_KV_WIDTH // LANES
    knt_pairs = [kn_pad[:, c * LANES:(c + 1) * LANES].T for c in range(n_pairs)]
    vnt_pairs = [vn_pad[:, c * LANES:(c + 1) * LANES].T for c in range(n_pairs)]
    is_new = lax.broadcasted_iota(jnp.int32, (hd, w), 1) < n_new
    group_col = lax.broadcasted_iota(jnp.int32, (rows, 1), 0) // tp
    half = z0_ref.shape[1]
    for h in range(ATT_KV_HEADS):
        heads = [ATT_GROUP * h + g for g in range(ATT_GROUP)]
        pair, odd = divmod(h, 2)
        kt = kc_ref[h]
        vt = vc_ref[h]
        qs = (jnp.concatenate([q_ref[:, j * hd:(j + 1) * hd] for j in heads], axis=0) * (hd ** -0.5)).astype(BF16)
        s_old = jnp.dot(qs, kt.astype(BF16), preferred_element_type=F32)
        s_new = lax.dot_general(qs, kn_b[:, h * hd:(h + 1) * hd], nt_dims, preferred_element_type=F32)
        s = jnp.concatenate([s_old, s_new], axis=1) + bias_ref[h]
        sink = _group_select(group_col, [sink_ref[j] for j in heads])
        mx = jnp.maximum(jnp.max(s, axis=-1, keepdims=True), sink)
        p = jnp.exp(s - mx)
        den = jnp.sum(p, axis=-1, keepdims=True) + jnp.exp(sink - mx)
        pb = p.astype(BF16)
        o = lax.dot_general(pb[:, :w], vt.astype(BF16), nt_dims, preferred_element_type=F32) \
            + jnp.dot(pb[:, w:], vn_b[:, h * hd:(h + 1) * hd], preferred_element_type=F32)
        o = o * (1.0 / den)
        for g, j in enumerate(heads):
            c0 = j * hd
            z_ref, zc = (z0_ref, c0) if c0 < half else (z1_ref, c0 - half)
            z = z_ref[:, zc:zc + hd]
            o_ref[:, c0:c0 + hd] = (o[g * tp:(g + 1) * tp] * _silu(z)).astype(o_ref.dtype)
        knt = knt_pairs[pair][odd * hd:(odd + 1) * hd]
        vnt = vnt_pairs[pair][odd * hd:(odd + 1) * hd]
        ko_ref[h] = pltpu.roll(jnp.where(is_new, knt, kt), w - n_new, axis=1)
        vo_ref[h] = pltpu.roll(jnp.where(is_new, vnt, vt), w - n_new, axis=1)


def _attn_sample(qkv3, z3, sinks, cache_kt, cache_vt, layer, n_new, k_acc, v_acc):
    nbatch, tp, _ = z3.shape
    w = cache_kt.shape[4]
    inner = ATT_Q_HEADS * ATT_HEAD_DIM
    kvw = ATT_KV_WIDTH
    k_col = inner // kvw
    half = inner // 2
    ns = ATTN_SAMPLE_SEQS_PER_STEP
    cache_spec = pl.BlockSpec((None, ns, ATT_KV_HEADS, ATT_HEAD_DIM, w), lambda b: (layer, b, 0, 0, 0))
    has_acc = k_acc is not None
    in_specs = [pl.BlockSpec(memory_space=pltpu.SMEM),
                pl.BlockSpec((ns, tp, inner), lambda b: (b, 0, 0)),
                pl.BlockSpec((ns, tp, kvw), lambda b: (b, 0, k_col)),
                pl.BlockSpec((ns, tp, kvw), lambda b: (b, 0, k_col + 1)),
                pl.BlockSpec((ns, tp, half), lambda b: (b, 0, 0)),
                pl.BlockSpec((ns, tp, half), lambda b: (b, 0, 1)),
                cache_spec, cache_spec]
    args = [sinks, qkv3, qkv3, qkv3, z3, z3, cache_kt, cache_vt]
    if has_acc:
        in_specs += [pl.BlockSpec(memory_space=pl.ANY)] * 2
        args += [k_acc, v_acc]
    return pl.pallas_call(
        functools.partial(_attn_sample_kernel, n_new=n_new, has_acc=has_acc),
        grid=(nbatch // ns,),
        in_specs=in_specs,
        out_specs=[pl.BlockSpec((ns, tp, inner), lambda b: (b, 0, 0)), cache_spec, cache_spec],
        out_shape=[jax.ShapeDtypeStruct((nbatch, tp, inner), BF16),
                   jax.ShapeDtypeStruct(cache_kt.shape, F32), jax.ShapeDtypeStruct(cache_vt.shape, F32)],
        scratch_shapes=[pltpu.VMEM((ATT_KV_HEADS, ATT_GROUP * tp, 2 * w), F32)],
        input_output_aliases={len(args) - 2: 1, len(args) - 1: 2} if has_acc else {},
        compiler_params=_params(("arbitrary",)),
        name="attn_sample",
    )(*args)


def _lane_scan(x, op, lane):
    n = x.shape[-1]
    shift = 1
    while shift < n:
        x = jnp.where(lane >= shift, op(x, pltpu.roll(x, shift, axis=x.ndim - 1)), x)
        shift *= 2
    return x


def _gate_prep_kernel(bias_ref, g_ref, a_ref, em_ref, wt_ref, u_ref, w_ref, mn_ref, *, chunks_per_seq):
    rows, length = g_ref.shape[2], g_ref.shape[3]
    lane = lax.broadcasted_iota(jnp.int32, (rows, length), 1)
    chunk = lax.broadcasted_iota(jnp.int32, (rows, length), 0) % chunks_per_seq
    for h in range(M_HEADS):
        li = g_ref[0, h] + bias_ref[h]
        lf = _log_sigmoid(g_ref[1, h] + bias_ref[M_HEADS + h])
        bcum = _lane_scan(lf, jnp.add, lane)
        w = li - bcum
        cmax = _lane_scan(w, jnp.maximum, lane)
        e = jnp.broadcast_to(bcum[:, length - 1:length], (rows, length))
        y = e + jnp.broadcast_to(cmax[:, length - 1:length], (rows, length))
        shift = 1
        while shift < chunks_per_seq:
            e_prev = pltpu.roll(e, shift, axis=0)
            y_prev = pltpu.roll(y, shift, axis=0)
            take = chunk >= shift
            y = jnp.where(take, jnp.maximum(y_prev + e, y), y)
            e = jnp.where(take, e_prev + e, e)
            shift *= 2
        m_incl = jnp.maximum(e, y)
        m_prev = jnp.where(chunk >= 1, pltpu.roll(m_incl, 1, axis=0), 0.0)
        u = -jnp.maximum(m_prev, cmax)
        u_last = jnp.broadcast_to(u[:, length - 1:length], (rows, length))
        a_ref[h] = jnp.exp(m_prev + u)
        em_ref[h] = jnp.exp(u - bcum)
        wt_ref[h] = jnp.exp(w + u_last)
        u_ref[h] = u
        w_ref[h] = w
        mn_ref[h] = m_incl


def _gate_prep(gates_t, b_if, chunks_per_seq):
    shape = gates_t.shape[1:]
    out = jax.ShapeDtypeStruct(shape, F32)
    return pl.pallas_call(
        functools.partial(_gate_prep_kernel, chunks_per_seq=chunks_per_seq),
        in_specs=[pl.BlockSpec(memory_space=pltpu.SMEM), pl.BlockSpec(memory_space=pltpu.VMEM)],
        out_specs=[pl.BlockSpec(memory_space=pltpu.VMEM)] * 6,
        out_shape=[out] * 6,
        name="mlstm_gate_prep",
    )(b_if, gates_t)


def _head_norm_gate(hout, g_row, o, z):
    hn = hout * lax.rsqrt(jnp.mean(hout * hout, axis=-1, keepdims=True) + NORM_EPS) * g_row
    return hn * (z / ((1.0 + jnp.exp2(o * -LOG2E)) * (1.0 + jnp.exp2(z * -LOG2E))))


def _mlstm_prompt_kernel(q_ref, k_ref, v_ref, o_ref, z_ref, col_ref, row_ref, gh_ref,
                         out_ref, s_out_ref, n_out_ref, sn_ref):
    c = pl.program_id(1)
    length = q_ref.shape[0]
    hs = M_HEADS

    @pl.when(c == 0)
    def _():
        sn_ref[...] = jnp.zeros_like(sn_ref)

    t_idx = lax.broadcasted_iota(jnp.int32, (length, length), 0)
    s_idx = lax.broadcasted_iota(jnp.int32, (length, length), 1)
    causal = s_idx <= t_idx
    col = col_ref[...]
    ones = jnp.ones((length, LANES), BF16)
    nt_dims = (((1,), (1,)), ((), ()))
    for h in range(hs):
        a_b = jnp.broadcast_to(col[:, h:h + 1], (length, LANES))
        e_b = jnp.broadcast_to(col[:, hs + h:hs + h + 1], (length, LANES))
        u_b = jnp.broadcast_to(col[:, 2 * hs + h:2 * hs + h + 1], (length, length))
        a_last = a_b[length - 1:length, :1]
        qb = q_ref[:, h * M_DK:(h + 1) * M_DK]
        kb = k_ref[:, h * M_DK:(h + 1) * M_DK]
        v1 = jnp.concatenate([v_ref[:, h * M_DV:(h + 1) * M_DV], ones], axis=1)
        dmat = jnp.where(causal, jnp.exp(u_b + row_ref[h:h + 1, :]), 0.0)
        qk = lax.dot_general(qb, kb, nt_dims, preferred_element_type=F32) * dmat
        sn_old = sn_ref[h]
        inter = jnp.dot(qb, sn_old.astype(BF16), preferred_element_type=F32)
        intra = jnp.dot(qk.astype(BF16), v1, preferred_element_type=F32)
        a_b3 = jnp.concatenate([a_b] * (1 + M_DV // LANES), axis=1)
        tot = a_b3 * inter + intra
        inv = 1.0 / jnp.maximum(jnp.abs(tot[:, M_DV:]), e_b)
        hout = tot[:, :M_DV] * jnp.concatenate([inv] * (M_DV // LANES), axis=1)
        kwt = (kb.astype(F32).T * row_ref[hs + h:hs + h + 1, :]).astype(BF16)
        sn_ref[h] = a_last * sn_old + jnp.dot(kwt, v1, preferred_element_type=F32)
        sl = slice(h * M_DV, (h + 1) * M_DV)
        out_ref[:, sl] = _head_norm_gate(hout, gh_ref[:, sl], o_ref[:, sl], z_ref[:, sl]).astype(out_ref.dtype)

    @pl.when(c == pl.num_programs(1) - 1)
    def _():
        for h in range(hs):
            s_out_ref[h] = sn_ref[h, :, :M_DV]
            n_out_ref[h:h + 1, :] = sn_ref[h, :, M_DV:].T[:1, :]


def _mlstm_prompt(qkv, oz, col, rows, g_head, batch, seq):
    length = M_CHUNK
    nc = seq // length
    inner = M_HEADS * M_DV
    qkw = M_QK_WIDTH
    v_col = 2 * qkw // inner
    row = lambda b, c: b * nc + c
    return pl.pallas_call(
        _mlstm_prompt_kernel,
        grid=(batch, nc),
        in_specs=[pl.BlockSpec((length, qkw), lambda b, c: (row(b, c), 0)),
                  pl.BlockSpec((length, qkw), lambda b, c: (row(b, c), 1)),
                  pl.BlockSpec((length, inner), lambda b, c: (row(b, c), v_col)),
                  pl.BlockSpec((length, inner), lambda b, c: (row(b, c), 0)),
                  pl.BlockSpec((length, inner), lambda b, c: (row(b, c), 1)),
                  pl.BlockSpec((length, 3 * M_HEADS), lambda b, c: (row(b, c), 0)),
                  pl.BlockSpec((None, 2 * M_HEADS, length), lambda b, c: (row(b, c), 0, 0)),
                  pl.BlockSpec((1, inner), lambda b, c: (0, 0))],
        out_specs=[pl.BlockSpec((length, inner), lambda b, c: (row(b, c), 0)),
                   pl.BlockSpec((None, M_HEADS, M_DK, M_DV), lambda b, c: (b, 0, 0, 0)),
                   pl.BlockSpec((None, M_HEADS, M_DK), lambda b, c: (b, 0, 0))],
        out_shape=[jax.ShapeDtypeStruct((batch * seq, inner), BF16),
                   jax.ShapeDtypeStruct((batch, M_HEADS, M_DK, M_DV), F32),
                   jax.ShapeDtypeStruct((batch, M_HEADS, M_DK), F32)],
        scratch_shapes=[pltpu.VMEM((M_HEADS, M_DK, M_DV + LANES), F32)],
        compiler_params=_params(("arbitrary", "arbitrary")),
        name="mlstm_prompt_scan",
    )(qkv, qkv, qkv, oz, oz, col, rows, g_head.reshape(1, inner))


def _prefix_scan(x, op, axis, n):
    idx = lax.broadcasted_iota(jnp.int32, x.shape, axis)
    take = (lambda t: x[t:t + 1, :]) if axis == 0 else (lambda t: x[:, t:t + 1])
    run = take(0)
    out = jnp.broadcast_to(run, x.shape)
    for t in range(1, n):
        run = op(run, take(t))
        out = jnp.where(idx >= t, run, out)
    return out


def _mlstm_sample_kernel(*refs, n_new, has_acc):
    ins, outs = refs[:13], refs[13 + has_acc:]
    shared = (7, 8, 10)
    for sq in range(ins[0].shape[0]):
        seq_ins = [r if i in shared else r.at[sq] for i, r in enumerate(ins)]
        _mlstm_sample_seq(*seq_ins, *[r.at[sq] for r in outs], n_new=n_new)


def _mlstm_sample_seq(q_ref, k_ref, v_ref, o_ref, z_ref, g_ref, gt_ref, brow_ref, bcol_ref, mrow_ref, gh_ref,
                      s_in_ref, n_in_ref, out_ref, s_out_ref, n_out_ref, m_out_ref, *, n_new):
    tp = q_ref.shape[0]
    last = n_new - 1
    hs = M_HEADS
    g = g_ref[...] + brow_ref[...]
    bcum_c = _prefix_scan(_log_sigmoid(g[:, hs:]), jnp.add, 0, n_new)
    w_c = g[:, :hs] - bcum_c
    m_prev_c = mrow_ref[...]
    u_c = -jnp.maximum(m_prev_c, _prefix_scan(w_c, jnp.maximum, 0, n_new))
    a_c = jnp.exp(m_prev_c + u_c)
    e_c = jnp.exp(u_c - bcum_c)
    real_c = lax.broadcasted_iota(jnp.int32, (tp, hs), 0) < n_new
    wt_c = jnp.where(real_c, jnp.exp(w_c + u_c[last:last + 1, :]), 0.0)
    m_out_ref[...] = bcum_c[last:last + 1, :] - u_c[last:last + 1, :]
    gt = gt_ref[...] + bcol_ref[...]
    w_r = gt[:hs, :] - _prefix_scan(_log_sigmoid(gt[hs:, :]), jnp.add, 1, n_new)
    keys = w_r.shape[1]
    t_idx = lax.broadcasted_iota(jnp.int32, (tp, keys), 0)
    s_idx = lax.broadcasted_iota(jnp.int32, (tp, keys), 1)
    causal = s_idx <= t_idx
    k_pad = jnp.concatenate([k_ref[...], jnp.zeros((keys - tp, k_ref.shape[1]), F32)], axis=0)
    v_pad = jnp.concatenate([v_ref[...], jnp.zeros((keys - tp, v_ref.shape[1]), F32)], axis=0).astype(BF16)
    wt_pad = jnp.concatenate([wt_c, jnp.zeros((keys - tp, hs), F32)], axis=0)
    for h in range(hs):
        a_col = a_c[:, h:h + 1]
        a_last = a_col[last:last + 1, :]
        qf = q_ref[:, h * M_DK:(h + 1) * M_DK] * (M_DK ** -0.5)
        qb = qf.astype(BF16)
        kf = k_pad[:, h * M_DK:(h + 1) * M_DK]
        vb = v_pad[:, h * M_DV:(h + 1) * M_DV]
        dmat = jnp.where(causal, jnp.exp(u_c[:, h:h + 1] + w_r[h:h + 1, :]), 0.0)
        qk = lax.dot_general(qb, kf.astype(BF16), (((1,), (1,)), ((), ())), preferred_element_type=F32) * dmat
        s_old = s_in_ref[h]
        n_old = n_in_ref[h:h + 1, :]
        num = a_col * jnp.dot(qb, s_old.astype(BF16), preferred_element_type=F32) \
            + jnp.dot(qk.astype(BF16), vb, preferred_element_type=F32)
        den = a_col * jnp.sum(qf * n_old, axis=-1, keepdims=True) + jnp.sum(qk, axis=-1, keepdims=True)
        hout = num / jnp.maximum(jnp.abs(den), e_c[:, h:h + 1])
        kw = kf * wt_pad[:, h:h + 1]
        s_out_ref[h] = a_last * s_old + jnp.dot(kw.T.astype(BF16), vb, preferred_element_type=F32)
        n_out_ref[h:h + 1, :] = a_last * n_old + jnp.sum(kw, axis=0, keepdims=True)
        sl = slice(h * M_DV, (h + 1) * M_DV)
        out_ref[:, sl] = _head_norm_gate(hout, gh_ref[:, sl], o_ref[:, sl], z_ref[:, sl]).astype(out_ref.dtype)


def _mlstm_sample(qkv3, oz3, gates3, gates3_t, b_if, state_c, state_n, state_m, g_head, layer, n_new, c_acc):
    nbatch, tp, _ = qkv3.shape
    hs = M_HEADS
    inner = hs * M_DV
    qkw = M_QK_WIDTH
    v_col = 2 * qkw // inner
    keys = gates3_t.shape[2]
    m_row = state_m[layer].reshape(nbatch, 1, hs)
    has_acc = c_acc is not None
    ns = MLSTM_SAMPLE_SEQS_PER_STEP
    in_specs = [pl.BlockSpec((ns, tp, qkw), lambda b: (b, 0, 0)),
                pl.BlockSpec((ns, tp, qkw), lambda b: (b, 0, 1)),
                pl.BlockSpec((ns, tp, inner), lambda b: (b, 0, v_col)),
                pl.BlockSpec((ns, tp, inner), lambda b: (b, 0, 0)),
                pl.BlockSpec((ns, tp, inner), lambda b: (b, 0, 1)),
                pl.BlockSpec((ns, tp, 2 * hs), lambda b: (b, 0, 0)),
                pl.BlockSpec((ns, 2 * hs, keys), lambda b: (b, 0, 0)),
                pl.BlockSpec((1, 2 * hs), lambda b: (0, 0)),
                pl.BlockSpec((2 * hs, 1), lambda b: (0, 0)),
                pl.BlockSpec((ns, 1, hs), lambda b: (b, 0, 0)),
                pl.BlockSpec((1, inner), lambda b: (0, 0)),
                pl.BlockSpec((None, ns, hs, M_DK, M_DV), lambda b: (layer, b, 0, 0, 0)),
                pl.BlockSpec((None, ns, hs, M_DK), lambda b: (layer, b, 0, 0))]
    args = [qkv3, qkv3, qkv3, oz3, oz3, gates3, gates3_t, b_if.reshape(1, 2 * hs), b_if.reshape(2 * hs, 1),
            m_row, g_head.reshape(1, inner), state_c, state_n]
    if has_acc:
        in_specs.append(pl.BlockSpec(memory_space=pl.ANY))
        args.append(c_acc)
    return pl.pallas_call(
        functools.partial(_mlstm_sample_kernel, n_new=n_new, has_acc=has_acc),
        grid=(nbatch // ns,),
        in_specs=in_specs,
        out_specs=[pl.BlockSpec((ns, tp, inner), lambda b: (b, 0, 0)),
                   pl.BlockSpec((None, ns, hs, M_DK, M_DV), lambda b: (layer, b, 0, 0, 0)),
                   pl.BlockSpec((ns, hs, M_DK), lambda b: (b, 0, 0)),
                   pl.BlockSpec((ns, 1, hs), lambda b: (b, 0, 0))],
        out_shape=[jax.ShapeDtypeStruct((nbatch, tp, inner), BF16),
                   jax.ShapeDtypeStruct(state_c.shape, F32),
                   jax.ShapeDtypeStruct((nbatch, hs, M_DK), F32),
                   jax.ShapeDtypeStruct((nbatch, 1, hs), F32)],
        input_output_aliases={len(args) - 1: 1} if has_acc else {},
        compiler_params=_params(("arbitrary",)),
        name="mlstm_sample_step",
    )(*args)


def kernel(x_prompt, x_sample, c_prompt, c_sample, cache_k, cache_v, state_C, state_n, state_m, w_ada, b_ada,
           g_pre, g_post, w_in_attn, sinks, w_out_attn, w_in_mlstm, b_if_mlstm, g_head_mlstm, w_out_mlstm):
    batch, seq, d = x_prompt.shape
    dec_batch, dec_seq, _ = x_sample.shape
    depth = w_ada.shape[0]
    tp = SAMPLE_ROWS
    rows_p = batch * seq
    rows_s = dec_batch * tp
    tm = 1024
    att_inner = ATT_Q_HEADS * ATT_HEAD_DIM
    att_qkv = att_inner + 2 * ATT_KV_WIDTH
    m_qkv = 2 * M_QK_WIDTH + M_HEADS * M_DV
    m_cols = m_qkv + 2 * M_HEADS * M_DV

    c_rows = batch + dec_batch
    c_pad = -c_rows % 8
    c_all = jnp.concatenate([c_prompt, c_sample, jnp.zeros((c_pad, d), F32)], axis=0)
    mod = _ada_all_layers(c_all, w_ada, b_ada)

    xp = x_prompt.reshape(rows_p, d)
    xs = jnp.pad(x_sample, ((0, 0), (0, tp - dec_seq), (0, 0))).reshape(rows_s, d)
    cache_kt = cache_k.transpose(0, 1, 3, 4, 2)
    cache_vt = cache_v.transpose(0, 1, 3, 4, 2)
    w_out_attn_b = w_out_attn.astype(BF16)
    w_out_mlstm_b = w_out_mlstm.astype(BF16)
    w_in_mlstm_t = w_in_mlstm.transpose(0, 2, 1)

    mod_p = mod[:, :batch].reshape(depth, batch, 1, 3 * d)
    mod_s = jnp.repeat(mod[:, batch:c_rows], tp, axis=1)

    hp = _prenorm(xp, mod_p[0, :, :, :d], mod_p[0, :, :, d:2 * d], g_pre[0], tm)
    hs = _prenorm(xs, mod_s[:1, :, :d], mod_s[:1, :, d:2 * d], g_pre[0], rows_s)
    gp = gs = None
    kp_l, vp_l = [], []
    cp_l, np_l, mp_l, ns_l, ms_l = [], [], [], [], []
    c_sample_new = k_sample_new = v_sample_new = None
    for l in range(depth):
        j = l // 2
        if l % 2 == 0:
            h_tail = hp.reshape(batch, seq, d)[:, seq - WINDOW:].reshape(batch * WINDOW, d)
            qkv_p, qkv_s = _proj(hp, jnp.concatenate([hs, h_tail], axis=0), w_in_attn, j, 0, att_qkv, tm, 1024, BF16,
                                 q_tiles=att_inner // 1024, q_scale=ATT_HEAD_DIM ** -0.5 * LOG2E)
            z_p, z_s = _proj(hp, hs, w_in_attn, j, att_qkv, att_inner, tm, 1024, F32)
            ap = _attn_prompt(qkv_p, z_p, sinks[j], batch, seq)
            qkv_s3 = qkv_s.reshape(-1, tp, att_qkv)
            a_s, k_sample_new, v_sample_new = _attn_sample(qkv_s3, z_s.reshape(dec_batch, tp, att_inner), sinks[j],
                                                           cache_kt, cache_vt, j, dec_seq, k_sample_new, v_sample_new)
            a_s = a_s.reshape(rows_s, att_inner)
            k0, v0 = att_inner, att_inner + ATT_KV_WIDTH
            kv_tail = qkv_s[rows_s:].reshape(batch, WINDOW, att_qkv)
            kv_shape = (ATT_KV_HEADS, ATT_HEAD_DIM)
            kp_l.append(kv_tail[:, :, k0:v0].reshape((batch, WINDOW) + kv_shape))
            vp_l.append(kv_tail[:, :, v0:v0 + ATT_KV_WIDTH].reshape((batch, WINDOW) + kv_shape))
            w_out = w_out_attn_b
        else:
            qkv_p, qkv_s = _proj(hp, hs, w_in_mlstm_t, j, 0, m_qkv, tm, 1024, BF16, q_tiles=M_QK_WIDTH // 1024,
                                 q_scale=M_DK ** -0.5, w_is_transposed=True)
            oz_p, oz_s = _proj(hp, hs, w_in_mlstm_t, j, m_qkv, m_cols - m_qkv, tm, 1024, F32, w_is_transposed=True)
            nc = seq // M_CHUNK
            gates_t = gp.T.reshape(2, M_HEADS, batch * nc, M_CHUNK)
            a_q, em_q, wt_q, u_q, w_q, mn_q = _gate_prep(gates_t, b_if_mlstm[j], nc)
            col = jnp.stack([a_q, em_q, u_q]).transpose(2, 3, 0, 1).reshape(rows_p, 3 * M_HEADS)
            rows = jnp.concatenate([w_q, wt_q], axis=0).transpose(1, 0, 2)
            ap, c_new, n_new = _mlstm_prompt(qkv_p, oz_p, col, rows, g_head_mlstm[j], batch, seq)
            cp_l.append(c_new)
            np_l.append(n_new)
            mp_l.append(mn_q.reshape(M_HEADS, batch, nc, M_CHUNK)[:, :, nc - 1, 0].T)
            gs3 = gs.reshape(dec_batch, tp, 2 * M_HEADS)
            gs3_t = jnp.pad(gs3.transpose(0, 2, 1), ((0, 0), (0, 0), (0, LANES - tp)))
            a_s, c_sample_new, n_new, m_new = _mlstm_sample(
                qkv_s.reshape(dec_batch, tp, m_qkv), oz_s.reshape(dec_batch, tp, m_cols - m_qkv), gs3, gs3_t,
                b_if_mlstm[j], state_C, state_n, state_m, g_head_mlstm[j], j, dec_seq, c_sample_new)
            a_s = a_s.reshape(rows_s, M_HEADS * M_DV)
            ns_l.append(n_new)
            ms_l.append(m_new.reshape(dec_batch, M_HEADS))
            w_out = w_out_mlstm_b
        nxt = min(l + 1, depth - 1)
        next_is_mlstm = l + 1 < depth and (l + 1) % 2 == 1
        gate_w = (w_in_mlstm_t, (l + 1) // 2, m_cols, 2 * M_HEADS) if next_is_mlstm else None
        outs = _out_proj(ap, a_s, w_out, j, xp, xs, mod_p, mod_s, l, g_post[l], g_pre[nxt], 512,
                         emit_next=l + 1 < depth, gate_w=gate_w)
        xp, xs = outs[:2]
        if l + 1 < depth:
            hp, hs = outs[2:4]
        if next_is_mlstm:
            gp, gs = outs[4:]

    y_prompt = xp.reshape(batch, seq, d)
    y_sample = xs.reshape(dec_batch, tp, d)[:, :dec_seq]
    to_cache_layout = lambda c: c.transpose(0, 1, 4, 2, 3)
    return (y_prompt, y_sample, jnp.stack(kp_l), jnp.stack(vp_l), to_cache_layout(k_sample_new),
            to_cache_layout(v_sample_new), jnp.stack(cp_l), jnp.stack(np_l), jnp.stack(mp_l), c_sample_new,
            jnp.stack(ns_l), jnp.stack(ms_l))
```

```python
import functools

import jax
import jax.numpy as jnp
from jax import lax
from jax.experimental import pallas as pl
from jax.experimental.pallas import tpu as pltpu

F32 = jnp.float32
BF16 = jnp.bfloat16

NORM_EPS = 1e-6
WINDOW = 128
ATT_HEAD_DIM = 64
ATT_KV_HEADS = 8
ATT_GROUP = 4
ATT_Q_HEADS = ATT_KV_HEADS * ATT_GROUP
ATT_KV_WIDTH = ATT_KV_HEADS * ATT_HEAD_DIM
M_HEADS = 8
M_DK = 128
M_DV = 256
M_QK_WIDTH = M_HEADS * M_DK
M_CHUNK = 128
SAMPLE_ROWS = 8
LANES = 128
LOG2E = 1.4426950408889634
OUT_PROJ_SUB_ROWS = 128
ATTN_SAMPLE_SEQS_PER_STEP = 1
MLSTM_SAMPLE_SEQS_PER_STEP = 2
MLSTM_CHUNKS_PER_STEP = 2

V7X_VMEM_LIMIT = 56 * 1024 * 1024


def _params(sem, vmem=V7X_VMEM_LIMIT):
    return pltpu.CompilerParams(dimension_semantics=sem, vmem_limit_bytes=vmem)


def _sigmoid(x):
    return 1.0 / (1.0 + jnp.exp2(x * -LOG2E))


def _silu(x):
    return x * _sigmoid(x)


def _log_sigmoid(x):
    return jnp.minimum(x, 0.0) - jnp.log1p(jnp.exp(-jnp.abs(x)))


def _alibi_slope(head):
    return float(2.0 ** (-8.0 * (head + 1) / ATT_Q_HEADS))


def _ada_kernel(c_ref, w_ref, b_ref, o_ref):
    s = _silu(c_ref[...]).astype(BF16)
    o_ref[...] = jnp.dot(s, w_ref[...].astype(BF16), preferred_element_type=F32) + b_ref[...]


def _ada_all_layers(c_all, w_ada, b_ada, tn=1024):
    depth, d, n = w_ada.shape
    r = c_all.shape[0]
    return pl.pallas_call(
        _ada_kernel,
        grid=(depth, n // tn),
        in_specs=[
            pl.BlockSpec((r, d), lambda l, j: (0, 0)),
            pl.BlockSpec((None, d, tn), lambda l, j: (l, 0, j)),
            pl.BlockSpec((None, 1, tn), lambda l, j: (l, 0, j)),
        ],
        out_specs=pl.BlockSpec((None, r, tn), lambda l, j: (l, 0, j)),
        out_shape=jax.ShapeDtypeStruct((depth, r, n), F32),
        compiler_params=_params(("arbitrary", "arbitrary")),
        name="adaln_mod",
    )(c_all, w_ada, b_ada.reshape(depth, 1, n))


def _pre_norm_mod(x, shift, scale, g):
    y = x * lax.rsqrt(jnp.mean(x * x, axis=-1, keepdims=True) + NORM_EPS) * g
    return y * (1.0 + scale) + shift


def _prenorm_kernel(x_ref, sh_ref, sc_ref, g_ref, h_ref):
    h_ref[...] = _pre_norm_mod(x_ref[...], sh_ref[...], sc_ref[...], g_ref[...]).astype(h_ref.dtype)


def _prenorm(x, shift, scale, g, tm):
    m, d = x.shape
    groups, r, _ = shift.shape
    tiles_per_group = m // tm // groups
    mod_spec = pl.BlockSpec((None, r, d), lambda i: (i // tiles_per_group, 0, 0))
    return pl.pallas_call(
        _prenorm_kernel,
        grid=(m // tm,),
        in_specs=[pl.BlockSpec((tm, d), lambda i: (i, 0)), mod_spec, mod_spec,
                  pl.BlockSpec((1, d), lambda i: (0, 0))],
        out_specs=pl.BlockSpec((tm, d), lambda i: (i, 0)),
        out_shape=jax.ShapeDtypeStruct((m, d), BF16),
        compiler_params=_params(("arbitrary",)),
        name="prenorm_mod",
    )(x, shift, scale, g.reshape(1, d))


def _matmul(a, w, w_is_transposed):
    dims = (((1,), (1,)), ((), ())) if w_is_transposed else (((1,), (0,)), ((), ()))
    return lax.dot_general(a, w, dims, preferred_element_type=F32)


def _proj_kernel(*refs, w_is_transposed, q_tiles, q_scale, with_cast):
    if with_cast:
        h_ref, hs_ref, w_ref, cast_in_ref, o_ref, os_ref, cast_out_ref, wb_ref = refs
        cast_out_ref[...] = cast_in_ref[...].astype(BF16)
    else:
        h_ref, hs_ref, w_ref, o_ref, os_ref, wb_ref = refs

    @pl.when(pl.program_id(1) == 0)
    def _():
        wb_ref[...] = w_ref[...].astype(BF16)
        os_ref[...] = _matmul(hs_ref[...], wb_ref[...], w_is_transposed)

    acc = _matmul(h_ref[...], wb_ref[...], w_is_transposed)
    if q_tiles:
        acc = acc * jnp.where(pl.program_id(0) < q_tiles, q_scale, 1.0)
    o_ref[...] = acc.astype(o_ref.dtype)


def _proj(h, hs, w_stack, layer, col0, n_cols, tm, tn, out_dtype, q_tiles=0, q_scale=1.0, w_is_transposed=False,
          cast_job=None):
    m, d = h.shape
    ms = hs.shape[0]
    t0 = col0 // tn
    m_tiles = m // tm
    if w_is_transposed:
        w_spec = pl.BlockSpec((None, tn, d), lambda j, i: (layer, t0 + j, 0))
        w_tile = (tn, d)
    else:
        w_spec = pl.BlockSpec((None, d, tn), lambda j, i: (layer, 0, t0 + j))
        w_tile = (d, tn)
    in_specs = [pl.BlockSpec((tm, d), lambda j, i: (i, 0)),
                pl.BlockSpec((ms, d), lambda j, i: (0, 0)),
                w_spec]
    args = [h, hs, w_stack]
    out_specs = [pl.BlockSpec((tm, tn), lambda j, i: (i, j)),
                 pl.BlockSpec((ms, tn), lambda j, i: (0, j))]
    out_shape = [jax.ShapeDtypeStruct((m, n_cols), out_dtype), jax.ShapeDtypeStruct((ms, n_cols), F32)]
    if cast_job is not None:
        w2_stack, layer2 = cast_job
        _, k2, n2 = w2_stack.shape
        slabs = n2 // LANES
        assert (n_cols // tn) * m_tiles >= slabs
        slab = lambda j, i: jnp.minimum(j * m_tiles + i, slabs - 1)
        in_specs.append(pl.BlockSpec((None, k2, LANES), lambda j, i: (layer2, 0, slab(j, i))))
        args.append(w2_stack)
        out_specs.append(pl.BlockSpec((k2, LANES), lambda j, i: (0, slab(j, i))))
        out_shape.append(jax.ShapeDtypeStruct((k2, n2), BF16))
    return pl.pallas_call(
        functools.partial(_proj_kernel, w_is_transposed=w_is_transposed, q_tiles=q_tiles, q_scale=q_scale,
                          with_cast=cast_job is not None),
        grid=(n_cols // tn, m_tiles),
        in_specs=in_specs,
        out_specs=out_specs,
        out_shape=out_shape,
        scratch_shapes=[pltpu.VMEM(w_tile, BF16)],
        compiler_params=_params(("arbitrary", "arbitrary")),
        name="in_proj",
    )(*args)


def _post_norm_residual(y, x, gate, g):
    return x + gate * (y * lax.rsqrt(jnp.mean(y * y, axis=-1, keepdims=True) + NORM_EPS) * g)


def _out_kernel(*refs, emit_next, emit_gates):
    a_ref, as_ref, w_ref, x_ref, xs_ref = refs[:5]
    mod_p, mod_s = refs[5:8], refs[8:11]
    g_ref, gn_ref = refs[11:13]
    wg_ref = refs[13] if emit_gates else None
    outs = refs[13 + emit_gates:]
    pick = lambda k: outs[k] if len(outs) > k else None

    def finish(a_in, x_in, mod, x_out, h_out, g_out):
        per_row = mod[0].shape[0] > 1
        for r in range(0, a_in.shape[0], OUT_PROJ_SUB_ROWS):
            rows = slice(r, r + OUT_PROJ_SUB_ROWS)
            gate, shift, scale = [m[rows, :] if per_row else m[...] for m in mod]
            y = jnp.dot(a_in[rows, :], w_ref[...], preferred_element_type=F32)
            x_new = _post_norm_residual(y, x_in[rows, :], gate, g_ref[...])
            x_out[rows, :] = x_new
            if emit_next:
                h_out[rows, :] = _pre_norm_mod(x_new, shift, scale, gn_ref[...]).astype(h_out.dtype)
        if emit_gates:
            g_out[...] = _matmul(h_out[...], wg_ref[...].astype(BF16), True)

    @pl.when(pl.program_id(0) == 0)
    def _():
        finish(as_ref, xs_ref, mod_s, outs[1], pick(3), pick(5))

    finish(a_ref, x_ref, mod_p, outs[0], pick(2), pick(4))


def _out_proj(a, a_s, w, x, xs, mod_p, mod_s, layer, g_post, g_pre_next, tm, emit_next, gate_w=None):
    m, d_in = a.shape
    ms = a_s.shape[0]
    d = x.shape[1]
    nxt = min(layer + 1, mod_p.shape[0] - 1)
    tiles_per_seq = m // tm // mod_p.shape[1]
    once = pl.Buffered(1)
    row_spec = lambda width: pl.BlockSpec((tm, width), lambda i: (i, 0))
    fixed_spec = lambda width: pl.BlockSpec((ms, width), lambda i: (0, 0))
    p_spec = lambda l, part: pl.BlockSpec((None, None, 1, d), lambda i: (l, i // tiles_per_seq, 0, part))
    s_spec = lambda l, part: pl.BlockSpec((None, ms, d), lambda i: (l, 0, part), pipeline_mode=once)
    shift, scale, gate = 0, 1, 2
    in_specs = [row_spec(d_in),
                pl.BlockSpec((ms, d_in), lambda i: (0, 0), pipeline_mode=once),
                pl.BlockSpec((d_in, d), lambda i: (0, 0), pipeline_mode=once),
                row_spec(d),
                pl.BlockSpec((ms, d), lambda i: (0, 0), pipeline_mode=once),
                p_spec(layer, gate), p_spec(nxt, shift), p_spec(nxt, scale),
                s_spec(layer, gate), s_spec(nxt, shift), s_spec(nxt, scale),
                pl.BlockSpec((1, d), lambda i: (0, 0)),
                pl.BlockSpec((1, d), lambda i: (0, 0))]
    args = [a, a_s, w, x, xs, mod_p, mod_p, mod_p, mod_s, mod_s, mod_s,
            g_post.reshape(1, d), g_pre_next.reshape(1, d)]
    out_specs = [row_spec(d), fixed_spec(d)]
    out_shape = [jax.ShapeDtypeStruct((m, d), F32), jax.ShapeDtypeStruct((ms, d), F32)]
    if emit_next:
        out_specs += [row_spec(d), fixed_spec(d)]
        out_shape += [jax.ShapeDtypeStruct((m, d), BF16), jax.ShapeDtypeStruct((ms, d), BF16)]
    if gate_w is not None:
        wt_stack, g_layer, row0, n = gate_w
        in_specs.append(pl.BlockSpec((None, n, d), lambda i: (g_layer, row0 // n, 0)))
        args.append(wt_stack)
        out_specs += [row_spec(n), fixed_spec(n)]
        out_shape += [jax.ShapeDtypeStruct((m, n), F32), jax.ShapeDtypeStruct((ms, n), F32)]
    return pl.pallas_call(
        functools.partial(_out_kernel, emit_next=emit_next, emit_gates=gate_w is not None),
        grid=(m // tm,),
        in_specs=in_specs,
        out_specs=out_specs,
        out_shape=out_shape,
        compiler_params=_params(("arbitrary",)),
        name="out_proj_postnorm",
    )(*args)


def _group_select(group_col, values):
    out = values[ATT_GROUP - 1]
    for g in range(ATT_GROUP - 2, -1, -1):
        out = jnp.where(group_col == g, values[g], out)
    return out


def _attn_prompt_kernel(sink_ref, q_ref, kc_ref, kp_ref, vc_ref, vp_ref, z0_ref, z1_ref, o_ref, bias_ref):
    blk = WINDOW
    hd = ATT_HEAD_DIM
    i = pl.program_id(1)
    cols = ATT_GROUP * blk

    @pl.when(i <= 1)
    def _():
        key = lax.broadcasted_iota(jnp.int32, (2 * blk, cols), 0)
        qcol = lax.broadcasted_iota(jnp.int32, (2 * blk, cols), 1)
        dist = (qcol % blk) + blk - key
        valid = (dist >= 0) & (dist < WINDOW) & ((key >= blk) | (i > 0))
        dist_f = dist.astype(F32)
        group = qcol // blk
        for h in range(ATT_KV_HEADS):
            slope = _group_select(group, [_alibi_slope(ATT_GROUP * h + g) for g in range(ATT_GROUP)])
            bias_ref[h] = jnp.where(valid, (-slope * dist_f) * LOG2E, -jnp.inf)

    kcat = jnp.concatenate([kp_ref[...], kc_ref[...]], axis=0)
    vcat = jnp.concatenate([vp_ref[...], vc_ref[...]], axis=0)
    half = z0_ref.shape[1]
    vt_pairs = [jnp.concatenate([vcat[:blk, c * LANES:(c + 1) * LANES].T, vcat[blk:, c * LANES:(c + 1) * LANES].T],
                                axis=1) for c in range(ATT_KV_WIDTH // LANES)]
    zero = jnp.zeros((hd, blk), BF16)
    for h in range(ATT_KV_HEADS):
        pair, odd = divmod(h, 2)
        k_pair = kcat[:, pair * LANES:(pair + 1) * LANES]
        vt = vt_pairs[pair][hd:] if odd else vt_pairs[pair][:hd]
        lhs = jnp.concatenate([vt, jnp.ones_like(vt)], axis=0)
        for t in range(2):
            c0 = (2 * h + t) * LANES
            qt_pair = q_ref[:, c0:c0 + LANES].T
            outs = []
            for e in range(2):
                g = 2 * t + e
                qt = qt_pair[e * hd:(e + 1) * hd]
                rhs = jnp.concatenate([zero, qt] if odd else [qt, zero], axis=0)
                s = jnp.dot(k_pair, rhs, preferred_element_type=F32) + bias_ref[h, :, g * blk:(g + 1) * blk]
                sink = sink_ref[ATT_GROUP * h + g] * LOG2E
                mx = jnp.maximum(jnp.max(s, axis=0, keepdims=True), sink)
                p = jnp.exp2(s - mx).astype(BF16)
                oa = jnp.dot(lhs, p, preferred_element_type=F32)
                den = oa[hd:hd + 1] + jnp.exp2(sink - mx)
                outs.append(oa[:hd] * (1.0 / den))
            ot = jnp.concatenate(outs, axis=0).T
            z_ref, zc = (z0_ref, c0) if c0 < half else (z1_ref, c0 - half)
            o_ref[:, c0:c0 + LANES] = (ot * _silu(z_ref[:, zc:zc + LANES])).astype(o_ref.dtype)


def _attn_prompt(qkv, z, sinks, batch, seq):
    blk = WINDOW
    nb = seq // blk
    inner = ATT_Q_HEADS * ATT_HEAD_DIM
    kvw = ATT_KV_WIDTH
    k_col = inner // kvw
    v_col = k_col + 1
    half = inner // 2
    cur = lambda b, i: b * nb + i
    prev = lambda b, i: b * nb + jnp.maximum(i - 1, 0)
    return pl.pallas_call(
        _attn_prompt_kernel,
        grid=(batch, nb),
        in_specs=[pl.BlockSpec(memory_space=pltpu.SMEM),
                  pl.BlockSpec((blk, inner), lambda b, i: (cur(b, i), 0)),
                  pl.BlockSpec((blk, kvw), lambda b, i: (cur(b, i), k_col)),
                  pl.BlockSpec((blk, kvw), lambda b, i: (prev(b, i), k_col)),
                  pl.BlockSpec((blk, kvw), lambda b, i: (cur(b, i), v_col)),
                  pl.BlockSpec((blk, kvw), lambda b, i: (prev(b, i), v_col)),
                  pl.BlockSpec((blk, half), lambda b, i: (cur(b, i), 0)),
                  pl.BlockSpec((blk, half), lambda b, i: (cur(b, i), 1))],
        out_specs=pl.BlockSpec((blk, inner), lambda b, i: (cur(b, i), 0)),
        out_shape=jax.ShapeDtypeStruct((batch * seq, inner), BF16),
        scratch_shapes=[pltpu.VMEM((ATT_KV_HEADS, 2 * blk, ATT_GROUP * blk), F32)],
        compiler_params=_params(("arbitrary", "arbitrary")),
        name="attn_prompt",
    )(sinks, qkv, qkv, qkv, qkv, qkv, z, z)


def _attn_sample_kernel(*refs, n_new, has_acc):
    sink_ref, bias_ref = refs[0], refs[-1]
    seq_refs = refs[1:8] + refs[8 + 2 * has_acc:-1]
    n_seq, tp = refs[1].shape[:2]
    w = refs[6].shape[3]
    rows = ATT_GROUP * tp

    @pl.when(pl.program_id(0) == 0)
    def _():
        t_row = lax.broadcasted_iota(jnp.int32, (rows, 2 * w), 0) % tp
        key = lax.broadcasted_iota(jnp.int32, (rows, 2 * w), 1)
        dist = t_row + w - key
        valid = (dist >= 0) & (dist < WINDOW) & (key < w + n_new)
        dist_f = dist.astype(F32)
        group = lax.broadcasted_iota(jnp.int32, (rows, 2 * w), 0) // tp
        for h in range(ATT_KV_HEADS):
            slope = _group_select(group, [_alibi_slope(ATT_GROUP * h + g) for g in range(ATT_GROUP)])
            bias_ref[h] = jnp.where(valid, -slope * dist_f, -jnp.inf)

    for sq in range(n_seq):
        _attn_sample_seq(sink_ref, bias_ref, *[r.at[sq] for r in seq_refs], n_new=n_new)


def _attn_sample_seq(sink_ref, bias_ref, q_ref, kn_ref, vn_ref, z0_ref, z1_ref, kc_ref, vc_ref, o_ref, ko_ref, vo_ref,
                     *, n_new):
    hd = ATT_HEAD_DIM
    tp = q_ref.shape[0]
    w = kc_ref.shape[2]
    rows = ATT_GROUP * tp
    nt_dims = (((1,), (1,)), ((), ()))
    pad = jnp.zeros((w - tp, kn_ref.shape[1]), F32)
    kn_pad = jnp.concatenate([kn_ref[...], pad], axis=0)
    vn_pad = jnp.concatenate([vn_ref[...], pad], axis=0)
    kn_b = kn_pad.astype(BF16)
    vn_b = vn_pad.astype(BF16)
    n_pairs = ATT_KV_WIDTH // LANES
    knt_pairs = [kn_pad[:, c * LANES:(c + 1) * LANES].T for c in range(n_pairs)]
    vnt_pairs = [vn_pad[:, c * LANES:(c + 1) * LANES].T for c in range(n_pairs)]
    is_new = lax.broadcasted_iota(jnp.int32, (hd, w), 1) < n_new
    group_col = lax.broadcasted_iota(jnp.int32, (rows, 1), 0) // tp
    half = z0_ref.shape[1]
    for h in range(ATT_KV_HEADS):
        heads = [ATT_GROUP * h + g for g in range(ATT_GROUP)]
        pair, odd = divmod(h, 2)
        kt = kc_ref[h]
        vt = vc_ref[h]
        qs = (jnp.concatenate([q_ref[:, j * hd:(j + 1) * hd] for j in heads], axis=0) * (hd ** -0.5)).astype(BF16)
        s_old = jnp.dot(qs, kt.astype(BF16), preferred_element_type=F32)
        s_new = lax.dot_general(qs, kn_b[:, h * hd:(h + 1) * hd], nt_dims, preferred_element_type=F32)
        s = jnp.concatenate([s_old, s_new], axis=1) + bias_ref[h]
        sink = _group_select(group_col, [sink_ref[j] for j in heads])
        mx = jnp.maximum(jnp.max(s, axis=-1, keepdims=True), sink)
        p = jnp.exp(s - mx)
        den = jnp.sum(p, axis=-1, keepdims=True) + jnp.exp(sink - mx)
        pb = p.astype(BF16)
        o = lax.dot_general(pb[:, :w], vt.astype(BF16), nt_dims, preferred_element_type=F32) \
            + jnp.dot(pb[:, w:], vn_b[:, h * hd:(h + 1) * hd], preferred_element_type=F32)
        o = o * (1.0 / den)
        for g, j in enumerate(heads):
            c0 = j * hd
            z_ref, zc = (z0_ref, c0) if c0 < half else (z1_ref, c0 - half)
            z = z_ref[:, zc:zc + hd]
            o_ref[:, c0:c0 + hd] = (o[g * tp:(g + 1) * tp] * _silu(z)).astype(o_ref.dtype)
        knt = knt_pairs[pair][odd * hd:(odd + 1) * hd]
        vnt = vnt_pairs[pair][odd * hd:(odd + 1) * hd]
        ko_ref[h] = pltpu.roll(jnp.where(is_new, knt, kt), w - n_new, axis=1)
        vo_ref[h] = pltpu.roll(jnp.where(is_new, vnt, vt), w - n_new, axis=1)


def _attn_sample(qkv3, z3, sinks, cache_kt, cache_vt, layer, n_new, k_acc, v_acc):
    nbatch, tp, _ = z3.shape
    w = cache_kt.shape[4]
    inner = ATT_Q_HEADS * ATT_HEAD_DIM
    kvw = ATT_KV_WIDTH
    k_col = inner // kvw
    half = inner // 2
    ns = ATTN_SAMPLE_SEQS_PER_STEP
    cache_spec = pl.BlockSpec((None, ns, ATT_KV_HEADS, ATT_HEAD_DIM, w), lambda b: (layer, b, 0, 0, 0))
    has_acc = k_acc is not None
    in_specs = [pl.BlockSpec(memory_space=pltpu.SMEM),
                pl.BlockSpec((ns, tp, inner), lambda b: (b, 0, 0)),
                pl.BlockSpec((ns, tp, kvw), lambda b: (b, 0, k_col)),
                pl.BlockSpec((ns, tp, kvw), lambda b: (b, 0, k_col + 1)),
                pl.BlockSpec((ns, tp, half), lambda b: (b, 0, 0)),
                pl.BlockSpec((ns, tp, half), lambda b: (b, 0, 1)),
                cache_spec, cache_spec]
    args = [sinks, qkv3, qkv3, qkv3, z3, z3, cache_kt, cache_vt]
    if has_acc:
        in_specs += [pl.BlockSpec(memory_space=pl.ANY)] * 2
        args += [k_acc, v_acc]
    return pl.pallas_call(
        functools.partial(_attn_sample_kernel, n_new=n_new, has_acc=has_acc),
        grid=(nbatch // ns,),
        in_specs=in_specs,
        out_specs=[pl.BlockSpec((ns, tp, inner), lambda b: (b, 0, 0)), cache_spec, cache_spec],
        out_shape=[jax.ShapeDtypeStruct((nbatch, tp, inner), BF16),
                   jax.ShapeDtypeStruct(cache_kt.shape, F32), jax.ShapeDtypeStruct(cache_vt.shape, F32)],
        scratch_shapes=[pltpu.VMEM((ATT_KV_HEADS, ATT_GROUP * tp, 2 * w), F32)],
        input_output_aliases={len(args) - 2: 1, len(args) - 1: 2} if has_acc else {},
        compiler_params=_params(("arbitrary",)),
        name="attn_sample",
    )(*args)


def _lane_scan(x, op, lane):
    n = x.shape[-1]
    shift = 1
    while shift < n:
        x = jnp.where(lane >= shift, op(x, pltpu.roll(x, shift, axis=x.ndim - 1)), x)
        shift *= 2
    return x


def _gate_prep_kernel(bias_ref, g_ref, a_ref, em_ref, wt_ref, u_ref, w_ref, mn_ref, *, chunks_per_seq):
    rows, length = g_ref.shape[2], g_ref.shape[3]
    lane = lax.broadcasted_iota(jnp.int32, (rows, length), 1)
    chunk = lax.broadcasted_iota(jnp.int32, (rows, length), 0) % chunks_per_seq
    for h in range(M_HEADS):
        li = g_ref[0, h] + bias_ref[h]
        lf = _log_sigmoid(g_ref[1, h] + bias_ref[M_HEADS + h])
        bcum = _lane_scan(lf, jnp.add, lane)
        w = li - bcum
        cmax = _lane_scan(w, jnp.maximum, lane)
        e = jnp.broadcast_to(bcum[:, length - 1:length], (rows, length))
        y = e + jnp.broadcast_to(cmax[:, length - 1:length], (rows, length))
        shift = 1
        while shift < chunks_per_seq:
            e_prev = pltpu.roll(e, shift, axis=0)
            y_prev = pltpu.roll(y, shift, axis=0)
            take = chunk >= shift
            y = jnp.where(take, jnp.maximum(y_prev + e, y), y)
            e = jnp.where(take, e_prev + e, e)
            shift *= 2
        m_incl = jnp.maximum(e, y)
        m_prev = jnp.where(chunk >= 1, pltpu.roll(m_incl, 1, axis=0), 0.0)
        u = -jnp.maximum(m_prev, cmax)
        u_last = jnp.broadcast_to(u[:, length - 1:length], (rows, length))
        a_ref[h] = jnp.exp(m_prev + u)
        em_ref[h] = jnp.exp(u - bcum)
        wt_ref[h] = jnp.exp(w + u_last)
        u_ref[h] = u
        w_ref[h] = w
        mn_ref[h] = m_incl


def _gate_prep(gates_t, b_if, chunks_per_seq):
    shape = gates_t.shape[1:]
    out = jax.ShapeDtypeStruct(shape, F32)
    return pl.pallas_call(
        functools.partial(_gate_prep_kernel, chunks_per_seq=chunks_per_seq),
        in_specs=[pl.BlockSpec(memory_space=pltpu.SMEM), pl.BlockSpec(memory_space=pltpu.VMEM)],
        out_specs=[pl.BlockSpec(memory_space=pltpu.VMEM)] * 6,
        out_shape=[out] * 6,
        name="mlstm_gate_prep",
    )(b_if, gates_t)


def _head_norm_gate(hout, g_row, o, z):
    hn = hout * lax.rsqrt(jnp.mean(hout * hout, axis=-1, keepdims=True) + NORM_EPS) * g_row
    return hn * (z / ((1.0 + jnp.exp2(o * -LOG2E)) * (1.0 + jnp.exp2(z * -LOG2E))))


def _mlstm_prompt_kernel(q_ref, k_ref, v_ref, o_ref, z_ref, col_ref, row_ref, gh_ref,
                         out_ref, s_out_ref, n_out_ref, sn_ref):
    c = pl.program_id(1)
    n_sub, _, length = row_ref.shape
    hs = M_HEADS

    @pl.when(c == 0)
    def _():
        sn_ref[...] = jnp.zeros_like(sn_ref)

    t_idx = lax.broadcasted_iota(jnp.int32, (length, length), 0)
    s_idx = lax.broadcasted_iota(jnp.int32, (length, length), 1)
    causal = s_idx <= t_idx
    ones = jnp.ones((length, LANES), BF16)
    nt_dims = (((1,), (1,)), ((), ()))
    for sub in range(n_sub):
        r = slice(sub * length, (sub + 1) * length)
        col = col_ref[r, :]
        for h in range(hs):
            a_b = jnp.broadcast_to(col[:, h:h + 1], (length, LANES))
            e_b = jnp.broadcast_to(col[:, hs + h:hs + h + 1], (length, LANES))
            u_b = jnp.broadcast_to(col[:, 2 * hs + h:2 * hs + h + 1], (length, length))
            a_last = a_b[length - 1:length, :1]
            qb = q_ref[r, h * M_DK:(h + 1) * M_DK]
            kb = k_ref[r, h * M_DK:(h + 1) * M_DK]
            v1 = jnp.concatenate([v_ref[r, h * M_DV:(h + 1) * M_DV], ones], axis=1)
            dmat = jnp.where(causal, jnp.exp(u_b + row_ref[sub, h:h + 1, :]), 0.0)
            qk = lax.dot_general(qb, kb, nt_dims, preferred_element_type=F32) * dmat
            sn_old = sn_ref[h]
            inter = jnp.dot(qb, sn_old.astype(BF16), preferred_element_type=F32)
            intra = jnp.dot(qk.astype(BF16), v1, preferred_element_type=F32)
            a_b3 = jnp.concatenate([a_b] * (1 + M_DV // LANES), axis=1)
            tot = a_b3 * inter + intra
            inv = 1.0 / jnp.maximum(jnp.abs(tot[:, M_DV:]), e_b)
            hout = tot[:, :M_DV] * jnp.concatenate([inv] * (M_DV // LANES), axis=1)
            kwt = (kb.astype(F32).T * row_ref[sub, hs + h:hs + h + 1, :]).astype(BF16)
            sn_ref[h] = a_last * sn_old + jnp.dot(kwt, v1, preferred_element_type=F32)
            sl = slice(h * M_DV, (h + 1) * M_DV)
            out_ref[r, sl] = _head_norm_gate(hout, gh_ref[:, sl], o_ref[r, sl], z_ref[r, sl]).astype(out_ref.dtype)

    @pl.when(c == pl.num_programs(1) - 1)
    def _():
        for h in range(hs):
            s_out_ref[h] = sn_ref[h, :, :M_DV]
            n_out_ref[h:h + 1, :] = sn_ref[h, :, M_DV:].T[:1, :]


def _mlstm_prompt(qkv, oz, col, rows, g_head, batch, seq):
    n_sub = MLSTM_CHUNKS_PER_STEP
    length = M_CHUNK
    span = n_sub * length
    nc = seq // span
    inner = M_HEADS * M_DV
    qkw = M_QK_WIDTH
    v_col = 2 * qkw // inner
    row = lambda b, c: b * nc + c
    return pl.pallas_call(
        _mlstm_prompt_kernel,
        grid=(batch, nc),
        in_specs=[pl.BlockSpec((span, qkw), lambda b, c: (row(b, c), 0)),
                  pl.BlockSpec((span, qkw), lambda b, c: (row(b, c), 1)),
                  pl.BlockSpec((span, inner), lambda b, c: (row(b, c), v_col)),
                  pl.BlockSpec((span, inner), lambda b, c: (row(b, c), 0)),
                  pl.BlockSpec((span, inner), lambda b, c: (row(b, c), 1)),
                  pl.BlockSpec((span, 3 * M_HEADS), lambda b, c: (row(b, c), 0)),
                  pl.BlockSpec((n_sub, 2 * M_HEADS, length), lambda b, c: (row(b, c), 0, 0)),
                  pl.BlockSpec((1, inner), lambda b, c: (0, 0))],
        out_specs=[pl.BlockSpec((span, inner), lambda b, c: (row(b, c), 0)),
                   pl.BlockSpec((None, M_HEADS, M_DK, M_DV), lambda b, c: (b, 0, 0, 0)),
                   pl.BlockSpec((None, M_HEADS, M_DK), lambda b, c: (b, 0, 0))],
        out_shape=[jax.ShapeDtypeStruct((batch * seq, inner), BF16),
                   jax.ShapeDtypeStruct((batch, M_HEADS, M_DK, M_DV), F32),
                   jax.ShapeDtypeStruct((batch, M_HEADS, M_DK), F32)],
        scratch_shapes=[pltpu.VMEM((M_HEADS, M_DK, M_DV + LANES), F32)],
        compiler_params=_params(("arbitrary", "arbitrary")),
        name="mlstm_prompt_scan",
    )(qkv, qkv, qkv, oz, oz, col, rows, g_head.reshape(1, inner))


def _prefix_scan(x, op, axis, n):
    idx = lax.broadcasted_iota(jnp.int32, x.shape, axis)
    take = (lambda t: x[t:t + 1, :]) if axis == 0 else (lambda t: x[:, t:t + 1])
    run = take(0)
    out = jnp.broadcast_to(run, x.shape)
    for t in range(1, n):
        run = op(run, take(t))
        out = jnp.where(idx >= t, run, out)
    return out


def _mlstm_sample_kernel(*refs, n_new, has_acc):
    ins, outs = refs[:13], refs[13 + has_acc:]
    shared = (7, 8, 10)
    for sq in range(ins[0].shape[0]):
        seq_ins = [r if i in shared else r.at[sq] for i, r in enumerate(ins)]
        _mlstm_sample_seq(*seq_ins, *[r.at[sq] for r in outs], n_new=n_new)


def _mlstm_sample_seq(q_ref, k_ref, v_ref, o_ref, z_ref, g_ref, gt_ref, brow_ref, bcol_ref, mrow_ref, gh_ref,
                      s_in_ref, n_in_ref, out_ref, s_out_ref, n_out_ref, m_out_ref, *, n_new):
    tp = q_ref.shape[0]
    last = n_new - 1
    hs = M_HEADS
    g = g_ref[...] + brow_ref[...]
    bcum_c = _prefix_scan(_log_sigmoid(g[:, hs:]), jnp.add, 0, n_new)
    w_c = g[:, :hs] - bcum_c
    m_prev_c = mrow_ref[...]
    u_c = -jnp.maximum(m_prev_c, _prefix_scan(w_c, jnp.maximum, 0, n_new))
    a_c = jnp.exp(m_prev_c + u_c)
    e_c = jnp.exp(u_c - bcum_c)
    real_c = lax.broadcasted_iota(jnp.int32, (tp, hs), 0) < n_new
    wt_c = jnp.where(real_c, jnp.exp(w_c + u_c[last:last + 1, :]), 0.0)
    m_out_ref[...] = bcum_c[last:last + 1, :] - u_c[last:last + 1, :]
    gt = gt_ref[...] + bcol_ref[...]
    w_r = gt[:hs, :] - _prefix_scan(_log_sigmoid(gt[hs:, :]), jnp.add, 1, n_new)
    keys = w_r.shape[1]
    t_idx = lax.broadcasted_iota(jnp.int32, (tp, keys), 0)
    s_idx = lax.broadcasted_iota(jnp.int32, (tp, keys), 1)
    causal = s_idx <= t_idx
    k_pad = jnp.concatenate([k_ref[...], jnp.zeros((keys - tp, k_ref.shape[1]), F32)], axis=0)
    v_pad = jnp.concatenate([v_ref[...], jnp.zeros((keys - tp, v_ref.shape[1]), F32)], axis=0).astype(BF16)
    wt_pad = jnp.concatenate([wt_c, jnp.zeros((keys - tp, hs), F32)], axis=0)
    for h in range(hs):
        a_col = a_c[:, h:h + 1]
        a_last = a_col[last:last + 1, :]
        qf = q_ref[:, h * M_DK:(h + 1) * M_DK] * (M_DK ** -0.5)
        qb = qf.astype(BF16)
        kf = k_pad[:, h * M_DK:(h + 1) * M_DK]
        vb = v_pad[:, h * M_DV:(h + 1) * M_DV]
        dmat = jnp.where(causal, jnp.exp(u_c[:, h:h + 1] + w_r[h:h + 1, :]), 0.0)
        qk = lax.dot_general(qb, kf.astype(BF16), (((1,), (1,)), ((), ())), preferred_element_type=F32) * dmat
        s_old = s_in_ref[h]
        n_old = n_in_ref[h:h + 1, :]
        num = a_col * jnp.dot(qb, s_old.astype(BF16), preferred_element_type=F32) \
            + jnp.dot(qk.astype(BF16), vb, preferred_element_type=F32)
        den = a_col * jnp.sum(qf * n_old, axis=-1, keepdims=True) + jnp.sum(qk, axis=-1, keepdims=True)
        hout = num / jnp.maximum(jnp.abs(den), e_c[:, h:h + 1])
        kw = kf * wt_pad[:, h:h + 1]
        s_out_ref[h] = a_last * s_old + jnp.dot(kw.T.astype(BF16), vb, preferred_element_type=F32)
        n_out_ref[h:h + 1, :] = a_last * n_old + jnp.sum(kw, axis=0, keepdims=True)
        sl = slice(h * M_DV, (h + 1) * M_DV)
        out_ref[:, sl] = _head_norm_gate(hout, gh_ref[:, sl], o_ref[:, sl], z_ref[:, sl]).astype(out_ref.dtype)


def _mlstm_sample(qkv3, oz3, gates3, gates3_t, b_if, state_c, state_n, state_m, g_head, layer, n_new, c_acc):
    nbatch, tp, _ = qkv3.shape
    hs = M_HEADS
    inner = hs * M_DV
    qkw = M_QK_WIDTH
    v_col = 2 * qkw // inner
    keys = gates3_t.shape[2]
    m_row = state_m[layer].reshape(nbatch, 1, hs)
    has_acc = c_acc is not None
    ns = MLSTM_SAMPLE_SEQS_PER_STEP
    in_specs = [pl.BlockSpec((ns, tp, qkw), lambda b: (b, 0, 0)),
                pl.BlockSpec((ns, tp, qkw), lambda b: (b, 0, 1)),
                pl.BlockSpec((ns, tp, inner), lambda b: (b, 0, v_col)),
                pl.BlockSpec((ns, tp, inner), lambda b: (b, 0, 0)),
                pl.BlockSpec((ns, tp, inner), lambda b: (b, 0, 1)),
                pl.BlockSpec((ns, tp, 2 * hs), lambda b: (b, 0, 0)),
                pl.BlockSpec((ns, 2 * hs, keys), lambda b: (b, 0, 0)),
                pl.BlockSpec((1, 2 * hs), lambda b: (0, 0)),
                pl.BlockSpec((2 * hs, 1), lambda b: (0, 0)),
                pl.BlockSpec((ns, 1, hs), lambda b: (b, 0, 0)),
                pl.BlockSpec((1, inner), lambda b: (0, 0)),
                pl.BlockSpec((None, ns, hs, M_DK, M_DV), lambda b: (layer, b, 0, 0, 0)),
                pl.BlockSpec((None, ns, hs, M_DK), lambda b: (layer, b, 0, 0))]
    args = [qkv3, qkv3, qkv3, oz3, oz3, gates3, gates3_t, b_if.reshape(1, 2 * hs), b_if.reshape(2 * hs, 1),
            m_row, g_head.reshape(1, inner), state_c, state_n]
    if has_acc:
        in_specs.append(pl.BlockSpec(memory_space=pl.ANY))
        args.append(c_acc)
    return pl.pallas_call(
        functools.partial(_mlstm_sample_kernel, n_new=n_new, has_acc=has_acc),
        grid=(nbatch // ns,),
        in_specs=in_specs,
        out_specs=[pl.BlockSpec((ns, tp, inner), lambda b: (b, 0, 0)),
                   pl.BlockSpec((None, ns, hs, M_DK, M_DV), lambda b: (layer, b, 0, 0, 0)),
                   pl.BlockSpec((ns, hs, M_DK), lambda b: (b, 0, 0)),
                   pl.BlockSpec((ns, 1, hs), lambda b: (b, 0, 0))],
        out_shape=[jax.ShapeDtypeStruct((nbatch, tp, inner), BF16),
                   jax.ShapeDtypeStruct(state_c.shape, F32),
                   jax.ShapeDtypeStruct((nbatch, hs, M_DK), F32),
                   jax.ShapeDtypeStruct((nbatch, 1, hs), F32)],
        input_output_aliases={len(args) - 1: 1} if has_acc else {},
        compiler_params=_params(("arbitrary",)),
        name="mlstm_sample_step",
    )(*args)


def kernel(x_prompt, x_sample, c_prompt, c_sample, cache_k, cache_v, state_C, state_n, state_m, w_ada, b_ada,
           g_pre, g_post, w_in_attn, sinks, w_out_attn, w_in_mlstm, b_if_mlstm, g_head_mlstm, w_out_mlstm):
    batch, seq, d = x_prompt.shape
    dec_batch, dec_seq, _ = x_sample.shape
    depth = w_ada.shape[0]
    tp = SAMPLE_ROWS
    rows_p = batch * seq
    rows_s = dec_batch * tp
    tm = 1024
    att_inner = ATT_Q_HEADS * ATT_HEAD_DIM
    att_qkv = att_inner + 2 * ATT_KV_WIDTH
    m_qkv = 2 * M_QK_WIDTH + M_HEADS * M_DV
    m_cols = m_qkv + 2 * M_HEADS * M_DV

    c_rows = batch + dec_batch
    c_pad = -c_rows % 8
    c_all = jnp.concatenate([c_prompt, c_sample, jnp.zeros((c_pad, d), F32)], axis=0)
    mod = _ada_all_layers(c_all, w_ada, b_ada)

    xp = x_prompt.reshape(rows_p, d)
    xs = jnp.pad(x_sample, ((0, 0), (0, tp - dec_seq), (0, 0))).reshape(rows_s, d)
    cache_kt = cache_k.transpose(0, 1, 3, 4, 2)
    cache_vt = cache_v.transpose(0, 1, 3, 4, 2)
    w_in_mlstm_t = w_in_mlstm.transpose(0, 2, 1)

    mod_p = mod[:, :batch].reshape(depth, batch, 1, 3 * d)
    mod_s = jnp.repeat(mod[:, batch:c_rows], tp, axis=1)

    hp = _prenorm(xp, mod_p[0, :, :, :d], mod_p[0, :, :, d:2 * d], g_pre[0], tm)
    hs = _prenorm(xs, mod_s[:1, :, :d], mod_s[:1, :, d:2 * d], g_pre[0], rows_s)
    gp = gs = None
    kp_l, vp_l = [], []
    cp_l, np_l, mp_l, ns_l, ms_l = [], [], [], [], []
    c_sample_new = k_sample_new = v_sample_new = None
    for l in range(depth):
        j = l // 2
        if l % 2 == 0:
            h_tail = hp.reshape(batch, seq, d)[:, seq - WINDOW:].reshape(batch * WINDOW, d)
            qkv_p, qkv_s = _proj(hp, jnp.concatenate([hs, h_tail], axis=0), w_in_attn, j, 0, att_qkv, tm, 1024, BF16,
                                 q_tiles=att_inner // 1024, q_scale=ATT_HEAD_DIM ** -0.5 * LOG2E)
            z_p, z_s, w_out = _proj(hp, hs, w_in_attn, j, att_qkv, att_inner, tm, 1024, F32,
                                    cast_job=(w_out_attn, j))
            ap = _attn_prompt(qkv_p, z_p, sinks[j], batch, seq)
            qkv_s3 = qkv_s.reshape(-1, tp, att_qkv)
            a_s, k_sample_new, v_sample_new = _attn_sample(qkv_s3, z_s.reshape(dec_batch, tp, att_inner), sinks[j],
                                                           cache_kt, cache_vt, j, dec_seq, k_sample_new, v_sample_new)
            a_s = a_s.reshape(rows_s, att_inner)
            k0, v0 = att_inner, att_inner + ATT_KV_WIDTH
            kv_tail = qkv_s[rows_s:].reshape(batch, WINDOW, att_qkv)
            kv_shape = (ATT_KV_HEADS, ATT_HEAD_DIM)
            kp_l.append(kv_tail[:, :, k0:v0].reshape((batch, WINDOW) + kv_shape))
            vp_l.append(kv_tail[:, :, v0:v0 + ATT_KV_WIDTH].reshape((batch, WINDOW) + kv_shape))
        else:
            qkv_p, qkv_s = _proj(hp, hs, w_in_mlstm_t, j, 0, m_qkv, tm, 1024, BF16, q_tiles=M_QK_WIDTH // 1024,
                                 q_scale=M_DK ** -0.5, w_is_transposed=True)
            oz_p, oz_s, w_out = _proj(hp, hs, w_in_mlstm_t, j, m_qkv, m_cols - m_qkv, tm, 1024, F32,
                                      w_is_transposed=True, cast_job=(w_out_mlstm, j))
            nc = seq // M_CHUNK
            gates_t = gp.T.reshape(2, M_HEADS, batch * nc, M_CHUNK)
            a_q, em_q, wt_q, u_q, w_q, mn_q = _gate_prep(gates_t, b_if_mlstm[j], nc)
            col = jnp.stack([a_q, em_q, u_q]).transpose(2, 3, 0, 1).reshape(rows_p, 3 * M_HEADS)
            rows = jnp.concatenate([w_q, wt_q], axis=0).transpose(1, 0, 2)
            ap, c_new, n_new = _mlstm_prompt(qkv_p, oz_p, col, rows, g_head_mlstm[j], batch, seq)
            cp_l.append(c_new)
            np_l.append(n_new)
            mp_l.append(mn_q.reshape(M_HEADS, batch, nc, M_CHUNK)[:, :, nc - 1, 0].T)
            gs3 = gs.reshape(dec_batch, tp, 2 * M_HEADS)
            gs3_t = jnp.pad(gs3.transpose(0, 2, 1), ((0, 0), (0, 0), (0, LANES - tp)))
            a_s, c_sample_new, n_new, m_new = _mlstm_sample(
                qkv_s.reshape(dec_batch, tp, m_qkv), oz_s.reshape(dec_batch, tp, m_cols - m_qkv), gs3, gs3_t,
                b_if_mlstm[j], state_C, state_n, state_m, g_head_mlstm[j], j, dec_seq, c_sample_new)
            a_s = a_s.reshape(rows_s, M_HEADS * M_DV)
            ns_l.append(n_new)
            ms_l.append(m_new.reshape(dec_batch, M_HEADS))
        nxt = min(l + 1, depth - 1)
        next_is_mlstm = l + 1 < depth and (l + 1) % 2 == 1
        gate_w = (w_in_mlstm_t, (l + 1) // 2, m_cols, 2 * M_HEADS) if next_is_mlstm else None
        outs = _out_proj(ap, a_s, w_out, xp, xs, mod_p, mod_s, l, g_post[l], g_pre[nxt], 512,
                         emit_next=l + 1 < depth, gate_w=gate_w)
        xp, xs = outs[:2]
        if l + 1 < depth:
            hp, hs = outs[2:4]
        if next_is_mlstm:
            gp, gs = outs[4:]

    y_prompt = xp.reshape(batch, seq, d)
    y_sample = xs.reshape(dec_batch, tp, d)[:, :dec_seq]
    to_cache_layout = lambda c: c.transpose(0, 1, 4, 2, 3)
    return (y_prompt, y_sample, jnp.stack(kp_l), jnp.stack(vp_l), to_cache_layout(k_sample_new),
            to_cache_layout(v_sample_new), jnp.stack(cp_l), jnp.stack(np_l), jnp.stack(mp_l), c_sample_new,
            jnp.stack(ns_l), jnp.stack(ms_l))
```

```python
import functools

import jax
import jax.numpy as jnp
from jax import lax
from jax.experimental import pallas as pl
from jax.experimental.pallas import tpu as pltpu

F32 = jnp.float32
BF16 = jnp.bfloat16

NORM_EPS = 1e-6
WINDOW = 128
ATT_HEAD_DIM = 64
ATT_KV_HEADS = 8
ATT_GROUP = 4
ATT_Q_HEADS = ATT_KV_HEADS * ATT_GROUP
ATT_KV_WIDTH = ATT_KV_HEADS * ATT_HEAD_DIM
M_HEADS = 8
M_DK = 128
M_DV = 256
M_QK_WIDTH = M_HEADS * M_DK
M_CHUNK = 128
SAMPLE_ROWS = 8
LANES = 128
LOG2E = 1.4426950408889634
OUT_PROJ_SUB_ROWS = 128
ATTN_SAMPLE_SEQS_PER_STEP = 2
MLSTM_SAMPLE_SEQS_PER_STEP = 4
MLSTM_CHUNKS_PER_STEP = 4

V7X_VMEM_LIMIT = 56 * 1024 * 1024


def _params(sem, vmem=V7X_VMEM_LIMIT):
    return pltpu.CompilerParams(dimension_semantics=sem, vmem_limit_bytes=vmem)


def _sigmoid(x):
    return 1.0 / (1.0 + jnp.exp2(x * -LOG2E))


def _silu(x):
    return x * _sigmoid(x)


def _log_sigmoid(x):
    return jnp.minimum(x, 0.0) - jnp.log1p(jnp.exp(-jnp.abs(x)))


def _alibi_slope(head):
    return float(2.0 ** (-8.0 * (head + 1) / ATT_Q_HEADS))


def _ada_kernel(c_ref, w_ref, b_ref, o_ref):
    s = _silu(c_ref[...]).astype(BF16)
    o_ref[...] = jnp.dot(s, w_ref[...].astype(BF16), preferred_element_type=F32) + b_ref[...]


def _ada_all_layers(c_all, w_ada, b_ada, tn=1024):
    depth, d, n = w_ada.shape
    r = c_all.shape[0]
    return pl.pallas_call(
        _ada_kernel,
        grid=(depth, n // tn),
        in_specs=[
            pl.BlockSpec((r, d), lambda l, j: (0, 0)),
            pl.BlockSpec((None, d, tn), lambda l, j: (l, 0, j)),
            pl.BlockSpec((None, 1, tn), lambda l, j: (l, 0, j)),
        ],
        out_specs=pl.BlockSpec((None, r, tn), lambda l, j: (l, 0, j)),
        out_shape=jax.ShapeDtypeStruct((depth, r, n), F32),
        compiler_params=_params(("arbitrary", "arbitrary")),
        name="adaln_mod",
    )(c_all, w_ada, b_ada.reshape(depth, 1, n))


def _pre_norm_mod(x, shift, scale, g):
    y = x * lax.rsqrt(jnp.mean(x * x, axis=-1, keepdims=True) + NORM_EPS) * g
    return y * (1.0 + scale) + shift


def _prenorm_kernel(x_ref, sh_ref, sc_ref, g_ref, h_ref):
    h_ref[...] = _pre_norm_mod(x_ref[...], sh_ref[...], sc_ref[...], g_ref[...]).astype(h_ref.dtype)


def _prenorm(x, shift, scale, g, tm):
    m, d = x.shape
    groups, r, _ = shift.shape
    tiles_per_group = m // tm // groups
    mod_spec = pl.BlockSpec((None, r, d), lambda i: (i // tiles_per_group, 0, 0))
    return pl.pallas_call(
        _prenorm_kernel,
        grid=(m // tm,),
        in_specs=[pl.BlockSpec((tm, d), lambda i: (i, 0)), mod_spec, mod_spec,
                  pl.BlockSpec((1, d), lambda i: (0, 0))],
        out_specs=pl.BlockSpec((tm, d), lambda i: (i, 0)),
        out_shape=jax.ShapeDtypeStruct((m, d), BF16),
        compiler_params=_params(("arbitrary",)),
        name="prenorm_mod",
    )(x, shift, scale, g.reshape(1, d))


def _matmul(a, w, w_is_transposed):
    dims = (((1,), (1,)), ((), ())) if w_is_transposed else (((1,), (0,)), ((), ()))
    return lax.dot_general(a, w, dims, preferred_element_type=F32)


def _proj_kernel(*refs, w_is_transposed, q_tiles, q_scale, with_cast):
    if with_cast:
        h_ref, hs_ref, w_ref, cast_in_ref, o_ref, os_ref, cast_out_ref, wb_ref = refs
        cast_out_ref[...] = cast_in_ref[...].astype(BF16)
    else:
        h_ref, hs_ref, w_ref, o_ref, os_ref, wb_ref = refs

    @pl.when(pl.program_id(1) == 0)
    def _():
        wb_ref[...] = w_ref[...].astype(BF16)
        os_ref[...] = _matmul(hs_ref[...], wb_ref[...], w_is_transposed)

    acc = _matmul(h_ref[...], wb_ref[...], w_is_transposed)
    if q_tiles:
        acc = acc * jnp.where(pl.program_id(0) < q_tiles, q_scale, 1.0)
    o_ref[...] = acc.astype(o_ref.dtype)


def _proj(h, hs, w_stack, layer, col0, n_cols, tm, tn, out_dtype, q_tiles=0, q_scale=1.0, w_is_transposed=False,
          cast_job=None):
    m, d = h.shape
    ms = hs.shape[0]
    t0 = col0 // tn
    m_tiles = m // tm
    if w_is_transposed:
        w_spec = pl.BlockSpec((None, tn, d), lambda j, i: (layer, t0 + j, 0))
        w_tile = (tn, d)
    else:
        w_spec = pl.BlockSpec((None, d, tn), lambda j, i: (layer, 0, t0 + j))
        w_tile = (d, tn)
    in_specs = [pl.BlockSpec((tm, d), lambda j, i: (i, 0)),
                pl.BlockSpec((ms, d), lambda j, i: (0, 0)),
                w_spec]
    args = [h, hs, w_stack]
    out_specs = [pl.BlockSpec((tm, tn), lambda j, i: (i, j)),
                 pl.BlockSpec((ms, tn), lambda j, i: (0, j))]
    out_shape = [jax.ShapeDtypeStruct((m, n_cols), out_dtype), jax.ShapeDtypeStruct((ms, n_cols), F32)]
    if cast_job is not None:
        w2_stack, layer2 = cast_job
        _, k2, n2 = w2_stack.shape
        slabs = n2 // LANES
        assert (n_cols // tn) * m_tiles >= slabs
        slab = lambda j, i: jnp.minimum(j * m_tiles + i, slabs - 1)
        in_specs.append(pl.BlockSpec((None, k2, LANES), lambda j, i: (layer2, 0, slab(j, i))))
        args.append(w2_stack)
        out_specs.append(pl.BlockSpec((k2, LANES), lambda j, i: (0, slab(j, i))))
        out_shape.append(jax.ShapeDtypeStruct((k2, n2), BF16))
    return pl.pallas_call(
        functools.partial(_proj_kernel, w_is_transposed=w_is_transposed, q_tiles=q_tiles, q_scale=q_scale,
                          with_cast=cast_job is not None),
        grid=(n_cols // tn, m_tiles),
        in_specs=in_specs,
        out_specs=out_specs,
        out_shape=out_shape,
        scratch_shapes=[pltpu.VMEM(w_tile, BF16)],
        compiler_params=_params(("arbitrary", "arbitrary")),
        name="in_proj",
    )(*args)


def _post_norm_residual(y, x, gate, g):
    return x + gate * (y * lax.rsqrt(jnp.mean(y * y, axis=-1, keepdims=True) + NORM_EPS) * g)


def _out_kernel(*refs, emit_next, emit_gates):
    a_ref, as_ref, w_ref, x_ref, xs_ref = refs[:5]
    mod_p, mod_s = refs[5:8], refs[8:11]
    g_ref, gn_ref = refs[11:13]
    wg_ref = refs[13] if emit_gates else None
    outs = refs[13 + emit_gates:]
    pick = lambda k: outs[k] if len(outs) > k else None

    def finish(a_in, x_in, mod, x_out, h_out, g_out):
        per_row = mod[0].shape[0] > 1
        for r in range(0, a_in.shape[0], OUT_PROJ_SUB_ROWS):
            rows = slice(r, r + OUT_PROJ_SUB_ROWS)
            gate, shift, scale = [m[rows, :] if per_row else m[...] for m in mod]
            y = jnp.dot(a_in[rows, :], w_ref[...], preferred_element_type=F32)
            x_new = _post_norm_residual(y, x_in[rows, :], gate, g_ref[...])
            x_out[rows, :] = x_new
            if emit_next:
                h_out[rows, :] = _pre_norm_mod(x_new, shift, scale, gn_ref[...]).astype(h_out.dtype)
        if emit_gates:
            g_out[...] = _matmul(h_out[...], wg_ref[...].astype(BF16), True)

    @pl.when(pl.program_id(0) == 0)
    def _():
        finish(as_ref, xs_ref, mod_s, outs[1], pick(3), pick(5))

    finish(a_ref, x_ref, mod_p, outs[0], pick(2), pick(4))


def _out_proj(a, a_s, w, x, xs, mod_p, mod_s, layer, g_post, g_pre_next, tm, emit_next, gate_w=None):
    m, d_in = a.shape
    ms = a_s.shape[0]
    d = x.shape[1]
    nxt = min(layer + 1, mod_p.shape[0] - 1)
    tiles_per_seq = m // tm // mod_p.shape[1]
    once = pl.Buffered(1)
    row_spec = lambda width: pl.BlockSpec((tm, width), lambda i: (i, 0))
    fixed_spec = lambda width: pl.BlockSpec((ms, width), lambda i: (0, 0))
    p_spec = lambda l, part: pl.BlockSpec((None, None, 1, d), lambda i: (l, i // tiles_per_seq, 0, part))
    s_spec = lambda l, part: pl.BlockSpec((None, ms, d), lambda i: (l, 0, part), pipeline_mode=once)
    shift, scale, gate = 0, 1, 2
    in_specs = [row_spec(d_in),
                pl.BlockSpec((ms, d_in), lambda i: (0, 0), pipeline_mode=once),
                pl.BlockSpec((d_in, d), lambda i: (0, 0), pipeline_mode=once),
                row_spec(d),
                pl.BlockSpec((ms, d), lambda i: (0, 0), pipeline_mode=once),
                p_spec(layer, gate), p_spec(nxt, shift), p_spec(nxt, scale),
                s_spec(layer, gate), s_spec(nxt, shift), s_spec(nxt, scale),
                pl.BlockSpec((1, d), lambda i: (0, 0)),
                pl.BlockSpec((1, d), lambda i: (0, 0))]
    args = [a, a_s, w, x, xs, mod_p, mod_p, mod_p, mod_s, mod_s, mod_s,
            g_post.reshape(1, d), g_pre_next.reshape(1, d)]
    out_specs = [row_spec(d), fixed_spec(d)]
    out_shape = [jax.ShapeDtypeStruct((m, d), F32), jax.ShapeDtypeStruct((ms, d), F32)]
    if emit_next:
        out_specs += [row_spec(d), fixed_spec(d)]
        out_shape += [jax.ShapeDtypeStruct((m, d), BF16), jax.ShapeDtypeStruct((ms, d), BF16)]
    if gate_w is not None:
        wt_stack, g_layer, row0, n = gate_w
        in_specs.append(pl.BlockSpec((None, n, d), lambda i: (g_layer, row0 // n, 0)))
        args.append(wt_stack)
        out_specs += [row_spec(n), fixed_spec(n)]
        out_shape += [jax.ShapeDtypeStruct((m, n), F32), jax.ShapeDtypeStruct((ms, n), F32)]
    return pl.pallas_call(
        functools.partial(_out_kernel, emit_next=emit_next, emit_gates=gate_w is not None),
        grid=(m // tm,),
        in_specs=in_specs,
        out_specs=out_specs,
        out_shape=out_shape,
        compiler_params=_params(("arbitrary",)),
        name="out_proj_postnorm",
    )(*args)


def _group_select(group_col, values):
    out = values[ATT_GROUP - 1]
    for g in range(ATT_GROUP - 2, -1, -1):
        out = jnp.where(group_col == g, values[g], out)
    return out


def _attn_prompt_kernel(sink_ref, q_ref, kc_ref, kp_ref, vc_ref, vp_ref, z0_ref, z1_ref, o_ref, bias_ref):
    blk = WINDOW
    hd = ATT_HEAD_DIM
    i = pl.program_id(1)
    cols = ATT_GROUP * blk

    @pl.when(i <= 1)
    def _():
        key = lax.broadcasted_iota(jnp.int32, (2 * blk, cols), 0)
        qcol = lax.broadcasted_iota(jnp.int32, (2 * blk, cols), 1)
        dist = (qcol % blk) + blk - key
        valid = (dist >= 0) & (dist < WINDOW) & ((key >= blk) | (i > 0))
        dist_f = dist.astype(F32)
        group = qcol // blk
        for h in range(ATT_KV_HEADS):
            slope = _group_select(group, [_alibi_slope(ATT_GROUP * h + g) for g in range(ATT_GROUP)])
            bias_ref[h] = jnp.where(valid, (-slope * dist_f) * LOG2E, -jnp.inf)

    kcat = jnp.concatenate([kp_ref[...], kc_ref[...]], axis=0)
    vcat = jnp.concatenate([vp_ref[...], vc_ref[...]], axis=0)
    half = z0_ref.shape[1]
    vt_pairs = [jnp.concatenate([vcat[:blk, c * LANES:(c + 1) * LANES].T, vcat[blk:, c * LANES:(c + 1) * LANES].T],
                                axis=1) for c in range(ATT_KV_WIDTH // LANES)]
    zero = jnp.zeros((hd, blk), BF16)
    for h in range(ATT_KV_HEADS):
        pair, odd = divmod(h, 2)
        k_pair = kcat[:, pair * LANES:(pair + 1) * LANES]
        vt = vt_pairs[pair][hd:] if odd else vt_pairs[pair][:hd]
        lhs = jnp.concatenate([vt, jnp.ones_like(vt)], axis=0)
        for t in range(2):
            c0 = (2 * h + t) * LANES
            qt_pair = q_ref[:, c0:c0 + LANES].T
            outs = []
            for e in range(2):
                g = 2 * t + e
                qt = qt_pair[e * hd:(e + 1) * hd]
                rhs = jnp.concatenate([zero, qt] if odd else [qt, zero], axis=0)
                s = jnp.dot(k_pair, rhs, preferred_element_type=F32) + bias_ref[h, :, g * blk:(g + 1) * blk]
                sink = sink_ref[ATT_GROUP * h + g] * LOG2E
                mx = jnp.maximum(jnp.max(s, axis=0, keepdims=True), sink)
                p = jnp.exp2(s - mx).astype(BF16)
                oa = jnp.dot(lhs, p, preferred_element_type=F32)
                den = oa[hd:hd + 1] + jnp.exp2(sink - mx)
                outs.append(oa[:hd] * (1.0 / den))
            ot = jnp.concatenate(outs, axis=0).T
            z_ref, zc = (z0_ref, c0) if c0 < half else (z1_ref, c0 - half)
            o_ref[:, c0:c0 + LANES] = (ot * _silu(z_ref[:, zc:zc + LANES])).astype(o_ref.dtype)


def _attn_prompt(qkv, z, sinks, batch, seq):
    blk = WINDOW
    nb = seq // blk
    inner = ATT_Q_HEADS * ATT_HEAD_DIM
    kvw = ATT_KV_WIDTH
    k_col = inner // kvw
    v_col = k_col + 1
    half = inner // 2
    cur = lambda b, i: b * nb + i
    prev = lambda b, i: b * nb + jnp.maximum(i - 1, 0)
    return pl.pallas_call(
        _attn_prompt_kernel,
        grid=(batch, nb),
        in_specs=[pl.BlockSpec(memory_space=pltpu.SMEM),
                  pl.BlockSpec((blk, inner), lambda b, i: (cur(b, i), 0)),
                  pl.BlockSpec((blk, kvw), lambda b, i: (cur(b, i), k_col)),
                  pl.BlockSpec((blk, kvw), lambda b, i: (prev(b, i), k_col)),
                  pl.BlockSpec((blk, kvw), lambda b, i: (cur(b, i), v_col)),
                  pl.BlockSpec((blk, kvw), lambda b, i: (prev(b, i), v_col)),
                  pl.BlockSpec((blk, half), lambda b, i: (cur(b, i), 0)),
                  pl.BlockSpec((blk, half), lambda b, i: (cur(b, i), 1))],
        out_specs=pl.BlockSpec((blk, inner), lambda b, i: (cur(b, i), 0)),
        out_shape=jax.ShapeDtypeStruct((batch * seq, inner), BF16),
        scratch_shapes=[pltpu.VMEM((ATT_KV_HEADS, 2 * blk, ATT_GROUP * blk), F32)],
        compiler_params=_params(("arbitrary", "arbitrary")),
        name="attn_prompt",
    )(sinks, qkv, qkv, qkv, qkv, qkv, z, z)


def _attn_sample_kernel(*refs, n_new, has_acc):
    sink_ref, bias_ref = refs[0], refs[-1]
    seq_refs = refs[1:8] + refs[8 + 2 * has_acc:-1]
    n_seq, tp = refs[1].shape[:2]
    w = refs[6].shape[3]
    rows = ATT_GROUP * tp

    @pl.when(pl.program_id(0) == 0)
    def _():
        t_row = lax.broadcasted_iota(jnp.int32, (rows, 2 * w), 0) % tp
        key = lax.broadcasted_iota(jnp.int32, (rows, 2 * w), 1)
        dist = t_row + w - key
        valid = (dist >= 0) & (dist < WINDOW) & (key < w + n_new)
        dist_f = dist.astype(F32)
        group = lax.broadcasted_iota(jnp.int32, (rows, 2 * w), 0) // tp
        for h in range(ATT_KV_HEADS):
            slope = _group_select(group, [_alibi_slope(ATT_GROUP * h + g) for g in range(ATT_GROUP)])
            bias_ref[h] = jnp.where(valid, -slope * dist_f, -jnp.inf)

    for sq in range(n_seq):
        _attn_sample_seq(sink_ref, bias_ref, *[r.at[sq] for r in seq_refs], n_new=n_new)


def _attn_sample_seq(sink_ref, bias_ref, q_ref, kn_ref, vn_ref, z0_ref, z1_ref, kc_ref, vc_ref, o_ref, ko_ref, vo_ref,
                     *, n_new):
    hd = ATT_HEAD_DIM
    tp = q_ref.shape[0]
    w = kc_ref.shape[2]
    rows = ATT_GROUP * tp
    nt_dims = (((1,), (1,)), ((), ()))
    pad = jnp.zeros((w - tp, kn_ref.shape[1]), F32)
    kn_pad = jnp.concatenate([kn_ref[...], pad], axis=0)
    vn_pad = jnp.concatenate([vn_ref[...], pad], axis=0)
    kn_b = kn_pad.astype(BF16)
    vn_b = vn_pad.astype(BF16)
    n_pairs = ATT_KV_WIDTH // LANES
    knt_pairs = [kn_pad[:, c * LANES:(c + 1) * LANES].T for c in range(n_pairs)]
    vnt_pairs = [vn_pad[:, c * LANES:(c + 1) * LANES].T for c in range(n_pairs)]
    is_new = lax.broadcasted_iota(jnp.int32, (hd, w), 1) < n_new
    group_col = lax.broadcasted_iota(jnp.int32, (rows, 1), 0) // tp
    half = z0_ref.shape[1]
    hs = range(ATT_KV_HEADS)
    heads = [[ATT_GROUP * h + g for g in range(ATT_GROUP)] for h in hs]
    kts = [kc_ref[h] for h in hs]
    vts = [vc_ref[h] for h in hs]
    qss = [(jnp.concatenate([q_ref[:, j * hd:(j + 1) * hd] for j in heads[h]], axis=0) * (hd ** -0.5)).astype(BF16)
           for h in hs]
    ss = [jnp.concatenate([jnp.dot(qss[h], kts[h].astype(BF16), preferred_element_type=F32),
                           lax.dot_general(qss[h], kn_b[:, h * hd:(h + 1) * hd], nt_dims,
                                           preferred_element_type=F32)], axis=1) + bias_ref[h] for h in hs]
    sinks = [_group_select(group_col, [sink_ref[j] for j in heads[h]]) for h in hs]
    mxs = [jnp.maximum(jnp.max(ss[h], axis=-1, keepdims=True), sinks[h]) for h in hs]
    ps = [jnp.exp(ss[h] - mxs[h]) for h in hs]
    dens = [jnp.sum(ps[h], axis=-1, keepdims=True) + jnp.exp(sinks[h] - mxs[h]) for h in hs]
    pbs = [p.astype(BF16) for p in ps]
    outs = [(lax.dot_general(pbs[h][:, :w], vts[h].astype(BF16), nt_dims, preferred_element_type=F32)
             + jnp.dot(pbs[h][:, w:], vn_b[:, h * hd:(h + 1) * hd], preferred_element_type=F32)) * (1.0 / dens[h])
            for h in hs]
    for h in hs:
        for g, j in enumerate(heads[h]):
            c0 = j * hd
            z_ref, zc = (z0_ref, c0) if c0 < half else (z1_ref, c0 - half)
            z = z_ref[:, zc:zc + hd]
            o_ref[:, c0:c0 + hd] = (outs[h][g * tp:(g + 1) * tp] * _silu(z)).astype(o_ref.dtype)
    for h in hs:
        pair, odd = divmod(h, 2)
        knt = knt_pairs[pair][odd * hd:(odd + 1) * hd]
        vnt = vnt_pairs[pair][odd * hd:(odd + 1) * hd]
        ko_ref[h] = pltpu.roll(jnp.where(is_new, knt, kts[h]), w - n_new, axis=1)
        vo_ref[h] = pltpu.roll(jnp.where(is_new, vnt, vts[h]), w - n_new, axis=1)


def _attn_sample(qkv3, z3, sinks, cache_kt, cache_vt, layer, n_new, k_acc, v_acc):
    nbatch, tp, _ = z3.shape
    w = cache_kt.shape[4]
    inner = ATT_Q_HEADS * ATT_HEAD_DIM
    kvw = ATT_KV_WIDTH
    k_col = inner // kvw
    half = inner // 2
    ns = ATTN_SAMPLE_SEQS_PER_STEP
    cache_spec = pl.BlockSpec((None, ns, ATT_KV_HEADS, ATT_HEAD_DIM, w), lambda b: (layer, b, 0, 0, 0))
    has_acc = k_acc is not None
    in_specs = [pl.BlockSpec(memory_space=pltpu.SMEM),
                pl.BlockSpec((ns, tp, inner), lambda b: (b, 0, 0)),
                pl.BlockSpec((ns, tp, kvw), lambda b: (b, 0, k_col)),
                pl.BlockSpec((ns, tp, kvw), lambda b: (b, 0, k_col + 1)),
                pl.BlockSpec((ns, tp, half), lambda b: (b, 0, 0)),
                pl.BlockSpec((ns, tp, half), lambda b: (b, 0, 1)),
                cache_spec, cache_spec]
    args = [sinks, qkv3, qkv3, qkv3, z3, z3, cache_kt, cache_vt]
    if has_acc:
        in_specs += [pl.BlockSpec(memory_space=pl.ANY)] * 2
        args += [k_acc, v_acc]
    return pl.pallas_call(
        functools.partial(_attn_sample_kernel, n_new=n_new, has_acc=has_acc),
        grid=(nbatch // ns,),
        in_specs=in_specs,
        out_specs=[pl.BlockSpec((ns, tp, inner), lambda b: (b, 0, 0)), cache_spec, cache_spec],
        out_shape=[jax.ShapeDtypeStruct((nbatch, tp, inner), BF16),
                   jax.ShapeDtypeStruct(cache_kt.shape, F32), jax.ShapeDtypeStruct(cache_vt.shape, F32)],
        scratch_shapes=[pltpu.VMEM((ATT_KV_HEADS, ATT_GROUP * tp, 2 * w), F32)],
        input_output_aliases={len(args) - 2: 1, len(args) - 1: 2} if has_acc else {},
        compiler_params=_params(("arbitrary",)),
        name="attn_sample",
    )(*args)


def _lane_scan(x, op, lane):
    n = x.shape[-1]
    shift = 1
    while shift < n:
        x = jnp.where(lane >= shift, op(x, pltpu.roll(x, shift, axis=x.ndim - 1)), x)
        shift *= 2
    return x


def _gate_prep_kernel(bias_ref, g_ref, a_ref, em_ref, wt_ref, u_ref, w_ref, mn_ref, *, chunks_per_seq):
    rows, length = g_ref.shape[2], g_ref.shape[3]
    lane = lax.broadcasted_iota(jnp.int32, (rows, length), 1)
    chunk = lax.broadcasted_iota(jnp.int32, (rows, length), 0) % chunks_per_seq
    for h in range(M_HEADS):
        li = g_ref[0, h] + bias_ref[h]
        lf = _log_sigmoid(g_ref[1, h] + bias_ref[M_HEADS + h])
        bcum = _lane_scan(lf, jnp.add, lane)
        w = li - bcum
        cmax = _lane_scan(w, jnp.maximum, lane)
        e = jnp.broadcast_to(bcum[:, length - 1:length], (rows, length))
        y = e + jnp.broadcast_to(cmax[:, length - 1:length], (rows, length))
        shift = 1
        while shift < chunks_per_seq:
            e_prev = pltpu.roll(e, shift, axis=0)
            y_prev = pltpu.roll(y, shift, axis=0)
            take = chunk >= shift
            y = jnp.where(take, jnp.maximum(y_prev + e, y), y)
            e = jnp.where(take, e_prev + e, e)
            shift *= 2
        m_incl = jnp.maximum(e, y)
        m_prev = jnp.where(chunk >= 1, pltpu.roll(m_incl, 1, axis=0), 0.0)
        u = -jnp.maximum(m_prev, cmax)
        u_last = jnp.broadcast_to(u[:, length - 1:length], (rows, length))
        a_ref[h] = jnp.exp(m_prev + u)
        em_ref[h] = jnp.exp(u - bcum)
        wt_ref[h] = jnp.exp(w + u_last)
        u_ref[h] = u
        w_ref[h] = w
        mn_ref[h] = m_incl


def _gate_prep(gates_t, b_if, chunks_per_seq):
    shape = gates_t.shape[1:]
    out = jax.ShapeDtypeStruct(shape, F32)
    return pl.pallas_call(
        functools.partial(_gate_prep_kernel, chunks_per_seq=chunks_per_seq),
        in_specs=[pl.BlockSpec(memory_space=pltpu.SMEM), pl.BlockSpec(memory_space=pltpu.VMEM)],
        out_specs=[pl.BlockSpec(memory_space=pltpu.VMEM)] * 6,
        out_shape=[out] * 6,
        name="mlstm_gate_prep",
    )(b_if, gates_t)


def _head_norm_gate(hout, g_row, o, z):
    hn = hout * lax.rsqrt(jnp.mean(hout * hout, axis=-1, keepdims=True) + NORM_EPS) * g_row
    return hn * (z / ((1.0 + jnp.exp2(o * -LOG2E)) * (1.0 + jnp.exp2(z * -LOG2E))))


def _mlstm_prompt_kernel(q_ref, k_ref, v_ref, o_ref, z_ref, col_ref, row_ref, gh_ref,
                         out_ref, s_out_ref, n_out_ref, sn_ref):
    c = pl.program_id(1)
    n_sub, _, length = row_ref.shape
    hs = M_HEADS

    @pl.when(c == 0)
    def _():
        sn_ref[...] = jnp.zeros_like(sn_ref)

    t_idx = lax.broadcasted_iota(jnp.int32, (length, length), 0)
    s_idx = lax.broadcasted_iota(jnp.int32, (length, length), 1)
    causal = s_idx <= t_idx
    ones = jnp.ones((length, LANES), BF16)
    nt_dims = (((1,), (1,)), ((), ()))
    for sub in range(n_sub):
        r = slice(sub * length, (sub + 1) * length)
        col = col_ref[r, :]
        for h in range(hs):
            a_b = jnp.broadcast_to(col[:, h:h + 1], (length, LANES))
            e_b = jnp.broadcast_to(col[:, hs + h:hs + h + 1], (length, LANES))
            u_b = jnp.broadcast_to(col[:, 2 * hs + h:2 * hs + h + 1], (length, length))
            a_last = a_b[length - 1:length, :1]
            qb = q_ref[r, h * M_DK:(h + 1) * M_DK]
            kb = k_ref[r, h * M_DK:(h + 1) * M_DK]
            v1 = jnp.concatenate([v_ref[r, h * M_DV:(h + 1) * M_DV], ones], axis=1)
            dmat = jnp.where(causal, jnp.exp(u_b + row_ref[sub, h:h + 1, :]), 0.0)
            qk = lax.dot_general(qb, kb, nt_dims, preferred_element_type=F32) * dmat
            sn_old = sn_ref[h]
            inter = jnp.dot(qb, sn_old.astype(BF16), preferred_element_type=F32)
            intra = jnp.dot(qk.astype(BF16), v1, preferred_element_type=F32)
            a_b3 = jnp.concatenate([a_b] * (1 + M_DV // LANES), axis=1)
            tot = a_b3 * inter + intra
            inv = 1.0 / jnp.maximum(jnp.abs(tot[:, M_DV:]), e_b)
            hout = tot[:, :M_DV] * jnp.concatenate([inv] * (M_DV // LANES), axis=1)
            kwt = (kb.astype(F32).T * row_ref[sub, hs + h:hs + h + 1, :]).astype(BF16)
            sn_ref[h] = a_last * sn_old + jnp.dot(kwt, v1, preferred_element_type=F32)
            sl = slice(h * M_DV, (h + 1) * M_DV)
            out_ref[r, sl] = _head_norm_gate(hout, gh_ref[:, sl], o_ref[r, sl], z_ref[r, sl]).astype(out_ref.dtype)

    @pl.when(c == pl.num_programs(1) - 1)
    def _():
        for h in range(hs):
            s_out_ref[h] = sn_ref[h, :, :M_DV]
            n_out_ref[h:h + 1, :] = sn_ref[h, :, M_DV:].T[:1, :]


def _mlstm_prompt(qkv, oz, col, rows, g_head, batch, seq):
    n_sub = MLSTM_CHUNKS_PER_STEP
    length = M_CHUNK
    span = n_sub * length
    nc = seq // span
    inner = M_HEADS * M_DV
    qkw = M_QK_WIDTH
    v_col = 2 * qkw // inner
    row = lambda b, c: b * nc + c
    return pl.pallas_call(
        _mlstm_prompt_kernel,
        grid=(batch, nc),
        in_specs=[pl.BlockSpec((span, qkw), lambda b, c: (row(b, c), 0)),
                  pl.BlockSpec((span, qkw), lambda b, c: (row(b, c), 1)),
                  pl.BlockSpec((span, inner), lambda b, c: (row(b, c), v_col)),
                  pl.BlockSpec((span, inner), lambda b, c: (row(b, c), 0)),
                  pl.BlockSpec((span, inner), lambda b, c: (row(b, c), 1)),
                  pl.BlockSpec((span, 3 * M_HEADS), lambda b, c: (row(b, c), 0)),
                  pl.BlockSpec((n_sub, 2 * M_HEADS, length), lambda b, c: (row(b, c), 0, 0)),
                  pl.BlockSpec((1, inner), lambda b, c: (0, 0))],
        out_specs=[pl.BlockSpec((span, inner), lambda b, c: (row(b, c), 0)),
                   pl.BlockSpec((None, M_HEADS, M_DK, M_DV), lambda b, c: (b, 0, 0, 0)),
                   pl.BlockSpec((None, M_HEADS, M_DK), lambda b, c: (b, 0, 0))],
        out_shape=[jax.ShapeDtypeStruct((batch * seq, inner), BF16),
                   jax.ShapeDtypeStruct((batch, M_HEADS, M_DK, M_DV), F32),
                   jax.ShapeDtypeStruct((batch, M_HEADS, M_DK), F32)],
        scratch_shapes=[pltpu.VMEM((M_HEADS, M_DK, M_DV + LANES), F32)],
        compiler_params=_params(("arbitrary", "arbitrary")),
        name="mlstm_prompt_scan",
    )(qkv, qkv, qkv, oz, oz, col, rows, g_head.reshape(1, inner))


def _prefix_scan(x, op, axis, n):
    idx = lax.broadcasted_iota(jnp.int32, x.shape, axis)
    take = (lambda t: x[t:t + 1, :]) if axis == 0 else (lambda t: x[:, t:t + 1])
    run = take(0)
    out = jnp.broadcast_to(run, x.shape)
    for t in range(1, n):
        run = op(run, take(t))
        out = jnp.where(idx >= t, run, out)
    return out


def _mlstm_sample_kernel(*refs, n_new, has_acc):
    ins, outs = refs[:13], refs[13 + has_acc:]
    shared = (7, 8, 10)
    for sq in range(ins[0].shape[0]):
        seq_ins = [r if i in shared else r.at[sq] for i, r in enumerate(ins)]
        _mlstm_sample_seq(*seq_ins, *[r.at[sq] for r in outs], n_new=n_new)


def _mlstm_sample_seq(q_ref, k_ref, v_ref, o_ref, z_ref, g_ref, gt_ref, brow_ref, bcol_ref, mrow_ref, gh_ref,
                      s_in_ref, n_in_ref, out_ref, s_out_ref, n_out_ref, m_out_ref, *, n_new):
    tp = q_ref.shape[0]
    last = n_new - 1
    hs = M_HEADS
    g = g_ref[...] + brow_ref[...]
    bcum_c = _prefix_scan(_log_sigmoid(g[:, hs:]), jnp.add, 0, n_new)
    w_c = g[:, :hs] - bcum_c
    m_prev_c = mrow_ref[...]
    u_c = -jnp.maximum(m_prev_c, _prefix_scan(w_c, jnp.maximum, 0, n_new))
    a_c = jnp.exp(m_prev_c + u_c)
    e_c = jnp.exp(u_c - bcum_c)
    real_c = lax.broadcasted_iota(jnp.int32, (tp, hs), 0) < n_new
    wt_c = jnp.where(real_c, jnp.exp(w_c + u_c[last:last + 1, :]), 0.0)
    m_out_ref[...] = bcum_c[last:last + 1, :] - u_c[last:last + 1, :]
    gt = gt_ref[...] + bcol_ref[...]
    w_r = gt[:hs, :] - _prefix_scan(_log_sigmoid(gt[hs:, :]), jnp.add, 1, n_new)
    keys = w_r.shape[1]
    t_idx = lax.broadcasted_iota(jnp.int32, (tp, keys), 0)
    s_idx = lax.broadcasted_iota(jnp.int32, (tp, keys), 1)
    causal = s_idx <= t_idx
    k_pad = jnp.concatenate([k_ref[...], jnp.zeros((keys - tp, k_ref.shape[1]), F32)], axis=0)
    v_pad = jnp.concatenate([v_ref[...], jnp.zeros((keys - tp, v_ref.shape[1]), F32)], axis=0).astype(BF16)
    wt_pad = jnp.concatenate([wt_c, jnp.zeros((keys - tp, hs), F32)], axis=0)
    for h in range(hs):
        a_col = a_c[:, h:h + 1]
        a_last = a_col[last:last + 1, :]
        qf = q_ref[:, h * M_DK:(h + 1) * M_DK] * (M_DK ** -0.5)
        qb = qf.astype(BF16)
        kf = k_pad[:, h * M_DK:(h + 1) * M_DK]
        vb = v_pad[:, h * M_DV:(h + 1) * M_DV]
        dmat = jnp.where(causal, jnp.exp(u_c[:, h:h + 1] + w_r[h:h + 1, :]), 0.0)
        qk = lax.dot_general(qb, kf.astype(BF16), (((1,), (1,)), ((), ())), preferred_element_type=F32) * dmat
        s_old = s_in_ref[h]
        n_old = n_in_ref[h:h + 1, :]
        num = a_col * jnp.dot(qb, s_old.astype(BF16), preferred_element_type=F32) \
            + jnp.dot(qk.astype(BF16), vb, preferred_element_type=F32)
        den = a_col * jnp.sum(qf * n_old, axis=-1, keepdims=True) + jnp.sum(qk, axis=-1, keepdims=True)
        hout = num / jnp.maximum(jnp.abs(den), e_c[:, h:h + 1])
        kw = kf * wt_pad[:, h:h + 1]
        s_out_ref[h] = a_last * s_old + jnp.dot(kw.T.astype(BF16), vb, preferred_element_type=F32)
        n_out_ref[h:h + 1, :] = a_last * n_old + jnp.sum(kw, axis=0, keepdims=True)
        sl = slice(h * M_DV, (h + 1) * M_DV)
        out_ref[:, sl] = _head_norm_gate(hout, gh_ref[:, sl], o_ref[:, sl], z_ref[:, sl]).astype(out_ref.dtype)


def _mlstm_sample(qkv3, oz3, gates3, gates3_t, b_if, state_c, state_n, state_m, g_head, layer, n_new, c_acc):
    nbatch, tp, _ = qkv3.shape
    hs = M_HEADS
    inner = hs * M_DV
    qkw = M_QK_WIDTH
    v_col = 2 * qkw // inner
    keys = gates3_t.shape[2]
    m_row = state_m[layer].reshape(nbatch, 1, hs)
    has_acc = c_acc is not None
    ns = MLSTM_SAMPLE_SEQS_PER_STEP
    in_specs = [pl.BlockSpec((ns, tp, qkw), lambda b: (b, 0, 0)),
                pl.BlockSpec((ns, tp, qkw), lambda b: (b, 0, 1)),
                pl.BlockSpec((ns, tp, inner), lambda b: (b, 0, v_col)),
                pl.BlockSpec((ns, tp, inner), lambda b: (b, 0, 0)),
                pl.BlockSpec((ns, tp, inner), lambda b: (b, 0, 1)),
                pl.BlockSpec((ns, tp, 2 * hs), lambda b: (b, 0, 0)),
                pl.BlockSpec((ns, 2 * hs, keys), lambda b: (b, 0, 0)),
                pl.BlockSpec((1, 2 * hs), lambda b: (0, 0)),
                pl.BlockSpec((2 * hs, 1), lambda b: (0, 0)),
                pl.BlockSpec((ns, 1, hs), lambda b: (b, 0, 0)),
                pl.BlockSpec((1, inner), lambda b: (0, 0)),
                pl.BlockSpec((None, ns, hs, M_DK, M_DV), lambda b: (layer, b, 0, 0, 0)),
                pl.BlockSpec((None, ns, hs, M_DK), lambda b: (layer, b, 0, 0))]
    args = [qkv3, qkv3, qkv3, oz3, oz3, gates3, gates3_t, b_if.reshape(1, 2 * hs), b_if.reshape(2 * hs, 1),
            m_row, g_head.reshape(1, inner), state_c, state_n]
    if has_acc:
        in_specs.append(pl.BlockSpec(memory_space=pl.ANY))
        args.append(c_acc)
    return pl.pallas_call(
        functools.partial(_mlstm_sample_kernel, n_new=n_new, has_acc=has_acc),
        grid=(nbatch // ns,),
        in_specs=in_specs,
        out_specs=[pl.BlockSpec((ns, tp, inner), lambda b: (b, 0, 0)),
                   pl.BlockSpec((None, ns, hs, M_DK, M_DV), lambda b: (layer, b, 0, 0, 0)),
                   pl.BlockSpec((ns, hs, M_DK), lambda b: (b, 0, 0)),
                   pl.BlockSpec((ns, 1, hs), lambda b: (b, 0, 0))],
        out_shape=[jax.ShapeDtypeStruct((nbatch, tp, inner), BF16),
                   jax.ShapeDtypeStruct(state_c.shape, F32),
                   jax.ShapeDtypeStruct((nbatch, hs, M_DK), F32),
                   jax.ShapeDtypeStruct((nbatch, 1, hs), F32)],
        input_output_aliases={len(args) - 1: 1} if has_acc else {},
        compiler_params=_params(("arbitrary",)),
        name="mlstm_sample_step",
    )(*args)


def kernel(x_prompt, x_sample, c_prompt, c_sample, cache_k, cache_v, state_C, state_n, state_m, w_ada, b_ada,
           g_pre, g_post, w_in_attn, sinks, w_out_attn, w_in_mlstm, b_if_mlstm, g_head_mlstm, w_out_mlstm):
    batch, seq, d = x_prompt.shape
    dec_batch, dec_seq, _ = x_sample.shape
    depth = w_ada.shape[0]
    tp = SAMPLE_ROWS
    rows_p = batch * seq
    rows_s = dec_batch * tp
    tm = 1024
    att_inner = ATT_Q_HEADS * ATT_HEAD_DIM
    att_qkv = att_inner + 2 * ATT_KV_WIDTH
    m_qkv = 2 * M_QK_WIDTH + M_HEADS * M_DV
    m_cols = m_qkv + 2 * M_HEADS * M_DV

    c_rows = batch + dec_batch
    c_pad = -c_rows % 8
    c_all = jnp.concatenate([c_prompt, c_sample, jnp.zeros((c_pad, d), F32)], axis=0)
    mod = _ada_all_layers(c_all, w_ada, b_ada)

    xp = x_prompt.reshape(rows_p, d)
    xs = jnp.pad(x_sample, ((0, 0), (0, tp - dec_seq), (0, 0))).reshape(rows_s, d)
    cache_kt = cache_k.transpose(0, 1, 3, 4, 2)
    cache_vt = cache_v.transpose(0, 1, 3, 4, 2)
    w_in_mlstm_t = w_in_mlstm.transpose(0, 2, 1)

    mod_p = mod[:, :batch].reshape(depth, batch, 1, 3 * d)
    mod_s = jnp.repeat(mod[:, batch:c_rows], tp, axis=1)

    hp = _prenorm(xp, mod_p[0, :, :, :d], mod_p[0, :, :, d:2 * d], g_pre[0], tm)
    hs = _prenorm(xs, mod_s[:1, :, :d], mod_s[:1, :, d:2 * d], g_pre[0], rows_s)
    gp = gs = None
    kp_l, vp_l = [], []
    cp_l, np_l, mp_l, ns_l, ms_l = [], [], [], [], []
    c_sample_new = k_sample_new = v_sample_new = None
    for l in range(depth):
        j = l // 2
        if l % 2 == 0:
            h_tail = hp.reshape(batch, seq, d)[:, seq - WINDOW:].reshape(batch * WINDOW, d)
            qkv_p, qkv_s = _proj(hp, jnp.concatenate([hs, h_tail], axis=0), w_in_attn, j, 0, att_qkv, tm, 1024, BF16,
                                 q_tiles=att_inner // 1024, q_scale=ATT_HEAD_DIM ** -0.5 * LOG2E)
            z_p, z_s, w_out = _proj(hp, hs, w_in_attn, j, att_qkv, att_inner, tm, 1024, F32,
                                    cast_job=(w_out_attn, j))
            ap = _attn_prompt(qkv_p, z_p, sinks[j], batch, seq)
            qkv_s3 = qkv_s.reshape(-1, tp, att_qkv)
            a_s, k_sample_new, v_sample_new = _attn_sample(qkv_s3, z_s.reshape(dec_batch, tp, att_inner), sinks[j],
                                                           cache_kt, cache_vt, j, dec_seq, k_sample_new, v_sample_new)
            a_s = a_s.reshape(rows_s, att_inner)
            k0, v0 = att_inner, att_inner + ATT_KV_WIDTH
            kv_tail = qkv_s[rows_s:].reshape(batch, WINDOW, att_qkv)
            kv_shape = (ATT_KV_HEADS, ATT_HEAD_DIM)
            kp_l.append(kv_tail[:, :, k0:v0].reshape((batch, WINDOW) + kv_shape))
            vp_l.append(kv_tail[:, :, v0:v0 + ATT_KV_WIDTH].reshape((batch, WINDOW) + kv_shape))
        else:
            qkv_p, qkv_s = _proj(hp, hs, w_in_mlstm_t, j, 0, m_qkv, tm, 1024, BF16, q_tiles=M_QK_WIDTH // 1024,
                                 q_scale=M_DK ** -0.5, w_is_transposed=True)
            oz_p, oz_s, w_out = _proj(hp, hs, w_in_mlstm_t, j, m_qkv, m_cols - m_qkv, tm, 1024, F32,
                                      w_is_transposed=True, cast_job=(w_out_mlstm, j))
            nc = seq // M_CHUNK
            gates_t = gp.T.reshape(2, M_HEADS, batch * nc, M_CHUNK)
            a_q, em_q, wt_q, u_q, w_q, mn_q = _gate_prep(gates_t, b_if_mlstm[j], nc)
            col = jnp.stack([a_q, em_q, u_q]).transpose(2, 3, 0, 1).reshape(rows_p, 3 * M_HEADS)
            rows = jnp.concatenate([w_q, wt_q], axis=0).transpose(1, 0, 2)
            ap, c_new, n_new = _mlstm_prompt(qkv_p, oz_p, col, rows, g_head_mlstm[j], batch, seq)
            cp_l.append(c_new)
            np_l.append(n_new)
            mp_l.append(mn_q.reshape(M_HEADS, batch, nc, M_CHUNK)[:, :, nc - 1, 0].T)
            gs3 = gs.reshape(dec_batch, tp, 2 * M_HEADS)
            gs3_t = jnp.pad(gs3.transpose(0, 2, 1), ((0, 0), (0, 0), (0, LANES - tp)))
            a_s, c_sample_new, n_new, m_new = _mlstm_sample(
                qkv_s.reshape(dec_batch, tp, m_qkv), oz_s.reshape(dec_batch, tp, m_cols - m_qkv), gs3, gs3_t,
                b_if_mlstm[j], state_C, state_n, state_m, g_head_mlstm[j], j, dec_seq, c_sample_new)
            a_s = a_s.reshape(rows_s, M_HEADS * M_DV)
            ns_l.append(n_new)
            ms_l.append(m_new.reshape(dec_batch, M_HEADS))
        nxt = min(l + 1, depth - 1)
        next_is_mlstm = l + 1 < depth and (l + 1) % 2 == 1
        gate_w = (w_in_mlstm_t, (l + 1) // 2, m_cols, 2 * M_HEADS) if next_is_mlstm else None
        outs = _out_proj(ap, a_s, w_out, xp, xs, mod_p, mod_s, l, g_post[l], g_pre[nxt], 512,
                         emit_next=l + 1 < depth, gate_w=gate_w)
        xp, xs = outs[:2]
        if l + 1 < depth:
            hp, hs = outs[2:4]
        if next_is_mlstm:
            gp, gs = outs[4:]

    y_prompt = xp.reshape(batch, seq, d)
    y_sample = xs.reshape(dec_batch, tp, d)[:, :dec_seq]
    to_cache_layout = lambda c: c.transpose(0, 1, 4, 2, 3)
    return (y_prompt, y_sample, jnp.stack(kp_l), jnp.stack(vp_l), to_cache_layout(k_sample_new),
            to_cache_layout(v_sample_new), jnp.stack(cp_l), jnp.stack(np_l), jnp.stack(mp_l), c_sample_new,
            jnp.stack(ns_l), jnp.stack(ms_l))
```

```python
import functools

import jax
import jax.numpy as jnp
from jax import lax
from jax.experimental import pallas as pl
from jax.experimental.pallas import tpu as pltpu

F32 = jnp.float32
BF16 = jnp.bfloat16

NORM_EPS = 1e-6
WINDOW = 128
ATT_HEAD_DIM = 64
ATT_KV_HEADS = 8
ATT_GROUP = 4
ATT_Q_HEADS = ATT_KV_HEADS * ATT_GROUP
ATT_KV_WIDTH = ATT_KV_HEADS * ATT_HEAD_DIM
M_HEADS = 8
M_DK = 128
M_DV = 256
M_QK_WIDTH = M_HEADS * M_DK
M_CHUNK = 128
SAMPLE_ROWS = 8
LANES = 128
LOG2E = 1.4426950408889634

PROJ_ROWS, PROJ_COLS = 1024, 1024
OUT_PROJ_ROWS = 512
OUT_PROJ_SUB_ROWS = 128
ADA_COLS = 1024
PRENORM_ROWS = 1024
ATTN_SAMPLE_SEQS_PER_STEP = 2
MLSTM_SAMPLE_SEQS_PER_STEP = 4
MLSTM_CHUNKS_PER_STEP = 2
V7X_VMEM_LIMIT = 56 * 1024 * 1024


def _params(sem, vmem=V7X_VMEM_LIMIT):
    return pltpu.CompilerParams(dimension_semantics=sem, vmem_limit_bytes=vmem)


def _sigmoid(x):
    return 1.0 / (1.0 + jnp.exp2(x * -LOG2E))


def _silu(x):
    return x * _sigmoid(x)


def _log_sigmoid(x):
    return jnp.minimum(x, 0.0) - jnp.log1p(jnp.exp(-jnp.abs(x)))


def _alibi_slope(head):
    return float(2.0 ** (-8.0 * (head + 1) / ATT_Q_HEADS))


def _ada_kernel(c_ref, w_ref, b_ref, o_ref):
    s = _silu(c_ref[...]).astype(BF16)
    o_ref[...] = jnp.dot(s, w_ref[...].astype(BF16), preferred_element_type=F32) + b_ref[...]


def _ada_all_layers(c_all, w_ada, b_ada, tn=ADA_COLS):
    depth, d, n = w_ada.shape
    r = c_all.shape[0]
    return pl.pallas_call(
        _ada_kernel,
        grid=(depth, n // tn),
        in_specs=[
            pl.BlockSpec((r, d), lambda l, j: (0, 0)),
            pl.BlockSpec((None, d, tn), lambda l, j: (l, 0, j)),
            pl.BlockSpec((None, 1, tn), lambda l, j: (l, 0, j)),
        ],
        out_specs=pl.BlockSpec((None, r, tn), lambda l, j: (l, 0, j)),
        out_shape=jax.ShapeDtypeStruct((depth, r, n), F32),
        compiler_params=_params(("arbitrary", "arbitrary")),
        name="adaln_mod",
    )(c_all, w_ada, b_ada.reshape(depth, 1, n))


def _pre_norm_mod(x, shift, scale, g):
    y = x * lax.rsqrt(jnp.mean(x * x, axis=-1, keepdims=True) + NORM_EPS) * g
    return y * (1.0 + scale) + shift


def _prenorm_kernel(x_ref, sh_ref, sc_ref, g_ref, h_ref):
    h_ref[...] = _pre_norm_mod(x_ref[...], sh_ref[...], sc_ref[...], g_ref[...]).astype(h_ref.dtype)


def _prenorm(x, shift, scale, g, tm):
    m, d = x.shape
    groups, r, _ = shift.shape
    tiles_per_group = m // tm // groups
    mod_spec = pl.BlockSpec((None, r, d), lambda i: (i // tiles_per_group, 0, 0))
    return pl.pallas_call(
        _prenorm_kernel,
        grid=(m // tm,),
        in_specs=[pl.BlockSpec((tm, d), lambda i: (i, 0)), mod_spec, mod_spec,
                  pl.BlockSpec((1, d), lambda i: (0, 0))],
        out_specs=pl.BlockSpec((tm, d), lambda i: (i, 0)),
        out_shape=jax.ShapeDtypeStruct((m, d), BF16),
        compiler_params=_params(("arbitrary",)),
        name="prenorm_mod",
    )(x, shift, scale, g.reshape(1, d))


def _matmul(a, w, w_is_transposed):
    dims = (((1,), (1,)), ((), ())) if w_is_transposed else (((1,), (0,)), ((), ()))
    return lax.dot_general(a, w, dims, preferred_element_type=F32)


def _proj_kernel(*refs, w_is_transposed, q_tiles, q_scale, with_cast):
    if with_cast:
        h_ref, hs_ref, w_ref, cast_in_ref, o_ref, os_ref, cast_out_ref, wb_ref = refs
        cast_out_ref[...] = cast_in_ref[...].astype(BF16)
    else:
        h_ref, hs_ref, w_ref, o_ref, os_ref, wb_ref = refs

    @pl.when(pl.program_id(1) == 0)
    def _():
        wb_ref[...] = w_ref[...].astype(BF16)
        os_ref[...] = _matmul(hs_ref[...], wb_ref[...], w_is_transposed)

    acc = _matmul(h_ref[...], wb_ref[...], w_is_transposed)
    if q_tiles:
        acc = acc * jnp.where(pl.program_id(0) < q_tiles, q_scale, 1.0)
    o_ref[...] = acc.astype(o_ref.dtype)


def _proj(h, hs, w_stack, layer, col0, n_cols, tm, tn, out_dtype, q_tiles=0, q_scale=1.0, w_is_transposed=False,
          cast_job=None):
    m, d = h.shape
    ms = hs.shape[0]
    t0 = col0 // tn
    m_tiles = m // tm
    if w_is_transposed:
        w_spec = pl.BlockSpec((None, tn, d), lambda j, i: (layer, t0 + j, 0))
        w_tile = (tn, d)
    else:
        w_spec = pl.BlockSpec((None, d, tn), lambda j, i: (layer, 0, t0 + j))
        w_tile = (d, tn)
    in_specs = [pl.BlockSpec((tm, d), lambda j, i: (i, 0)),
                pl.BlockSpec((ms, d), lambda j, i: (0, 0)),
                w_spec]
    args = [h, hs, w_stack]
    out_specs = [pl.BlockSpec((tm, tn), lambda j, i: (i, j)),
                 pl.BlockSpec((ms, tn), lambda j, i: (0, j))]
    out_shape = [jax.ShapeDtypeStruct((m, n_cols), out_dtype), jax.ShapeDtypeStruct((ms, n_cols), F32)]
    if cast_job is not None:
        w2_stack, layer2 = cast_job
        _, k2, n2 = w2_stack.shape
        slabs = n2 // LANES
        assert (n_cols // tn) * m_tiles >= slabs
        slab = lambda j, i: jnp.minimum(j * m_tiles + i, slabs - 1)
        in_specs.append(pl.BlockSpec((None, k2, LANES), lambda j, i: (layer2, 0, slab(j, i))))
        args.append(w2_stack)
        out_specs.append(pl.BlockSpec((k2, LANES), lambda j, i: (0, slab(j, i))))
        out_shape.append(jax.ShapeDtypeStruct((k2, n2), BF16))
    return pl.pallas_call(
        functools.partial(_proj_kernel, w_is_transposed=w_is_transposed, q_tiles=q_tiles, q_scale=q_scale,
                          with_cast=cast_job is not None),
        grid=(n_cols // tn, m_tiles),
        in_specs=in_specs,
        out_specs=out_specs,
        out_shape=out_shape,
        scratch_shapes=[pltpu.VMEM(w_tile, BF16)],
        compiler_params=_params(("arbitrary", "arbitrary")),
        name="in_proj",
    )(*args)


def _post_norm_residual(y, x, gate, g):
    return x + gate * (y * lax.rsqrt(jnp.mean(y * y, axis=-1, keepdims=True) + NORM_EPS) * g)


def _out_kernel(*refs, emit_next, emit_gates):
    a_ref, as_ref, w_ref, x_ref, xs_ref = refs[:5]
    mod_p, mod_s = refs[5:8], refs[8:11]
    g_ref, gn_ref = refs[11:13]
    wg_ref = refs[13] if emit_gates else None
    outs = refs[13 + emit_gates:]
    pick = lambda k: outs[k] if len(outs) > k else None

    def finish(a_in, x_in, mod, x_out, h_out, g_out):
        per_row = mod[0].shape[0] > 1
        for r in range(0, a_in.shape[0], OUT_PROJ_SUB_ROWS):
            rows = slice(r, r + OUT_PROJ_SUB_ROWS)
            gate, shift, scale = [m[rows, :] if per_row else m[...] for m in mod]
            y = jnp.dot(a_in[rows, :], w_ref[...], preferred_element_type=F32)
            x_new = _post_norm_residual(y, x_in[rows, :], gate, g_ref[...])
            x_out[rows, :] = x_new
            if emit_next:
                h_out[rows, :] = _pre_norm_mod(x_new, shift, scale, gn_ref[...]).astype(h_out.dtype)
        if emit_gates:
            g_out[...] = _matmul(h_out[...], wg_ref[...].astype(BF16), True)

    @pl.when(pl.program_id(0) == 0)
    def _():
        finish(as_ref, xs_ref, mod_s, outs[1], pick(3), pick(5))

    finish(a_ref, x_ref, mod_p, outs[0], pick(2), pick(4))


def _out_proj(a, a_s, w, x, xs, mod_p, mod_s, layer, g_post, g_pre_next, tm, emit_next, gate_w=None):
    m, d_in = a.shape
    ms = a_s.shape[0]
    d = x.shape[1]
    nxt = min(layer + 1, mod_p.shape[0] - 1)
    tiles_per_seq = m // tm // mod_p.shape[1]
    once = pl.Buffered(1)
    row_spec = lambda width: pl.BlockSpec((tm, width), lambda i: (i, 0))
    fixed_spec = lambda width: pl.BlockSpec((ms, width), lambda i: (0, 0))
    p_spec = lambda l, part: pl.BlockSpec((None, None, 1, d), lambda i: (l, i // tiles_per_seq, 0, part))
    s_spec = lambda l, part: pl.BlockSpec((None, ms, d), lambda i: (l, 0, part), pipeline_mode=once)
    shift, scale, gate = 0, 1, 2
    in_specs = [row_spec(d_in),
                pl.BlockSpec((ms, d_in), lambda i: (0, 0), pipeline_mode=once),
                pl.BlockSpec((d_in, d), lambda i: (0, 0), pipeline_mode=once),
                row_spec(d),
                pl.BlockSpec((ms, d), lambda i: (0, 0), pipeline_mode=once),
                p_spec(layer, gate), p_spec(nxt, shift), p_spec(nxt, scale),
                s_spec(layer, gate), s_spec(nxt, shift), s_spec(nxt, scale),
                pl.BlockSpec((1, d), lambda i: (0, 0)),
                pl.BlockSpec((1, d), lambda i: (0, 0))]
    args = [a, a_s, w, x, xs, mod_p, mod_p, mod_p, mod_s, mod_s, mod_s,
            g_post.reshape(1, d), g_pre_next.reshape(1, d)]
    out_specs = [row_spec(d), fixed_spec(d)]
    out_shape = [jax.ShapeDtypeStruct((m, d), F32), jax.ShapeDtypeStruct((ms, d), F32)]
    if emit_next:
        out_specs += [row_spec(d), fixed_spec(d)]
        out_shape += [jax.ShapeDtypeStruct((m, d), BF16), jax.ShapeDtypeStruct((ms, d), BF16)]
    if gate_w is not None:
        wt_stack, g_layer, row0, n = gate_w
        in_specs.append(pl.BlockSpec((None, n, d), lambda i: (g_layer, row0 // n, 0)))
        args.append(wt_stack)
        out_specs += [row_spec(n), fixed_spec(n)]
        out_shape += [jax.ShapeDtypeStruct((m, n), F32), jax.ShapeDtypeStruct((ms, n), F32)]
    return pl.pallas_call(
        functools.partial(_out_kernel, emit_next=emit_next, emit_gates=gate_w is not None),
        grid=(m // tm,),
        in_specs=in_specs,
        out_specs=out_specs,
        out_shape=out_shape,
        compiler_params=_params(("arbitrary",)),
        name="out_proj_postnorm",
    )(*args)


def _group_select(group_col, values):
    out = values[ATT_GROUP - 1]
    for g in range(ATT_GROUP - 2, -1, -1):
        out = jnp.where(group_col == g, values[g], out)
    return out


def _attn_prompt_kernel(sink_ref, q_ref, kc_ref, kp_ref, vc_ref, vp_ref, z0_ref, z1_ref, o_ref, bias_ref):
    blk = WINDOW
    hd = ATT_HEAD_DIM
    i = pl.program_id(1)
    cols = ATT_GROUP * blk

    @pl.when(i <= 1)
    def _():
        key = lax.broadcasted_iota(jnp.int32, (2 * blk, cols), 0)
        qcol = lax.broadcasted_iota(jnp.int32, (2 * blk, cols), 1)
        dist = (qcol % blk) + blk - key
        valid = (dist >= 0) & (dist < WINDOW) & ((key >= blk) | (i > 0))
        dist_f = dist.astype(F32)
        group = qcol // blk
        for h in range(ATT_KV_HEADS):
            slope = _group_select(group, [_alibi_slope(ATT_GROUP * h + g) for g in range(ATT_GROUP)])
            bias_ref[h] = jnp.where(valid, (-slope * dist_f) * LOG2E, -jnp.inf)

    kcat = jnp.concatenate([kp_ref[...], kc_ref[...]], axis=0)
    vcat = jnp.concatenate([vp_ref[...], vc_ref[...]], axis=0)
    half = z0_ref.shape[1]
    vt_pairs = [jnp.concatenate([vcat[:blk, c * LANES:(c + 1) * LANES].T, vcat[blk:, c * LANES:(c + 1) * LANES].T],
                                axis=1) for c in range(ATT_KV_WIDTH // LANES)]
    zero = jnp.zeros((hd, blk), BF16)
    for h in range(ATT_KV_HEADS):
        pair, odd = divmod(h, 2)
        k_pair = kcat[:, pair * LANES:(pair + 1) * LANES]
        vt = vt_pairs[pair][hd:] if odd else vt_pairs[pair][:hd]
        lhs = jnp.concatenate([vt, jnp.ones_like(vt)], axis=0)
        for t in range(2):
            c0 = (2 * h + t) * LANES
            qt_pair = q_ref[:, c0:c0 + LANES].T
            outs = []
            for e in range(2):
                g = 2 * t + e
                qt = qt_pair[e * hd:(e + 1) * hd]
                rhs = jnp.concatenate([zero, qt] if odd else [qt, zero], axis=0)
                s = jnp.dot(k_pair, rhs, preferred_element_type=F32) + bias_ref[h, :, g * blk:(g + 1) * blk]
                sink = sink_ref[ATT_GROUP * h + g] * LOG2E
                mx = jnp.maximum(jnp.max(s, axis=0, keepdims=True), sink)
                p = jnp.exp2(s - mx).astype(BF16)
                oa = jnp.dot(lhs, p, preferred_element_type=F32)
                den = oa[hd:hd + 1] + jnp.exp2(sink - mx)
                outs.append(oa[:hd] * (1.0 / den))
            ot = jnp.concatenate(outs, axis=0).T
            z_ref, zc = (z0_ref, c0) if c0 < half else (z1_ref, c0 - half)
            o_ref[:, c0:c0 + LANES] = (ot * _silu(z_ref[:, zc:zc + LANES])).astype(o_ref.dtype)


def _attn_prompt(qkv, z, sinks, batch, seq):
    blk = WINDOW
    nb = seq // blk
    inner = ATT_Q_HEADS * ATT_HEAD_DIM
    kvw = ATT_KV_WIDTH
    k_col = inner // kvw
    v_col = k_col + 1
    half = inner // 2
    cur = lambda b, i: b * nb + i
    prev = lambda b, i: b * nb + jnp.maximum(i - 1, 0)
    return pl.pallas_call(
        _attn_prompt_kernel,
        grid=(batch, nb),
        in_specs=[pl.BlockSpec(memory_space=pltpu.SMEM),
                  pl.BlockSpec((blk, inner), lambda b, i: (cur(b, i), 0)),
                  pl.BlockSpec((blk, kvw), lambda b, i: (cur(b, i), k_col)),
                  pl.BlockSpec((blk, kvw), lambda b, i: (prev(b, i), k_col)),
                  pl.BlockSpec((blk, kvw), lambda b, i: (cur(b, i), v_col)),
                  pl.BlockSpec((blk, kvw), lambda b, i: (prev(b, i), v_col)),
                  pl.BlockSpec((blk, half), lambda b, i: (cur(b, i), 0)),
                  pl.BlockSpec((blk, half), lambda b, i: (cur(b, i), 1))],
        out_specs=pl.BlockSpec((blk, inner), lambda b, i: (cur(b, i), 0)),
        out_shape=jax.ShapeDtypeStruct((batch * seq, inner), BF16),
        scratch_shapes=[pltpu.VMEM((ATT_KV_HEADS, 2 * blk, ATT_GROUP * blk), F32)],
        compiler_params=_params(("arbitrary", "arbitrary")),
        name="attn_prompt",
    )(sinks, qkv, qkv, qkv, qkv, qkv, z, z)


def _attn_sample_kernel(*refs, n_new, has_acc):
    sink_ref, bias_ref = refs[0], refs[-1]
    seq_refs = refs[1:8] + refs[8 + 2 * has_acc:-1]
    n_seq, tp = refs[1].shape[:2]
    w = refs[6].shape[3]
    rows = ATT_GROUP * tp

    @pl.when(pl.program_id(0) == 0)
    def _():
        t_row = lax.broadcasted_iota(jnp.int32, (rows, 2 * w), 0) % tp
        key = lax.broadcasted_iota(jnp.int32, (rows, 2 * w), 1)
        dist = t_row + w - key
        valid = (dist >= 0) & (dist < WINDOW) & (key < w + n_new)
        dist_f = dist.astype(F32)
        group = lax.broadcasted_iota(jnp.int32, (rows, 2 * w), 0) // tp
        for h in range(ATT_KV_HEADS):
            slope = _group_select(group, [_alibi_slope(ATT_GROUP * h + g) for g in range(ATT_GROUP)])
            bias_ref[h] = jnp.where(valid, -slope * dist_f, -jnp.inf)

    for sq in range(n_seq):
        _attn_sample_seq(sink_ref, bias_ref, *[r.at[sq] for r in seq_refs], n_new=n_new)


def _attn_sample_seq(sink_ref, bias_ref, q_ref, kn_ref, vn_ref, z0_ref, z1_ref, kc_ref, vc_ref, o_ref, ko_ref, vo_ref,
                     *, n_new):
    hd = ATT_HEAD_DIM
    tp = q_ref.shape[0]
    w = kc_ref.shape[2]
    rows = ATT_GROUP * tp
    nt_dims = (((1,), (1,)), ((), ()))
    pad = jnp.zeros((w - tp, kn_ref.shape[1]), F32)
    kn_pad = jnp.concatenate([kn_ref[...], pad], axis=0)
    vn_pad = jnp.concatenate([vn_ref[...], pad], axis=0)
    kn_b = kn_pad.astype(BF16)
    vn_b = vn_pad.astype(BF16)
    n_pairs = ATT_KV_WIDTH // LANES
    knt_pairs = [kn_pad[:, c * LANES:(c + 1) * LANES].T for c in range(n_pairs)]
    vnt_pairs = [vn_pad[:, c * LANES:(c + 1) * LANES].T for c in range(n_pairs)]
    is_new = lax.broadcasted_iota(jnp.int32, (hd, w), 1) < n_new
    group_col = lax.broadcasted_iota(jnp.int32, (rows, 1), 0) // tp
    half = z0_ref.shape[1]
    hs = range(ATT_KV_HEADS)
    heads = [[ATT_GROUP * h + g for g in range(ATT_GROUP)] for h in hs]
    kts = [kc_ref[h] for h in hs]
    vts = [vc_ref[h] for h in hs]
    qss = [(jnp.concatenate([q_ref[:, j * hd:(j + 1) * hd] for j in heads[h]], axis=0) * (hd ** -0.5)).astype(BF16)
           for h in hs]
    ss = [jnp.concatenate([jnp.dot(qss[h], kts[h].astype(BF16), preferred_element_type=F32),
                           lax.dot_general(qss[h], kn_b[:, h * hd:(h + 1) * hd], nt_dims,
                                           preferred_element_type=F32)], axis=1) + bias_ref[h] for h in hs]
    sinks = [_group_select(group_col, [sink_ref[j] for j in heads[h]]) for h in hs]
    mxs = [jnp.maximum(jnp.max(ss[h], axis=-1, keepdims=True), sinks[h]) for h in hs]
    ps = [jnp.exp(ss[h] - mxs[h]) for h in hs]
    dens = [jnp.sum(ps[h], axis=-1, keepdims=True) + jnp.exp(sinks[h] - mxs[h]) for h in hs]
    pbs = [p.astype(BF16) for p in ps]
    outs = [(lax.dot_general(pbs[h][:, :w], vts[h].astype(BF16), nt_dims, preferred_element_type=F32)
             + jnp.dot(pbs[h][:, w:], vn_b[:, h * hd:(h + 1) * hd], preferred_element_type=F32)) * (1.0 / dens[h])
            for h in hs]
    for h in hs:
        for g, j in enumerate(heads[h]):
            c0 = j * hd
            z_ref, zc = (z0_ref, c0) if c0 < half else (z1_ref, c0 - half)
            z = z_ref[:, zc:zc + hd]
            o_ref[:, c0:c0 + hd] = (outs[h][g * tp:(g + 1) * tp] * _silu(z)).astype(o_ref.dtype)
    for h in hs:
        pair, odd = divmod(h, 2)
        knt = knt_pairs[pair][odd * hd:(odd + 1) * hd]
        vnt = vnt_pairs[pair][odd * hd:(odd + 1) * hd]
        ko_ref[h] = pltpu.roll(jnp.where(is_new, knt, kts[h]), w - n_new, axis=1)
        vo_ref[h] = pltpu.roll(jnp.where(is_new, vnt, vts[h]), w - n_new, axis=1)


def _attn_sample(qkv3, z3, sinks, cache_kt, cache_vt, layer, n_new, k_acc, v_acc):
    nbatch, tp, _ = z3.shape
    w = cache_kt.shape[4]
    inner = ATT_Q_HEADS * ATT_HEAD_DIM
    kvw = ATT_KV_WIDTH
    k_col = inner // kvw
    half = inner // 2
    ns = ATTN_SAMPLE_SEQS_PER_STEP
    cache_spec = pl.BlockSpec((None, ns, ATT_KV_HEADS, ATT_HEAD_DIM, w), lambda b: (layer, b, 0, 0, 0))
    has_acc = k_acc is not None
    in_specs = [pl.BlockSpec(memory_space=pltpu.SMEM),
                pl.BlockSpec((ns, tp, inner), lambda b: (b, 0, 0)),
                pl.BlockSpec((ns, tp, kvw), lambda b: (b, 0, k_col)),
                pl.BlockSpec((ns, tp, kvw), lambda b: (b, 0, k_col + 1)),
                pl.BlockSpec((ns, tp, half), lambda b: (b, 0, 0)),
                pl.BlockSpec((ns, tp, half), lambda b: (b, 0, 1)),
                cache_spec, cache_spec]
    args = [sinks, qkv3, qkv3, qkv3, z3, z3, cache_kt, cache_vt]
    if has_acc:
        in_specs += [pl.BlockSpec(memory_space=pl.ANY)] * 2
        args += [k_acc, v_acc]
    return pl.pallas_call(
        functools.partial(_attn_sample_kernel, n_new=n_new, has_acc=has_acc),
        grid=(nbatch // ns,),
        in_specs=in_specs,
        out_specs=[pl.BlockSpec((ns, tp, inner), lambda b: (b, 0, 0)), cache_spec, cache_spec],
        out_shape=[jax.ShapeDtypeStruct((nbatch, tp, inner), BF16),
                   jax.ShapeDtypeStruct(cache_kt.shape, F32), jax.ShapeDtypeStruct(cache_vt.shape, F32)],
        scratch_shapes=[pltpu.VMEM((ATT_KV_HEADS, ATT_GROUP * tp, 2 * w), F32)],
        input_output_aliases={len(args) - 2: 1, len(args) - 1: 2} if has_acc else {},
        compiler_params=_params(("arbitrary",)),
        name="attn_sample",
    )(*args)


def _lane_scan(x, op, lane):
    n = x.shape[-1]
    shift = 1
    while shift < n:
        x = jnp.where(lane >= shift, op(x, pltpu.roll(x, shift, axis=x.ndim - 1)), x)
        shift *= 2
    return x


def _gate_prep_kernel(bias_ref, g_ref, a_ref, em_ref, wt_ref, u_ref, w_ref, mn_ref, *, chunks_per_seq):
    rows, length = g_ref.shape[2], g_ref.shape[3]
    lane = lax.broadcasted_iota(jnp.int32, (rows, length), 1)
    chunk = lax.broadcasted_iota(jnp.int32, (rows, length), 0) % chunks_per_seq
    for h in range(M_HEADS):
        li = g_ref[0, h] + bias_ref[h]
        lf = _log_sigmoid(g_ref[1, h] + bias_ref[M_HEADS + h])
        bcum = _lane_scan(lf, jnp.add, lane)
        w = li - bcum
        cmax = _lane_scan(w, jnp.maximum, lane)
        e = jnp.broadcast_to(bcum[:, length - 1:length], (rows, length))
        y = e + jnp.broadcast_to(cmax[:, length - 1:length], (rows, length))
        shift = 1
        while shift < chunks_per_seq:
            e_prev = pltpu.roll(e, shift, axis=0)
            y_prev = pltpu.roll(y, shift, axis=0)
            take = chunk >= shift
            y = jnp.where(take, jnp.maximum(y_prev + e, y), y)
            e = jnp.where(take, e_prev + e, e)
            shift *= 2
        m_incl = jnp.maximum(e, y)
        m_prev = jnp.where(chunk >= 1, pltpu.roll(m_incl, 1, axis=0), 0.0)
        u = -jnp.maximum(m_prev, cmax)
        u_last = jnp.broadcast_to(u[:, length - 1:length], (rows, length))
        a_ref[h] = jnp.exp(m_prev + u)
        em_ref[h] = jnp.exp(u - bcum)
        wt_ref[h] = jnp.exp(w + u_last)
        u_ref[h] = u
        w_ref[h] = w
        mn_ref[h] = m_incl


def _gate_prep(gates_t, b_if, chunks_per_seq):
    shape = gates_t.shape[1:]
    out = jax.ShapeDtypeStruct(shape, F32)
    return pl.pallas_call(
        functools.partial(_gate_prep_kernel, chunks_per_seq=chunks_per_seq),
        in_specs=[pl.BlockSpec(memory_space=pltpu.SMEM), pl.BlockSpec(memory_space=pltpu.VMEM)],
        out_specs=[pl.BlockSpec(memory_space=pltpu.VMEM)] * 6,
        out_shape=[out] * 6,
        name="mlstm_gate_prep",
    )(b_if, gates_t)


def _head_norm_gate(hout, g_row, o, z):
    hn = hout * lax.rsqrt(jnp.mean(hout * hout, axis=-1, keepdims=True) + NORM_EPS) * g_row
    return hn * (z / ((1.0 + jnp.exp2(o * -LOG2E)) * (1.0 + jnp.exp2(z * -LOG2E))))


def _mlstm_prompt_kernel(q_ref, k_ref, v_ref, o_ref, z_ref, col_ref, row_ref, gh_ref,
                         out_ref, s_out_ref, n_out_ref, sn_ref):
    c = pl.program_id(1)
    n_sub, _, length = row_ref.shape
    hs = M_HEADS

    @pl.when(c == 0)
    def _():
        sn_ref[...] = jnp.zeros_like(sn_ref)

    t_idx = lax.broadcasted_iota(jnp.int32, (length, length), 0)
    s_idx = lax.broadcasted_iota(jnp.int32, (length, length), 1)
    causal = s_idx <= t_idx
    ones = jnp.ones((length, LANES), BF16)
    nt_dims = (((1,), (1,)), ((), ()))
    for sub in range(n_sub):
        r = slice(sub * length, (sub + 1) * length)
        col = col_ref[r, :]
        for h in range(hs):
            a_b = jnp.broadcast_to(col[:, h:h + 1], (length, LANES))
            e_b = jnp.broadcast_to(col[:, hs + h:hs + h + 1], (length, LANES))
            u_b = jnp.broadcast_to(col[:, 2 * hs + h:2 * hs + h + 1], (length, length))
            a_last = a_b[length - 1:length, :1]
            qb = q_ref[r, h * M_DK:(h + 1) * M_DK]
            kb = k_ref[r, h * M_DK:(h + 1) * M_DK]
            v1 = jnp.concatenate([v_ref[r, h * M_DV:(h + 1) * M_DV], ones], axis=1)
            dmat = jnp.where(causal, jnp.exp(u_b + row_ref[sub, h:h + 1, :]), 0.0)
            qk = lax.dot_general(qb, kb, nt_dims, preferred_element_type=F32) * dmat
            sn_old = sn_ref[h]
            inter = jnp.dot(qb, sn_old.astype(BF16), preferred_element_type=F32)
            intra = jnp.dot(qk.astype(BF16), v1, preferred_element_type=F32)
            a_b3 = jnp.concatenate([a_b] * (1 + M_DV // LANES), axis=1)
            tot = a_b3 * inter + intra
            inv = 1.0 / jnp.maximum(jnp.abs(tot[:, M_DV:]), e_b)
            hout = tot[:, :M_DV] * jnp.concatenate([inv] * (M_DV // LANES), axis=1)
            kwt = (kb.astype(F32).T * row_ref[sub, hs + h:hs + h + 1, :]).astype(BF16)
            sn_ref[h] = a_last * sn_old + jnp.dot(kwt, v1, preferred_element_type=F32)
            sl = slice(h * M_DV, (h + 1) * M_DV)
            out_ref[r, sl] = _head_norm_gate(hout, gh_ref[:, sl], o_ref[r, sl], z_ref[r, sl]).astype(out_ref.dtype)

    @pl.when(c == pl.num_programs(1) - 1)
    def _():
        for h in range(hs):
            s_out_ref[h] = sn_ref[h, :, :M_DV]
            n_out_ref[h:h + 1, :] = sn_ref[h, :, M_DV:].T[:1, :]


def _mlstm_prompt(qkv, oz, col, rows, g_head, batch, seq):
    n_sub = MLSTM_CHUNKS_PER_STEP
    length = M_CHUNK
    span = n_sub * length
    nc = seq // span
    inner = M_HEADS * M_DV
    qkw = M_QK_WIDTH
    v_col = 2 * qkw // inner
    row = lambda b, c: b * nc + c
    return pl.pallas_call(
        _mlstm_prompt_kernel,
        grid=(batch, nc),
        in_specs=[pl.BlockSpec((span, qkw), lambda b, c: (row(b, c), 0)),
                  pl.BlockSpec((span, qkw), lambda b, c: (row(b, c), 1)),
                  pl.BlockSpec((span, inner), lambda b, c: (row(b, c), v_col)),
                  pl.BlockSpec((span, inner), lambda b, c: (row(b, c), 0)),
                  pl.BlockSpec((span, inner), lambda b, c: (row(b, c), 1)),
                  pl.BlockSpec((span, 3 * M_HEADS), lambda b, c: (row(b, c), 0)),
                  pl.BlockSpec((n_sub, 2 * M_HEADS, length), lambda b, c: (row(b, c), 0, 0)),
                  pl.BlockSpec((1, inner), lambda b, c: (0, 0))],
        out_specs=[pl.BlockSpec((span, inner), lambda b, c: (row(b, c), 0)),
                   pl.BlockSpec((None, M_HEADS, M_DK, M_DV), lambda b, c: (b, 0, 0, 0)),
                   pl.BlockSpec((None, M_HEADS, M_DK), lambda b, c: (b, 0, 0))],
        out_shape=[jax.ShapeDtypeStruct((batch * seq, inner), BF16),
                   jax.ShapeDtypeStruct((batch, M_HEADS, M_DK, M_DV), F32),
                   jax.ShapeDtypeStruct((batch, M_HEADS, M_DK), F32)],
        scratch_shapes=[pltpu.VMEM((M_HEADS, M_DK, M_DV + LANES), F32)],
        compiler_params=_params(("arbitrary", "arbitrary")),
        name="mlstm_prompt_scan",
    )(qkv, qkv, qkv, oz, oz, col, rows, g_head.reshape(1, inner))


def _prefix_scan(x, op, axis, n):
    idx = lax.broadcasted_iota(jnp.int32, x.shape, axis)
    take = (lambda t: x[t:t + 1, :]) if axis == 0 else (lambda t: x[:, t:t + 1])
    run = take(0)
    out = jnp.broadcast_to(run, x.shape)
    for t in range(1, n):
        run = op(run, take(t))
        out = jnp.where(idx >= t, run, out)
    return out


def _mlstm_sample_kernel(*refs, n_new, has_acc):
    ins, outs = refs[:13], refs[13 + has_acc:]
    shared = (7, 8, 10)
    for sq in range(ins[0].shape[0]):
        seq_ins = [r if i in shared else r.at[sq] for i, r in enumerate(ins)]
        _mlstm_sample_seq(*seq_ins, *[r.at[sq] for r in outs], n_new=n_new)


def _mlstm_sample_seq(q_ref, k_ref, v_ref, o_ref, z_ref, g_ref, gt_ref, brow_ref, bcol_ref, mrow_ref, gh_ref,
                      s_in_ref, n_in_ref, out_ref, s_out_ref, n_out_ref, m_out_ref, *, n_new):
    tp = q_ref.shape[0]
    last = n_new - 1
    hs = M_HEADS
    g = g_ref[...] + brow_ref[...]
    bcum_c = _prefix_scan(_log_sigmoid(g[:, hs:]), jnp.add, 0, n_new)
    w_c = g[:, :hs] - bcum_c
    m_prev_c = mrow_ref[...]
    u_c = -jnp.maximum(m_prev_c, _prefix_scan(w_c, jnp.maximum, 0, n_new))
    a_c = jnp.exp(m_prev_c + u_c)
    e_c = jnp.exp(u_c - bcum_c)
    real_c = lax.broadcasted_iota(jnp.int32, (tp, hs), 0) < n_new
    wt_c = jnp.where(real_c, jnp.exp(w_c + u_c[last:last + 1, :]), 0.0)
    m_out_ref[...] = bcum_c[last:last + 1, :] - u_c[last:last + 1, :]
    gt = gt_ref[...] + bcol_ref[...]
    w_r = gt[:hs, :] - _prefix_scan(_log_sigmoid(gt[hs:, :]), jnp.add, 1, n_new)
    keys = w_r.shape[1]
    t_idx = lax.broadcasted_iota(jnp.int32, (tp, keys), 0)
    s_idx = lax.broadcasted_iota(jnp.int32, (tp, keys), 1)
    causal = s_idx <= t_idx
    k_pad = jnp.concatenate([k_ref[...], jnp.zeros((keys - tp, k_ref.shape[1]), F32)], axis=0)
    v_pad = jnp.concatenate([v_ref[...], jnp.zeros((keys - tp, v_ref.shape[1]), F32)], axis=0).astype(BF16)
    wt_pad = jnp.concatenate([wt_c, jnp.zeros((keys - tp, hs), F32)], axis=0)
    for h in range(hs):
        a_col = a_c[:, h:h + 1]
        a_last = a_col[last:last + 1, :]
        qf = q_ref[:, h * M_DK:(h + 1) * M_DK] * (M_DK ** -0.5)
        qb = qf.astype(BF16)
        kf = k_pad[:, h * M_DK:(h + 1) * M_DK]
        vb = v_pad[:, h * M_DV:(h + 1) * M_DV]
        dmat = jnp.where(causal, jnp.exp(u_c[:, h:h + 1] + w_r[h:h + 1, :]), 0.0)
        qk = lax.dot_general(qb, kf.astype(BF16), (((1,), (1,)), ((), ())), preferred_element_type=F32) * dmat
        s_old = s_in_ref[h]
        n_old = n_in_ref[h:h + 1, :]
        num = a_col * jnp.dot(qb, s_old.astype(BF16), preferred_element_type=F32) \
            + jnp.dot(qk.astype(BF16), vb, preferred_element_type=F32)
        den = a_col * jnp.sum(qf * n_old, axis=-1, keepdims=True) + jnp.sum(qk, axis=-1, keepdims=True)
        hout = num / jnp.maximum(jnp.abs(den), e_c[:, h:h + 1])
        kw = kf * wt_pad[:, h:h + 1]
        s_out_ref[h] = a_last * s_old + jnp.dot(kw.T.astype(BF16), vb, preferred_element_type=F32)
        n_out_ref[h:h + 1, :] = a_last * n_old + jnp.sum(kw, axis=0, keepdims=True)
        sl = slice(h * M_DV, (h + 1) * M_DV)
        out_ref[:, sl] = _head_norm_gate(hout, gh_ref[:, sl], o_ref[:, sl], z_ref[:, sl]).astype(out_ref.dtype)


def _mlstm_sample(qkv3, oz3, gates3, gates3_t, b_if, state_c, state_n, state_m, g_head, layer, n_new, c_acc):
    nbatch, tp, _ = qkv3.shape
    hs = M_HEADS
    inner = hs * M_DV
    qkw = M_QK_WIDTH
    v_col = 2 * qkw // inner
    keys = gates3_t.shape[2]
    m_row = state_m[layer].reshape(nbatch, 1, hs)
    has_acc = c_acc is not None
    ns = MLSTM_SAMPLE_SEQS_PER_STEP
    in_specs = [pl.BlockSpec((ns, tp, qkw), lambda b: (b, 0, 0)),
                pl.BlockSpec((ns, tp, qkw), lambda b: (b, 0, 1)),
                pl.BlockSpec((ns, tp, inner), lambda b: (b, 0, v_col)),
                pl.BlockSpec((ns, tp, inner), lambda b: (b, 0, 0)),
                pl.BlockSpec((ns, tp, inner), lambda b: (b, 0, 1)),
                pl.BlockSpec((ns, tp, 2 * hs), lambda b: (b, 0, 0)),
                pl.BlockSpec((ns, 2 * hs, keys), lambda b: (b, 0, 0)),
                pl.BlockSpec((1, 2 * hs), lambda b: (0, 0)),
                pl.BlockSpec((2 * hs, 1), lambda b: (0, 0)),
                pl.BlockSpec((ns, 1, hs), lambda b: (b, 0, 0)),
                pl.BlockSpec((1, inner), lambda b: (0, 0)),
                pl.BlockSpec((None, ns, hs, M_DK, M_DV), lambda b: (layer, b, 0, 0, 0)),
                pl.BlockSpec((None, ns, hs, M_DK), lambda b: (layer, b, 0, 0))]
    args = [qkv3, qkv3, qkv3, oz3, oz3, gates3, gates3_t, b_if.reshape(1, 2 * hs), b_if.reshape(2 * hs, 1),
            m_row, g_head.reshape(1, inner), state_c, state_n]
    if has_acc:
        in_specs.append(pl.BlockSpec(memory_space=pl.ANY))
        args.append(c_acc)
    return pl.pallas_call(
        functools.partial(_mlstm_sample_kernel, n_new=n_new, has_acc=has_acc),
        grid=(nbatch // ns,),
        in_specs=in_specs,
        out_specs=[pl.BlockSpec((ns, tp, inner), lambda b: (b, 0, 0)),
                   pl.BlockSpec((None, ns, hs, M_DK, M_DV), lambda b: (layer, b, 0, 0, 0)),
                   pl.BlockSpec((ns, hs, M_DK), lambda b: (b, 0, 0)),
                   pl.BlockSpec((ns, 1, hs), lambda b: (b, 0, 0))],
        out_shape=[jax.ShapeDtypeStruct((nbatch, tp, inner), BF16),
                   jax.ShapeDtypeStruct(state_c.shape, F32),
                   jax.ShapeDtypeStruct((nbatch, hs, M_DK), F32),
                   jax.ShapeDtypeStruct((nbatch, 1, hs), F32)],
        input_output_aliases={len(args) - 1: 1} if has_acc else {},
        compiler_params=_params(("arbitrary",)),
        name="mlstm_sample_step",
    )(*args)


def kernel(x_prompt, x_sample, c_prompt, c_sample, cache_k, cache_v, state_C, state_n, state_m, w_ada, b_ada,
           g_pre, g_post, w_in_attn, sinks, w_out_attn, w_in_mlstm, b_if_mlstm, g_head_mlstm, w_out_mlstm):
    batch, seq, d = x_prompt.shape
    dec_batch, dec_seq, _ = x_sample.shape
    depth = w_ada.shape[0]
    tp = SAMPLE_ROWS
    rows_p = batch * seq
    rows_s = dec_batch * tp
    tm, tn = PROJ_ROWS, PROJ_COLS
    att_inner = ATT_Q_HEADS * ATT_HEAD_DIM
    att_qkv = att_inner + 2 * ATT_KV_WIDTH
    m_qkv = 2 * M_QK_WIDTH + M_HEADS * M_DV
    m_cols = m_qkv + 2 * M_HEADS * M_DV

    c_rows = batch + dec_batch
    c_pad = -c_rows % 8
    c_all = jnp.concatenate([c_prompt, c_sample, jnp.zeros((c_pad, d), F32)], axis=0)
    mod = _ada_all_layers(c_all, w_ada, b_ada)

    xp = x_prompt.reshape(rows_p, d)
    xs = jnp.pad(x_sample, ((0, 0), (0, tp - dec_seq), (0, 0))).reshape(rows_s, d)
    cache_kt = cache_k.transpose(0, 1, 3, 4, 2)
    cache_vt = cache_v.transpose(0, 1, 3, 4, 2)
    w_in_mlstm_t = w_in_mlstm.transpose(0, 2, 1)

    mod_p = mod[:, :batch].reshape(depth, batch, 1, 3 * d)
    mod_s = jnp.repeat(mod[:, batch:c_rows], tp, axis=1)

    hp = _prenorm(xp, mod_p[0, :, :, :d], mod_p[0, :, :, d:2 * d], g_pre[0], PRENORM_ROWS)
    hs = _prenorm(xs, mod_s[:1, :, :d], mod_s[:1, :, d:2 * d], g_pre[0], rows_s)
    gp = gs = None
    kp_l, vp_l = [], []
    cp_l, np_l, mp_l, ns_l, ms_l = [], [], [], [], []
    c_sample_new = k_sample_new = v_sample_new = None
    for l in range(depth):
        j = l // 2
        if l % 2 == 0:
            h_tail = hp.reshape(batch, seq, d)[:, seq - WINDOW:].reshape(batch * WINDOW, d)
            qkv_p, qkv_s = _proj(hp, jnp.concatenate([hs, h_tail], axis=0), w_in_attn, j, 0, att_qkv, tm, tn, BF16,
                                 q_tiles=att_inner // tn, q_scale=ATT_HEAD_DIM ** -0.5 * LOG2E)
            z_p, z_s, w_out = _proj(hp, hs, w_in_attn, j, att_qkv, att_inner, tm, tn, F32,
                                    cast_job=(w_out_attn, j))
            ap = _attn_prompt(qkv_p, z_p, sinks[j], batch, seq)
            qkv_s3 = qkv_s.reshape(-1, tp, att_qkv)
            a_s, k_sample_new, v_sample_new = _attn_sample(qkv_s3, z_s.reshape(dec_batch, tp, att_inner), sinks[j],
                                                           cache_kt, cache_vt, j, dec_seq, k_sample_new, v_sample_new)
            a_s = a_s.reshape(rows_s, att_inner)
            k0, v0 = att_inner, att_inner + ATT_KV_WIDTH
            kv_tail = qkv_s[rows_s:].reshape(batch, WINDOW, att_qkv)
            kv_shape = (ATT_KV_HEADS, ATT_HEAD_DIM)
            kp_l.append(kv_tail[:, :, k0:v0].reshape((batch, WINDOW) + kv_shape))
            vp_l.append(kv_tail[:, :, v0:v0 + ATT_KV_WIDTH].reshape((batch, WINDOW) + kv_shape))
        else:
            qkv_p, qkv_s = _proj(hp, hs, w_in_mlstm_t, j, 0, m_qkv, tm, tn, BF16, q_tiles=M_QK_WIDTH // tn,
                                 q_scale=M_DK ** -0.5, w_is_transposed=True)
            oz_p, oz_s, w_out = _proj(hp, hs, w_in_mlstm_t, j, m_qkv, m_cols - m_qkv, tm, tn, F32,
                                      w_is_transposed=True, cast_job=(w_out_mlstm, j))
            nc = seq // M_CHUNK
            gates_t = gp.T.reshape(2, M_HEADS, batch * nc, M_CHUNK)
            a_q, em_q, wt_q, u_q, w_q, mn_q = _gate_prep(gates_t, b_if_mlstm[j], nc)
            col = jnp.stack([a_q, em_q, u_q]).transpose(2, 3, 0, 1).reshape(rows_p, 3 * M_HEADS)
            rows = jnp.concatenate([w_q, wt_q], axis=0).transpose(1, 0, 2)
            ap, c_new, n_new = _mlstm_prompt(qkv_p, oz_p, col, rows, g_head_mlstm[j], batch, seq)
            cp_l.append(c_new)
            np_l.append(n_new)
            mp_l.append(mn_q.reshape(M_HEADS, batch, nc, M_CHUNK)[:, :, nc - 1, 0].T)
            gs3 = gs.reshape(dec_batch, tp, 2 * M_HEADS)
            gs3_t = jnp.pad(gs3.transpose(0, 2, 1), ((0, 0), (0, 0), (0, LANES - tp)))
            a_s, c_sample_new, n_new, m_new = _mlstm_sample(
                qkv_s.reshape(dec_batch, tp, m_qkv), oz_s.reshape(dec_batch, tp, m_cols - m_qkv), gs3, gs3_t,
                b_if_mlstm[j], state_C, state_n, state_m, g_head_mlstm[j], j, dec_seq, c_sample_new)
            a_s = a_s.reshape(rows_s, M_HEADS * M_DV)
            ns_l.append(n_new)
            ms_l.append(m_new.reshape(dec_batch, M_HEADS))
        nxt = min(l + 1, depth - 1)
        next_is_mlstm = l + 1 < depth and (l + 1) % 2 == 1
        gate_w = (w_in_mlstm_t, (l + 1) // 2, m_cols, 2 * M_HEADS) if next_is_mlstm else None
        outs = _out_proj(ap, a_s, w_out, xp, xs, mod_p, mod_s, l, g_post[l], g_pre[nxt], OUT_PROJ_ROWS,
                         emit_next=l + 1 < depth, gate_w=gate_w)
        xp, xs = outs[:2]
        if l + 1 < depth:
            hp, hs = outs[2:4]
        if next_is_mlstm:
            gp, gs = outs[4:]

    y_prompt = xp.reshape(batch, seq, d)
    y_sample = xs.reshape(dec_batch, tp, d)[:, :dec_seq]
    to_cache_layout = lambda c: c.transpose(0, 1, 4, 2, 3)
    return (y_prompt, y_sample, jnp.stack(kp_l), jnp.stack(vp_l), to_cache_layout(k_sample_new),
            to_cache_layout(v_sample_new), jnp.stack(cp_l), jnp.stack(np_l), jnp.stack(mp_l), c_sample_new,
            jnp.stack(ns_l), jnp.stack(ms_l))
```

```python
import functools

import jax
import jax.numpy as jnp
from jax import lax
from jax.experimental import pallas as pl
from jax.experimental.pallas import tpu as pltpu

F32 = jnp.float32
BF16 = jnp.bfloat16

NORM_EPS = 1e-6
WINDOW = 128
ATT_HEAD_DIM = 64
ATT_KV_HEADS = 8
ATT_GROUP = 4
ATT_Q_HEADS = ATT_KV_HEADS * ATT_GROUP
ATT_KV_WIDTH = ATT_KV_HEADS * ATT_HEAD_DIM
M_HEADS = 8
M_DK = 128
M_DV = 256
M_QK_WIDTH = M_HEADS * M_DK
M_CHUNK = 256
SAMPLE_ROWS = 8
LANES = 128
LOG2E = 1.4426950408889634

PROJ_ROWS, PROJ_COLS = 1024, 1024
OUT_PROJ_ROWS = 512
OUT_PROJ_SUB_ROWS = 128
ADA_COLS = 1024
PRENORM_ROWS = 1024
ATTN_SAMPLE_SEQS_PER_STEP = 2
MLSTM_SAMPLE_SEQS_PER_STEP = 4
MLSTM_CHUNKS_PER_STEP = 2
V7X_VMEM_LIMIT = 56 * 1024 * 1024


def _params(sem, vmem=V7X_VMEM_LIMIT):
    return pltpu.CompilerParams(dimension_semantics=sem, vmem_limit_bytes=vmem)


def _sigmoid(x):
    return 1.0 / (1.0 + jnp.exp2(x * -LOG2E))


def _silu(x):
    return x * _sigmoid(x)


def _log_sigmoid(x):
    return jnp.minimum(x, 0.0) - jnp.log1p(jnp.exp(-jnp.abs(x)))


def _alibi_slope(head):
    return float(2.0 ** (-8.0 * (head + 1) / ATT_Q_HEADS))


def _ada_kernel(c_ref, w_ref, b_ref, o_ref):
    s = _silu(c_ref[...]).astype(BF16)
    o_ref[...] = jnp.dot(s, w_ref[...].astype(BF16), preferred_element_type=F32) + b_ref[...]


def _ada_all_layers(c_all, w_ada, b_ada, tn=ADA_COLS):
    depth, d, n = w_ada.shape
    r = c_all.shape[0]
    return pl.pallas_call(
        _ada_kernel,
        grid=(depth, n // tn),
        in_specs=[
            pl.BlockSpec((r, d), lambda l, j: (0, 0)),
            pl.BlockSpec((None, d, tn), lambda l, j: (l, 0, j)),
            pl.BlockSpec((None, 1, tn), lambda l, j: (l, 0, j)),
        ],
        out_specs=pl.BlockSpec((None, r, tn), lambda l, j: (l, 0, j)),
        out_shape=jax.ShapeDtypeStruct((depth, r, n), F32),
        compiler_params=_params(("arbitrary", "arbitrary")),
        name="adaln_mod",
    )(c_all, w_ada, b_ada.reshape(depth, 1, n))


def _pre_norm_mod(x, shift, scale, g):
    y = x * lax.rsqrt(jnp.mean(x * x, axis=-1, keepdims=True) + NORM_EPS) * g
    return y * (1.0 + scale) + shift


def _prenorm_kernel(x_ref, sh_ref, sc_ref, g_ref, h_ref):
    h_ref[...] = _pre_norm_mod(x_ref[...], sh_ref[...], sc_ref[...], g_ref[...]).astype(h_ref.dtype)


def _prenorm(x, shift, scale, g, tm):
    m, d = x.shape
    groups, r, _ = shift.shape
    tiles_per_group = m // tm // groups
    mod_spec = pl.BlockSpec((None, r, d), lambda i: (i // tiles_per_group, 0, 0))
    return pl.pallas_call(
        _prenorm_kernel,
        grid=(m // tm,),
        in_specs=[pl.BlockSpec((tm, d), lambda i: (i, 0)), mod_spec, mod_spec,
                  pl.BlockSpec((1, d), lambda i: (0, 0))],
        out_specs=pl.BlockSpec((tm, d), lambda i: (i, 0)),
        out_shape=jax.ShapeDtypeStruct((m, d), BF16),
        compiler_params=_params(("arbitrary",)),
        name="prenorm_mod",
    )(x, shift, scale, g.reshape(1, d))


def _matmul(a, w, w_is_transposed):
    dims = (((1,), (1,)), ((), ())) if w_is_transposed else (((1,), (0,)), ((), ()))
    return lax.dot_general(a, w, dims, preferred_element_type=F32)


def _proj_kernel(*refs, w_is_transposed, q_tiles, q_scale, with_cast):
    if with_cast:
        h_ref, hs_ref, w_ref, cast_in_ref, o_ref, os_ref, cast_out_ref, wb_ref = refs
        cast_out_ref[...] = cast_in_ref[...].astype(BF16)
    else:
        h_ref, hs_ref, w_ref, o_ref, os_ref, wb_ref = refs

    @pl.when(pl.program_id(1) == 0)
    def _():
        wb_ref[...] = w_ref[...].astype(BF16)
        os_ref[...] = _matmul(hs_ref[...], wb_ref[...], w_is_transposed)

    acc = _matmul(h_ref[...], wb_ref[...], w_is_transposed)
    if q_tiles:
        acc = acc * jnp.where(pl.program_id(0) < q_tiles, q_scale, 1.0)
    o_ref[...] = acc.astype(o_ref.dtype)


def _proj(h, hs, w_stack, layer, col0, n_cols, tm, tn, out_dtype, q_tiles=0, q_scale=1.0, w_is_transposed=False,
          cast_job=None):
    m, d = h.shape
    ms = hs.shape[0]
    t0 = col0 // tn
    m_tiles = m // tm
    if w_is_transposed:
        w_spec = pl.BlockSpec((None, tn, d), lambda j, i: (layer, t0 + j, 0))
        w_tile = (tn, d)
    else:
        w_spec = pl.BlockSpec((None, d, tn), lambda j, i: (layer, 0, t0 + j))
        w_tile = (d, tn)
    in_specs = [pl.BlockSpec((tm, d), lambda j, i: (i, 0)),
                pl.BlockSpec((ms, d), lambda j, i: (0, 0)),
                w_spec]
    args = [h, hs, w_stack]
    out_specs = [pl.BlockSpec((tm, tn), lambda j, i: (i, j)),
                 pl.BlockSpec((ms, tn), lambda j, i: (0, j))]
    out_shape = [jax.ShapeDtypeStruct((m, n_cols), out_dtype), jax.ShapeDtypeStruct((ms, n_cols), F32)]
    if cast_job is not None:
        w2_stack, layer2 = cast_job
        _, k2, n2 = w2_stack.shape
        slabs = n2 // LANES
        assert (n_cols // tn) * m_tiles >= slabs
        slab = lambda j, i: jnp.minimum(j * m_tiles + i, slabs - 1)
        in_specs.append(pl.BlockSpec((None, k2, LANES), lambda j, i: (layer2, 0, slab(j, i))))
        args.append(w2_stack)
        out_specs.append(pl.BlockSpec((k2, LANES), lambda j, i: (0, slab(j, i))))
        out_shape.append(jax.ShapeDtypeStruct((k2, n2), BF16))
    return pl.pallas_call(
        functools.partial(_proj_kernel, w_is_transposed=w_is_transposed, q_tiles=q_tiles, q_scale=q_scale,
                          with_cast=cast_job is not None),
        grid=(n_cols // tn, m_tiles),
        in_specs=in_specs,
        out_specs=out_specs,
        out_shape=out_shape,
        scratch_shapes=[pltpu.VMEM(w_tile, BF16)],
        compiler_params=_params(("arbitrary", "arbitrary")),
        name="in_proj",
    )(*args)


def _post_norm_residual(y, x, gate, g):
    return x + gate * (y * lax.rsqrt(jnp.mean(y * y, axis=-1, keepdims=True) + NORM_EPS) * g)


def _out_kernel(*refs, emit_next, emit_gates):
    a_ref, as_ref, w_ref, x_ref, xs_ref = refs[:5]
    mod_p, mod_s = refs[5:8], refs[8:11]
    g_ref, gn_ref = refs[11:13]
    wg_ref = refs[13] if emit_gates else None
    outs = refs[13 + emit_gates:]
    pick = lambda k: outs[k] if len(outs) > k else None

    def finish(a_in, x_in, mod, x_out, h_out, g_out):
        per_row = mod[0].shape[0] > 1
        for r in range(0, a_in.shape[0], OUT_PROJ_SUB_ROWS):
            rows = slice(r, r + OUT_PROJ_SUB_ROWS)
            gate, shift, scale = [m[rows, :] if per_row else m[...] for m in mod]
            y = jnp.dot(a_in[rows, :], w_ref[...], preferred_element_type=F32)
            x_new = _post_norm_residual(y, x_in[rows, :], gate, g_ref[...])
            x_out[rows, :] = x_new
            if emit_next:
                h_out[rows, :] = _pre_norm_mod(x_new, shift, scale, gn_ref[...]).astype(h_out.dtype)
        if emit_gates:
            g_out[...] = _matmul(h_out[...], wg_ref[...].astype(BF16), True)

    @pl.when(pl.program_id(0) == 0)
    def _():
        finish(as_ref, xs_ref, mod_s, outs[1], pick(3), pick(5))

    finish(a_ref, x_ref, mod_p, outs[0], pick(2), pick(4))


def _out_proj(a, a_s, w, x, xs, mod_p, mod_s, layer, g_post, g_pre_next, tm, emit_next, gate_w=None):
    m, d_in = a.shape
    ms = a_s.shape[0]
    d = x.shape[1]
    nxt = min(layer + 1, mod_p.shape[0] - 1)
    tiles_per_seq = m // tm // mod_p.shape[1]
    once = pl.Buffered(1)
    row_spec = lambda width: pl.BlockSpec((tm, width), lambda i: (i, 0))
    fixed_spec = lambda width: pl.BlockSpec((ms, width), lambda i: (0, 0))
    p_spec = lambda l, part: pl.BlockSpec((None, None, 1, d), lambda i: (l, i // tiles_per_seq, 0, part))
    s_spec = lambda l, part: pl.BlockSpec((None, ms, d), lambda i: (l, 0, part), pipeline_mode=once)
    shift, scale, gate = 0, 1, 2
    in_specs = [row_spec(d_in),
                pl.BlockSpec((ms, d_in), lambda i: (0, 0), pipeline_mode=once),
                pl.BlockSpec((d_in, d), lambda i: (0, 0), pipeline_mode=once),
                row_spec(d),
                pl.BlockSpec((ms, d), lambda i: (0, 0), pipeline_mode=once),
                p_spec(layer, gate), p_spec(nxt, shift), p_spec(nxt, scale),
                s_spec(layer, gate), s_spec(nxt, shift), s_spec(nxt, scale),
                pl.BlockSpec((1, d), lambda i: (0, 0)),
                pl.BlockSpec((1, d), lambda i: (0, 0))]
    args = [a, a_s, w, x, xs, mod_p, mod_p, mod_p, mod_s, mod_s, mod_s,
            g_post.reshape(1, d), g_pre_next.reshape(1, d)]
    out_specs = [row_spec(d), fixed_spec(d)]
    out_shape = [jax.ShapeDtypeStruct((m, d), F32), jax.ShapeDtypeStruct((ms, d), F32)]
    if emit_next:
        out_specs += [row_spec(d), fixed_spec(d)]
        out_shape += [jax.ShapeDtypeStruct((m, d), BF16), jax.ShapeDtypeStruct((ms, d), BF16)]
    if gate_w is not None:
        wt_stack, g_layer, row0, n = gate_w
        in_specs.append(pl.BlockSpec((None, n, d), lambda i: (g_layer, row0 // n, 0)))
        args.append(wt_stack)
        out_specs += [row_spec(n), fixed_spec(n)]
        out_shape += [jax.ShapeDtypeStruct((m, n), F32), jax.ShapeDtypeStruct((ms, n), F32)]
    return pl.pallas_call(
        functools.partial(_out_kernel, emit_next=emit_next, emit_gates=gate_w is not None),
        grid=(m // tm,),
        in_specs=in_specs,
        out_specs=out_specs,
        out_shape=out_shape,
        compiler_params=_params(("arbitrary",)),
        name="out_proj_postnorm",
    )(*args)


def _group_select(group_col, values):
    out = values[ATT_GROUP - 1]
    for g in range(ATT_GROUP - 2, -1, -1):
        out = jnp.where(group_col == g, values[g], out)
    return out


def _attn_prompt_kernel(sink_ref, q_ref, kc_ref, kp_ref, vc_ref, vp_ref, z0_ref, z1_ref, o_ref, bias_ref):
    blk = WINDOW
    hd = ATT_HEAD_DIM
    i = pl.program_id(1)
    cols = ATT_GROUP * blk

    @pl.when(i <= 1)
    def _():
        key = lax.broadcasted_iota(jnp.int32, (2 * blk, cols), 0)
        qcol = lax.broadcasted_iota(jnp.int32, (2 * blk, cols), 1)
        dist = (qcol % blk) + blk - key
        valid = (dist >= 0) & (dist < WINDOW) & ((key >= blk) | (i > 0))
        dist_f = dist.astype(F32)
        group = qcol // blk
        for h in range(ATT_KV_HEADS):
            slope = _group_select(group, [_alibi_slope(ATT_GROUP * h + g) for g in range(ATT_GROUP)])
            bias_ref[h] = jnp.where(valid, (-slope * dist_f) * LOG2E, -jnp.inf)

    kcat = jnp.concatenate([kp_ref[...], kc_ref[...]], axis=0)
    vcat = jnp.concatenate([vp_ref[...], vc_ref[...]], axis=0)
    half = z0_ref.shape[1]
    vt_pairs = [jnp.concatenate([vcat[:blk, c * LANES:(c + 1) * LANES].T, vcat[blk:, c * LANES:(c + 1) * LANES].T],
                                axis=1) for c in range(ATT_KV_WIDTH // LANES)]
    zero = jnp.zeros((hd, blk), BF16)
    for h in range(ATT_KV_HEADS):
        pair, odd = divmod(h, 2)
        k_pair = kcat[:, pair * LANES:(pair + 1) * LANES]
        vt = vt_pairs[pair][hd:] if odd else vt_pairs[pair][:hd]
        lhs = jnp.concatenate([vt, jnp.ones_like(vt)], axis=0)
        for t in range(2):
            c0 = (2 * h + t) * LANES
            qt_pair = q_ref[:, c0:c0 + LANES].T
            outs = []
            for e in range(2):
                g = 2 * t + e
                qt = qt_pair[e * hd:(e + 1) * hd]
                rhs = jnp.concatenate([zero, qt] if odd else [qt, zero], axis=0)
                s = jnp.dot(k_pair, rhs, preferred_element_type=F32) + bias_ref[h, :, g * blk:(g + 1) * blk]
                sink = sink_ref[ATT_GROUP * h + g] * LOG2E
                mx = jnp.maximum(jnp.max(s, axis=0, keepdims=True), sink)
                p = jnp.exp2(s - mx).astype(BF16)
                oa = jnp.dot(lhs, p, preferred_element_type=F32)
                den = oa[hd:hd + 1] + jnp.exp2(sink - mx)
                outs.append(oa[:hd] * (1.0 / den))
            ot = jnp.concatenate(outs, axis=0).T
            z_ref, zc = (z0_ref, c0) if c0 < half else (z1_ref, c0 - half)
            o_ref[:, c0:c0 + LANES] = (ot * _silu(z_ref[:, zc:zc + LANES])).astype(o_ref.dtype)


def _attn_prompt(qkv, z, sinks, batch, seq):
    blk = WINDOW
    nb = seq // blk
    inner = ATT_Q_HEADS * ATT_HEAD_DIM
    kvw = ATT_KV_WIDTH
    k_col = inner // kvw
    v_col = k_col + 1
    half = inner // 2
    cur = lambda b, i: b * nb + i
    prev = lambda b, i: b * nb + jnp.maximum(i - 1, 0)
    return pl.pallas_call(
        _attn_prompt_kernel,
        grid=(batch, nb),
        in_specs=[pl.BlockSpec(memory_space=pltpu.SMEM),
                  pl.BlockSpec((blk, inner), lambda b, i: (cur(b, i), 0)),
                  pl.BlockSpec((blk, kvw), lambda b, i: (cur(b, i), k_col)),
                  pl.BlockSpec((blk, kvw), lambda b, i: (prev(b, i), k_col)),
                  pl.BlockSpec((blk, kvw), lambda b, i: (cur(b, i), v_col)),
                  pl.BlockSpec((blk, kvw), lambda b, i: (prev(b, i), v_col)),
                  pl.BlockSpec((blk, half), lambda b, i: (cur(b, i), 0)),
                  pl.BlockSpec((blk, half), lambda b, i: (cur(b, i), 1))],
        out_specs=pl.BlockSpec((blk, inner), lambda b, i: (cur(b, i), 0)),
        out_shape=jax.ShapeDtypeStruct((batch * seq, inner), BF16),
        scratch_shapes=[pltpu.VMEM((ATT_KV_HEADS, 2 * blk, ATT_GROUP * blk), F32)],
        compiler_params=_params(("arbitrary", "arbitrary")),
        name="attn_prompt",
    )(sinks, qkv, qkv, qkv, qkv, qkv, z, z)


def _attn_sample_kernel(*refs, n_new, has_acc):
    sink_ref, bias_ref = refs[0], refs[-1]
    seq_refs = refs[1:8] + refs[8 + 2 * has_acc:-1]
    n_seq, tp = refs[1].shape[:2]
    w = refs[6].shape[3]
    rows = ATT_GROUP * tp

    @pl.when(pl.program_id(0) == 0)
    def _():
        t_row = lax.broadcasted_iota(jnp.int32, (rows, 2 * w), 0) % tp
        key = lax.broadcasted_iota(jnp.int32, (rows, 2 * w), 1)
        dist = t_row + w - key
        valid = (dist >= 0) & (dist < WINDOW) & (key < w + n_new)
        dist_f = dist.astype(F32)
        group = lax.broadcasted_iota(jnp.int32, (rows, 2 * w), 0) // tp
        for h in range(ATT_KV_HEADS):
            slope = _group_select(group, [_alibi_slope(ATT_GROUP * h + g) for g in range(ATT_GROUP)])
            bias_ref[h] = jnp.where(valid, -slope * dist_f, -jnp.inf)

    for sq in range(n_seq):
        _attn_sample_seq(sink_ref, bias_ref, *[r.at[sq] for r in seq_refs], n_new=n_new)


def _attn_sample_seq(sink_ref, bias_ref, q_ref, kn_ref, vn_ref, z0_ref, z1_ref, kc_ref, vc_ref, o_ref, ko_ref, vo_ref,
                     *, n_new):
    hd = ATT_HEAD_DIM
    tp = q_ref.shape[0]
    w = kc_ref.shape[2]
    rows = ATT_GROUP * tp
    nt_dims = (((1,), (1,)), ((), ()))
    pad = jnp.zeros((w - tp, kn_ref.shape[1]), F32)
    kn_pad = jnp.concatenate([kn_ref[...], pad], axis=0)
    vn_pad = jnp.concatenate([vn_ref[...], pad], axis=0)
    kn_b = kn_pad.astype(BF16)
    vn_b = vn_pad.astype(BF16)
    n_pairs = ATT_KV_WIDTH // LANES
    knt_pairs = [kn_pad[:, c * LANES:(c + 1) * LANES].T for c in range(n_pairs)]
    vnt_pairs = [vn_pad[:, c * LANES:(c + 1) * LANES].T for c in range(n_pairs)]
    is_new = lax.broadcasted_iota(jnp.int32, (hd, w), 1) < n_new
    group_col = lax.broadcasted_iota(jnp.int32, (rows, 1), 0) // tp
    half = z0_ref.shape[1]
    hs = range(ATT_KV_HEADS)
    heads = [[ATT_GROUP * h + g for g in range(ATT_GROUP)] for h in hs]
    kts = [kc_ref[h] for h in hs]
    vts = [vc_ref[h] for h in hs]
    qss = [(jnp.concatenate([q_ref[:, j * hd:(j + 1) * hd] for j in heads[h]], axis=0) * (hd ** -0.5)).astype(BF16)
           for h in hs]
    ss = [jnp.concatenate([jnp.dot(qss[h], kts[h].astype(BF16), preferred_element_type=F32),
                           lax.dot_general(qss[h], kn_b[:, h * hd:(h + 1) * hd], nt_dims,
                                           preferred_element_type=F32)], axis=1) + bias_ref[h] for h in hs]
    sinks = [_group_select(group_col, [sink_ref[j] for j in heads[h]]) for h in hs]
    mxs = [jnp.maximum(jnp.max(ss[h], axis=-1, keepdims=True), sinks[h]) for h in hs]
    ps = [jnp.exp(ss[h] - mxs[h]) for h in hs]
    dens = [jnp.sum(ps[h], axis=-1, keepdims=True) + jnp.exp(sinks[h] - mxs[h]) for h in hs]
    pbs = [p.astype(BF16) for p in ps]
    outs = [(lax.dot_general(pbs[h][:, :w], vts[h].astype(BF16), nt_dims, preferred_element_type=F32)
             + jnp.dot(pbs[h][:, w:], vn_b[:, h * hd:(h + 1) * hd], preferred_element_type=F32)) * (1.0 / dens[h])
            for h in hs]
    for h in hs:
        for g, j in enumerate(heads[h]):
            c0 = j * hd
            z_ref, zc = (z0_ref, c0) if c0 < half else (z1_ref, c0 - half)
            z = z_ref[:, zc:zc + hd]
            o_ref[:, c0:c0 + hd] = (outs[h][g * tp:(g + 1) * tp] * _silu(z)).astype(o_ref.dtype)
    for h in hs:
        pair, odd = divmod(h, 2)
        knt = knt_pairs[pair][odd * hd:(odd + 1) * hd]
        vnt = vnt_pairs[pair][odd * hd:(odd + 1) * hd]
        ko_ref[h] = pltpu.roll(jnp.where(is_new, knt, kts[h]), w - n_new, axis=1)
        vo_ref[h] = pltpu.roll(jnp.where(is_new, vnt, vts[h]), w - n_new, axis=1)


def _attn_sample(qkv3, z3, sinks, cache_kt, cache_vt, layer, n_new, k_acc, v_acc):
    nbatch, tp, _ = z3.shape
    w = cache_kt.shape[4]
    inner = ATT_Q_HEADS * ATT_HEAD_DIM
    kvw = ATT_KV_WIDTH
    k_col = inner // kvw
    half = inner // 2
    ns = ATTN_SAMPLE_SEQS_PER_STEP
    cache_spec = pl.BlockSpec((None, ns, ATT_KV_HEADS, ATT_HEAD_DIM, w), lambda b: (layer, b, 0, 0, 0))
    has_acc = k_acc is not None
    in_specs = [pl.BlockSpec(memory_space=pltpu.SMEM),
                pl.BlockSpec((ns, tp, inner), lambda b: (b, 0, 0)),
                pl.BlockSpec((ns, tp, kvw), lambda b: (b, 0, k_col)),
                pl.BlockSpec((ns, tp, kvw), lambda b: (b, 0, k_col + 1)),
                pl.BlockSpec((ns, tp, half), lambda b: (b, 0, 0)),
                pl.BlockSpec((ns, tp, half), lambda b: (b, 0, 1)),
                cache_spec, cache_spec]
    args = [sinks, qkv3, qkv3, qkv3, z3, z3, cache_kt, cache_vt]
    if has_acc:
        in_specs += [pl.BlockSpec(memory_space=pl.ANY)] * 2
        args += [k_acc, v_acc]
    return pl.pallas_call(
        functools.partial(_attn_sample_kernel, n_new=n_new, has_acc=has_acc),
        grid=(nbatch // ns,),
        in_specs=in_specs,
        out_specs=[pl.BlockSpec((ns, tp, inner), lambda b: (b, 0, 0)), cache_spec, cache_spec],
        out_shape=[jax.ShapeDtypeStruct((nbatch, tp, inner), BF16),
                   jax.ShapeDtypeStruct(cache_kt.shape, F32), jax.ShapeDtypeStruct(cache_vt.shape, F32)],
        scratch_shapes=[pltpu.VMEM((ATT_KV_HEADS, ATT_GROUP * tp, 2 * w), F32)],
        input_output_aliases={len(args) - 2: 1, len(args) - 1: 2} if has_acc else {},
        compiler_params=_params(("arbitrary",)),
        name="attn_sample",
    )(*args)


def _lane_scan(x, op, lane):
    n = x.shape[-1]
    shift = 1
    while shift < n:
        x = jnp.where(lane >= shift, op(x, pltpu.roll(x, shift, axis=x.ndim - 1)), x)
        shift *= 2
    return x


def _gate_prep_kernel(bias_ref, g_ref, a_ref, em_ref, wt_ref, u_ref, w_ref, mn_ref, *, chunks_per_seq):
    rows, length = g_ref.shape[2], g_ref.shape[3]
    lane = lax.broadcasted_iota(jnp.int32, (rows, length), 1)
    chunk = lax.broadcasted_iota(jnp.int32, (rows, length), 0) % chunks_per_seq
    for h in range(M_HEADS):
        li = g_ref[0, h] + bias_ref[h]
        lf = _log_sigmoid(g_ref[1, h] + bias_ref[M_HEADS + h])
        bcum = _lane_scan(lf, jnp.add, lane)
        w = li - bcum
        cmax = _lane_scan(w, jnp.maximum, lane)
        e = jnp.broadcast_to(bcum[:, length - 1:length], (rows, length))
        y = e + jnp.broadcast_to(cmax[:, length - 1:length], (rows, length))
        shift = 1
        while shift < chunks_per_seq:
            e_prev = pltpu.roll(e, shift, axis=0)
            y_prev = pltpu.roll(y, shift, axis=0)
            take = chunk >= shift
            y = jnp.where(take, jnp.maximum(y_prev + e, y), y)
            e = jnp.where(take, e_prev + e, e)
            shift *= 2
        m_incl = jnp.maximum(e, y)
        m_prev = jnp.where(chunk >= 1, pltpu.roll(m_incl, 1, axis=0), 0.0)
        u = -jnp.maximum(m_prev, cmax)
        u_last = jnp.broadcast_to(u[:, length - 1:length], (rows, length))
        a_ref[h] = jnp.exp(m_prev + u)
        em_ref[h] = jnp.exp(u - bcum)
        wt_ref[h] = jnp.exp(w + u_last)
        u_ref[h] = u
        w_ref[h] = w
        mn_ref[h] = m_incl


def _gate_prep(gates_t, b_if, chunks_per_seq):
    shape = gates_t.shape[1:]
    out = jax.ShapeDtypeStruct(shape, F32)
    return pl.pallas_call(
        functools.partial(_gate_prep_kernel, chunks_per_seq=chunks_per_seq),
        in_specs=[pl.BlockSpec(memory_space=pltpu.SMEM), pl.BlockSpec(memory_space=pltpu.VMEM)],
        out_specs=[pl.BlockSpec(memory_space=pltpu.VMEM)] * 6,
        out_shape=[out] * 6,
        name="mlstm_gate_prep",
    )(b_if, gates_t)


def _head_norm_gate(hout, g_row, o, z):
    hn = hout * lax.rsqrt(jnp.mean(hout * hout, axis=-1, keepdims=True) + NORM_EPS) * g_row
    return hn * (z / ((1.0 + jnp.exp2(o * -LOG2E)) * (1.0 + jnp.exp2(z * -LOG2E))))


def _mlstm_prompt_kernel(q_ref, k_ref, v_ref, o_ref, z_ref, col_ref, row_ref, gh_ref,
                         out_ref, s_out_ref, n_out_ref, sn_ref):
    c = pl.program_id(1)
    n_sub, _, length = row_ref.shape
    hs = M_HEADS

    @pl.when(c == 0)
    def _():
        sn_ref[...] = jnp.zeros_like(sn_ref)

    t_idx = lax.broadcasted_iota(jnp.int32, (length, length), 0)
    s_idx = lax.broadcasted_iota(jnp.int32, (length, length), 1)
    causal = s_idx <= t_idx
    ones = jnp.ones((length, LANES), BF16)
    nt_dims = (((1,), (1,)), ((), ()))
    for sub in range(n_sub):
        r = slice(sub * length, (sub + 1) * length)
        col = col_ref[r, :]
        for h in range(hs):
            a_b = jnp.broadcast_to(col[:, h:h + 1], (length, LANES))
            e_b = jnp.broadcast_to(col[:, hs + h:hs + h + 1], (length, LANES))
            u_b = jnp.broadcast_to(col[:, 2 * hs + h:2 * hs + h + 1], (length, length))
            a_last = a_b[length - 1:length, :1]
            qb = q_ref[r, h * M_DK:(h + 1) * M_DK]
            kb = k_ref[r, h * M_DK:(h + 1) * M_DK]
            v1 = jnp.concatenate([v_ref[r, h * M_DV:(h + 1) * M_DV], ones], axis=1)
            dmat = jnp.where(causal, jnp.exp(u_b + row_ref[sub, h:h + 1, :]), 0.0)
            qk = lax.dot_general(qb, kb, nt_dims, preferred_element_type=F32) * dmat
            sn_old = sn_ref[h]
            inter = jnp.dot(qb, sn_old.astype(BF16), preferred_element_type=F32)
            intra = jnp.dot(qk.astype(BF16), v1, preferred_element_type=F32)
            a_b3 = jnp.concatenate([a_b] * (1 + M_DV // LANES), axis=1)
            tot = a_b3 * inter + intra
            inv = 1.0 / jnp.maximum(jnp.abs(tot[:, M_DV:]), e_b)
            hout = tot[:, :M_DV] * jnp.concatenate([inv] * (M_DV // LANES), axis=1)
            kwt = (kb.astype(F32).T * row_ref[sub, hs + h:hs + h + 1, :]).astype(BF16)
            sn_ref[h] = a_last * sn_old + jnp.dot(kwt, v1, preferred_element_type=F32)
            sl = slice(h * M_DV, (h + 1) * M_DV)
            out_ref[r, sl] = _head_norm_gate(hout, gh_ref[:, sl], o_ref[r, sl], z_ref[r, sl]).astype(out_ref.dtype)

    @pl.when(c == pl.num_programs(1) - 1)
    def _():
        for h in range(hs):
            s_out_ref[h] = sn_ref[h, :, :M_DV]
            n_out_ref[h:h + 1, :] = sn_ref[h, :, M_DV:].T[:1, :]


def _mlstm_prompt(qkv, oz, col, rows, g_head, batch, seq):
    n_sub = MLSTM_CHUNKS_PER_STEP
    length = M_CHUNK
    span = n_sub * length
    nc = seq // span
    inner = M_HEADS * M_DV
    qkw = M_QK_WIDTH
    v_col = 2 * qkw // inner
    row = lambda b, c: b * nc + c
    return pl.pallas_call(
        _mlstm_prompt_kernel,
        grid=(batch, nc),
        in_specs=[pl.BlockSpec((span, qkw), lambda b, c: (row(b, c), 0)),
                  pl.BlockSpec((span, qkw), lambda b, c: (row(b, c), 1)),
                  pl.BlockSpec((span, inner), lambda b, c: (row(b, c), v_col)),
                  pl.BlockSpec((span, inner), lambda b, c: (row(b, c), 0)),
                  pl.BlockSpec((span, inner), lambda b, c: (row(b, c), 1)),
                  pl.BlockSpec((span, 3 * M_HEADS), lambda b, c: (row(b, c), 0)),
                  pl.BlockSpec((n_sub, 2 * M_HEADS, length), lambda b, c: (row(b, c), 0, 0)),
                  pl.BlockSpec((1, inner), lambda b, c: (0, 0))],
        out_specs=[pl.BlockSpec((span, inner), lambda b, c: (row(b, c), 0)),
                   pl.BlockSpec((None, M_HEADS, M_DK, M_DV), lambda b, c: (b, 0, 0, 0)),
                   pl.BlockSpec((None, M_HEADS, M_DK), lambda b, c: (b, 0, 0))],
        out_shape=[jax.ShapeDtypeStruct((batch * seq, inner), BF16),
                   jax.ShapeDtypeStruct((batch, M_HEADS, M_DK, M_DV), F32),
                   jax.ShapeDtypeStruct((batch, M_HEADS, M_DK), F32)],
        scratch_shapes=[pltpu.VMEM((M_HEADS, M_DK, M_DV + LANES), F32)],
        compiler_params=_params(("arbitrary", "arbitrary")),
        name="mlstm_prompt_scan",
    )(qkv, qkv, qkv, oz, oz, col, rows, g_head.reshape(1, inner))


def _prefix_scan(x, op, axis, n):
    idx = lax.broadcasted_iota(jnp.int32, x.shape, axis)
    take = (lambda t: x[t:t + 1, :]) if axis == 0 else (lambda t: x[:, t:t + 1])
    run = take(0)
    out = jnp.broadcast_to(run, x.shape)
    for t in range(1, n):
        run = op(run, take(t))
        out = jnp.where(idx >= t, run, out)
    return out


def _mlstm_sample_kernel(*refs, n_new, has_acc):
    ins, outs = refs[:13], refs[13 + has_acc:]
    shared = (7, 8, 10)
    for sq in range(ins[0].shape[0]):
        seq_ins = [r if i in shared else r.at[sq] for i, r in enumerate(ins)]
        _mlstm_sample_seq(*seq_ins, *[r.at[sq] for r in outs], n_new=n_new)


def _mlstm_sample_seq(q_ref, k_ref, v_ref, o_ref, z_ref, g_ref, gt_ref, brow_ref, bcol_ref, mrow_ref, gh_ref,
                      s_in_ref, n_in_ref, out_ref, s_out_ref, n_out_ref, m_out_ref, *, n_new):
    tp = q_ref.shape[0]
    last = n_new - 1
    hs = M_HEADS
    g = g_ref[...] + brow_ref[...]
    bcum_c = _prefix_scan(_log_sigmoid(g[:, hs:]), jnp.add, 0, n_new)
    w_c = g[:, :hs] - bcum_c
    m_prev_c = mrow_ref[...]
    u_c = -jnp.maximum(m_prev_c, _prefix_scan(w_c, jnp.maximum, 0, n_new))
    a_c = jnp.exp(m_prev_c + u_c)
    e_c = jnp.exp(u_c - bcum_c)
    real_c = lax.broadcasted_iota(jnp.int32, (tp, hs), 0) < n_new
    wt_c = jnp.where(real_c, jnp.exp(w_c + u_c[last:last + 1, :]), 0.0)
    m_out_ref[...] = bcum_c[last:last + 1, :] - u_c[last:last + 1, :]
    gt = gt_ref[...] + bcol_ref[...]
    w_r = gt[:hs, :] - _prefix_scan(_log_sigmoid(gt[hs:, :]), jnp.add, 1, n_new)
    keys = w_r.shape[1]
    t_idx = lax.broadcasted_iota(jnp.int32, (tp, keys), 0)
    s_idx = lax.broadcasted_iota(jnp.int32, (tp, keys), 1)
    causal = s_idx <= t_idx
    k_pad = jnp.concatenate([k_ref[...], jnp.zeros((keys - tp, k_ref.shape[1]), F32)], axis=0)
    v_pad = jnp.concatenate([v_ref[...], jnp.zeros((keys - tp, v_ref.shape[1]), F32)], axis=0).astype(BF16)
    wt_pad = jnp.concatenate([wt_c, jnp.zeros((keys - tp, hs), F32)], axis=0)
    for h in range(hs):
        a_col = a_c[:, h:h + 1]
        a_last = a_col[last:last + 1, :]
        qf = q_ref[:, h * M_DK:(h + 1) * M_DK] * (M_DK ** -0.5)
        qb = qf.astype(BF16)
        kf = k_pad[:, h * M_DK:(h + 1) * M_DK]
        vb = v_pad[:, h * M_DV:(h + 1) * M_DV]
        dmat = jnp.where(causal, jnp.exp(u_c[:, h:h + 1] + w_r[h:h + 1, :]), 0.0)
        qk = lax.dot_general(qb, kf.astype(BF16), (((1,), (1,)), ((), ())), preferred_element_type=F32) * dmat
        s_old = s_in_ref[h]
        n_old = n_in_ref[h:h + 1, :]
        num = a_col * jnp.dot(qb, s_old.astype(BF16), preferred_element_type=F32) \
            + jnp.dot(qk.astype(BF16), vb, preferred_element_type=F32)
        den = a_col * jnp.sum(qf * n_old, axis=-1, keepdims=True) + jnp.sum(qk, axis=-1, keepdims=True)
        hout = num / jnp.maximum(jnp.abs(den), e_c[:, h:h + 1])
        kw = kf * wt_pad[:, h:h + 1]
        s_out_ref[h] = a_last * s_old + jnp.dot(kw.T.astype(BF16), vb, preferred_element_type=F32)
        n_out_ref[h:h + 1, :] = a_last * n_old + jnp.sum(kw, axis=0, keepdims=True)
        sl = slice(h * M_DV, (h + 1) * M_DV)
        out_ref[:, sl] = _head_norm_gate(hout, gh_ref[:, sl], o_ref[:, sl], z_ref[:, sl]).astype(out_ref.dtype)


def _mlstm_sample(qkv3, oz3, gates3, gates3_t, b_if, state_c, state_n, state_m, g_head, layer, n_new, c_acc):
    nbatch, tp, _ = qkv3.shape
    hs = M_HEADS
    inner = hs * M_DV
    qkw = M_QK_WIDTH
    v_col = 2 * qkw // inner
    keys = gates3_t.shape[2]
    m_row = state_m[layer].reshape(nbatch, 1, hs)
    has_acc = c_acc is not None
    ns = MLSTM_SAMPLE_SEQS_PER_STEP
    in_specs = [pl.BlockSpec((ns, tp, qkw), lambda b: (b, 0, 0)),
                pl.BlockSpec((ns, tp, qkw), lambda b: (b, 0, 1)),
                pl.BlockSpec((ns, tp, inner), lambda b: (b, 0, v_col)),
                pl.BlockSpec((ns, tp, inner), lambda b: (b, 0, 0)),
                pl.BlockSpec((ns, tp, inner), lambda b: (b, 0, 1)),
                pl.BlockSpec((ns, tp, 2 * hs), lambda b: (b, 0, 0)),
                pl.BlockSpec((ns, 2 * hs, keys), lambda b: (b, 0, 0)),
                pl.BlockSpec((1, 2 * hs), lambda b: (0, 0)),
                pl.BlockSpec((2 * hs, 1), lambda b: (0, 0)),
                pl.BlockSpec((ns, 1, hs), lambda b: (b, 0, 0)),
                pl.BlockSpec((1, inner), lambda b: (0, 0)),
                pl.BlockSpec((None, ns, hs, M_DK, M_DV), lambda b: (layer, b, 0, 0, 0)),
                pl.BlockSpec((None, ns, hs, M_DK), lambda b: (layer, b, 0, 0))]
    args = [qkv3, qkv3, qkv3, oz3, oz3, gates3, gates3_t, b_if.reshape(1, 2 * hs), b_if.reshape(2 * hs, 1),
            m_row, g_head.reshape(1, inner), state_c, state_n]
    if has_acc:
        in_specs.append(pl.BlockSpec(memory_space=pl.ANY))
        args.append(c_acc)
    return pl.pallas_call(
        functools.partial(_mlstm_sample_kernel, n_new=n_new, has_acc=has_acc),
        grid=(nbatch // ns,),
        in_specs=in_specs,
        out_specs=[pl.BlockSpec((ns, tp, inner), lambda b: (b, 0, 0)),
                   pl.BlockSpec((None, ns, hs, M_DK, M_DV), lambda b: (layer, b, 0, 0, 0)),
                   pl.BlockSpec((ns, hs, M_DK), lambda b: (b, 0, 0)),
                   pl.BlockSpec((ns, 1, hs), lambda b: (b, 0, 0))],
        out_shape=[jax.ShapeDtypeStruct((nbatch, tp, inner), BF16),
                   jax.ShapeDtypeStruct(state_c.shape, F32),
                   jax.ShapeDtypeStruct((nbatch, hs, M_DK), F32),
                   jax.ShapeDtypeStruct((nbatch, 1, hs), F32)],
        input_output_aliases={len(args) - 1: 1} if has_acc else {},
        compiler_params=_params(("arbitrary",)),
        name="mlstm_sample_step",
    )(*args)


def kernel(x_prompt, x_sample, c_prompt, c_sample, cache_k, cache_v, state_C, state_n, state_m, w_ada, b_ada,
           g_pre, g_post, w_in_attn, sinks, w_out_attn, w_in_mlstm, b_if_mlstm, g_head_mlstm, w_out_mlstm):
    batch, seq, d = x_prompt.shape
    dec_batch, dec_seq, _ = x_sample.shape
    depth = w_ada.shape[0]
    tp = SAMPLE_ROWS
    rows_p = batch * seq
    rows_s = dec_batch * tp
    tm, tn = PROJ_ROWS, PROJ_COLS
    att_inner = ATT_Q_HEADS * ATT_HEAD_DIM
    att_qkv = att_inner + 2 * ATT_KV_WIDTH
    m_qkv = 2 * M_QK_WIDTH + M_HEADS * M_DV
    m_cols = m_qkv + 2 * M_HEADS * M_DV

    c_rows = batch + dec_batch
    c_pad = -c_rows % 8
    c_all = jnp.concatenate([c_prompt, c_sample, jnp.zeros((c_pad, d), F32)], axis=0)
    mod = _ada_all_layers(c_all, w_ada, b_ada)

    xp = x_prompt.reshape(rows_p, d)
    xs = jnp.pad(x_sample, ((0, 0), (0, tp - dec_seq), (0, 0))).reshape(rows_s, d)
    cache_kt = cache_k.transpose(0, 1, 3, 4, 2)
    cache_vt = cache_v.transpose(0, 1, 3, 4, 2)
    w_in_mlstm_t = w_in_mlstm.transpose(0, 2, 1)

    mod_p = mod[:, :batch].reshape(depth, batch, 1, 3 * d)
    mod_s = jnp.repeat(mod[:, batch:c_rows], tp, axis=1)

    hp = _prenorm(xp, mod_p[0, :, :, :d], mod_p[0, :, :, d:2 * d], g_pre[0], PRENORM_ROWS)
    hs = _prenorm(xs, mod_s[:1, :, :d], mod_s[:1, :, d:2 * d], g_pre[0], rows_s)
    gp = gs = None
    kp_l, vp_l = [], []
    cp_l, np_l, mp_l, ns_l, ms_l = [], [], [], [], []
    c_sample_new = k_sample_new = v_sample_new = None
    for l in range(depth):
        j = l // 2
        if l % 2 == 0:
            h_tail = hp.reshape(batch, seq, d)[:, seq - WINDOW:].reshape(batch * WINDOW, d)
            qkv_p, qkv_s = _proj(hp, jnp.concatenate([hs, h_tail], axis=0), w_in_attn, j, 0, att_qkv, tm, tn, BF16,
                                 q_tiles=att_inner // tn, q_scale=ATT_HEAD_DIM ** -0.5 * LOG2E)
            z_p, z_s, w_out = _proj(hp, hs, w_in_attn, j, att_qkv, att_inner, tm, tn, F32,
                                    cast_job=(w_out_attn, j))
            ap = _attn_prompt(qkv_p, z_p, sinks[j], batch, seq)
            qkv_s3 = qkv_s.reshape(-1, tp, att_qkv)
            a_s, k_sample_new, v_sample_new = _attn_sample(qkv_s3, z_s.reshape(dec_batch, tp, att_inner), sinks[j],
                                                           cache_kt, cache_vt, j, dec_seq, k_sample_new, v_sample_new)
            a_s = a_s.reshape(rows_s, att_inner)
            k0, v0 = att_inner, att_inner + ATT_KV_WIDTH
            kv_tail = qkv_s[rows_s:].reshape(batch, WINDOW, att_qkv)
            kv_shape = (ATT_KV_HEADS, ATT_HEAD_DIM)
            kp_l.append(kv_tail[:, :, k0:v0].reshape((batch, WINDOW) + kv_shape))
            vp_l.append(kv_tail[:, :, v0:v0 + ATT_KV_WIDTH].reshape((batch, WINDOW) + kv_shape))
        else:
            qkv_p, qkv_s = _proj(hp, hs, w_in_mlstm_t, j, 0, m_qkv, tm, tn, BF16, q_tiles=M_QK_WIDTH // tn,
                                 q_scale=M_DK ** -0.5, w_is_transposed=True)
            oz_p, oz_s, w_out = _proj(hp, hs, w_in_mlstm_t, j, m_qkv, m_cols - m_qkv, tm, tn, F32,
                                      w_is_transposed=True, cast_job=(w_out_mlstm, j))
            nc = seq // M_CHUNK
            gates_t = gp.T.reshape(2, M_HEADS, batch * nc, M_CHUNK)
            a_q, em_q, wt_q, u_q, w_q, mn_q = _gate_prep(gates_t, b_if_mlstm[j], nc)
            col = jnp.stack([a_q, em_q, u_q]).transpose(2, 3, 0, 1).reshape(rows_p, 3 * M_HEADS)
            rows = jnp.concatenate([w_q, wt_q], axis=0).transpose(1, 0, 2)
            ap, c_new, n_new = _mlstm_prompt(qkv_p, oz_p, col, rows, g_head_mlstm[j], batch, seq)
            cp_l.append(c_new)
            np_l.append(n_new)
            mp_l.append(mn_q.reshape(M_HEADS, batch, nc, M_CHUNK)[:, :, nc - 1, 0].T)
            gs3 = gs.reshape(dec_batch, tp, 2 * M_HEADS)
            gs3_t = jnp.pad(gs3.transpose(0, 2, 1), ((0, 0), (0, 0), (0, LANES - tp)))
            a_s, c_sample_new, n_new, m_new = _mlstm_sample(
                qkv_s.reshape(dec_batch, tp, m_qkv), oz_s.reshape(dec_batch, tp, m_cols - m_qkv), gs3, gs3_t,
                b_if_mlstm[j], state_C, state_n, state_m, g_head_mlstm[j], j, dec_seq, c_sample_new)
            a_s = a_s.reshape(rows_s, M_HEADS * M_DV)
            ns_l.append(n_new)
            ms_l.append(m_new.reshape(dec_batch, M_HEADS))
        nxt = min(l + 1, depth - 1)
        next_is_mlstm = l + 1 < depth and (l + 1) % 2 == 1
        gate_w = (w_in_mlstm_t, (l + 1) // 2, m_cols, 2 * M_HEADS) if next_is_mlstm else None
        outs = _out_proj(ap, a_s, w_out, xp, xs, mod_p, mod_s, l, g_post[l], g_pre[nxt], OUT_PROJ_ROWS,
                         emit_next=l + 1 < depth, gate_w=gate_w)
        xp, xs = outs[:2]
        if l + 1 < depth:
            hp, hs = outs[2:4]
        if next_is_mlstm:
            gp, gs = outs[4:]

    y_prompt = xp.reshape(batch, seq, d)
    y_sample = xs.reshape(dec_batch, tp, d)[:, :dec_seq]
    to_cache_layout = lambda c: c.transpose(0, 1, 4, 2, 3)
    return (y_prompt, y_sample, jnp.stack(kp_l), jnp.stack(vp_l), to_cache_layout(k_sample_new),
            to_cache_layout(v_sample_new), jnp.stack(cp_l), jnp.stack(np_l), jnp.stack(mp_l), c_sample_new,
            jnp.stack(ns_l), jnp.stack(ms_l))
```

```python
import functools

import jax
import jax.numpy as jnp
from jax import lax
from jax.experimental import pallas as pl
from jax.experimental.pallas import tpu as pltpu

F32 = jnp.float32
BF16 = jnp.bfloat16

NORM_EPS = 1e-6
WINDOW = 128
ATT_HEAD_DIM = 64
ATT_KV_HEADS = 8
ATT_GROUP = 4
ATT_Q_HEADS = ATT_KV_HEADS * ATT_GROUP
ATT_KV_WIDTH = ATT_KV_HEADS * ATT_HEAD_DIM
M_HEADS = 8
M_DK = 128
M_DV = 256
M_QK_WIDTH = M_HEADS * M_DK
M_CHUNK = 256
SAMPLE_ROWS = 8
LANES = 128
LOG2E = 1.4426950408889634

PROJ_ROWS, PROJ_COLS = 1024, 1024
OUT_PROJ_ROWS = 512
OUT_PROJ_SUB_ROWS = 128
ADA_COLS = 2048
PRENORM_ROWS = 1024
ATTN_SAMPLE_SEQS_PER_STEP = 2
MLSTM_SAMPLE_SEQS_PER_STEP = 4
MLSTM_CHUNKS_PER_STEP = 2
V7X_VMEM_LIMIT = 56 * 1024 * 1024


def _params(sem, vmem=V7X_VMEM_LIMIT):
    return pltpu.CompilerParams(dimension_semantics=sem, vmem_limit_bytes=vmem)


def _sigmoid(x):
    return 1.0 / (1.0 + jnp.exp2(x * -LOG2E))


def _silu(x):
    return x * _sigmoid(x)


def _log_sigmoid(x):
    return jnp.minimum(x, 0.0) - jnp.log1p(jnp.exp(-jnp.abs(x)))


def _alibi_slope(head):
    return float(2.0 ** (-8.0 * (head + 1) / ATT_Q_HEADS))


def _ada_kernel(c_ref, w_ref, b_ref, o_ref):
    s = _silu(c_ref[...]).astype(BF16)
    o_ref[...] = jnp.dot(s, w_ref[...].astype(BF16), preferred_element_type=F32) + b_ref[...]


def _ada_all_layers(c_all, w_ada, b_ada, tn=ADA_COLS):
    depth, d, n = w_ada.shape
    r = c_all.shape[0]
    return pl.pallas_call(
        _ada_kernel,
        grid=(depth, n // tn),
        in_specs=[
            pl.BlockSpec((r, d), lambda l, j: (0, 0)),
            pl.BlockSpec((None, d, tn), lambda l, j: (l, 0, j)),
            pl.BlockSpec((None, 1, tn), lambda l, j: (l, 0, j)),
        ],
        out_specs=pl.BlockSpec((None, r, tn), lambda l, j: (l, 0, j)),
        out_shape=jax.ShapeDtypeStruct((depth, r, n), F32),
        compiler_params=_params(("arbitrary", "arbitrary")),
        name="adaln_mod",
    )(c_all, w_ada, b_ada.reshape(depth, 1, n))


def _pre_norm_mod(x, shift, scale, g):
    y = x * lax.rsqrt(jnp.mean(x * x, axis=-1, keepdims=True) + NORM_EPS) * g
    return y * (1.0 + scale) + shift


def _prenorm_kernel(x_ref, sh_ref, sc_ref, g_ref, h_ref):
    h_ref[...] = _pre_norm_mod(x_ref[...], sh_ref[...], sc_ref[...], g_ref[...]).astype(h_ref.dtype)


def _prenorm(x, shift, scale, g, tm):
    m, d = x.shape
    groups, r, _ = shift.shape
    tiles_per_group = m // tm // groups
    mod_spec = pl.BlockSpec((None, r, d), lambda i: (i // tiles_per_group, 0, 0))
    return pl.pallas_call(
        _prenorm_kernel,
        grid=(m // tm,),
        in_specs=[pl.BlockSpec((tm, d), lambda i: (i, 0)), mod_spec, mod_spec,
                  pl.BlockSpec((1, d), lambda i: (0, 0))],
        out_specs=pl.BlockSpec((tm, d), lambda i: (i, 0)),
        out_shape=jax.ShapeDtypeStruct((m, d), BF16),
        compiler_params=_params(("arbitrary",)),
        name="prenorm_mod",
    )(x, shift, scale, g.reshape(1, d))


def _matmul(a, w, w_is_transposed):
    dims = (((1,), (1,)), ((), ())) if w_is_transposed else (((1,), (0,)), ((), ()))
    return lax.dot_general(a, w, dims, preferred_element_type=F32)


def _proj_kernel(*refs, w_is_transposed, q_tiles, q_scale, with_cast):
    if with_cast:
        h_ref, hs_ref, w_ref, cast_in_ref, o_ref, os_ref, cast_out_ref, wb_ref = refs
        cast_out_ref[...] = cast_in_ref[...].astype(BF16)
    else:
        h_ref, hs_ref, w_ref, o_ref, os_ref, wb_ref = refs

    @pl.when(pl.program_id(1) == 0)
    def _():
        wb_ref[...] = w_ref[...].astype(BF16)
        os_ref[...] = _matmul(hs_ref[...], wb_ref[...], w_is_transposed)

    acc = _matmul(h_ref[...], wb_ref[...], w_is_transposed)
    if q_tiles:
        acc = acc * jnp.where(pl.program_id(0) < q_tiles, q_scale, 1.0)
    o_ref[...] = acc.astype(o_ref.dtype)


def _proj(h, hs, w_stack, layer, col0, n_cols, tm, tn, out_dtype, q_tiles=0, q_scale=1.0, w_is_transposed=False,
          cast_job=None):
    m, d = h.shape
    ms = hs.shape[0]
    t0 = col0 // tn
    m_tiles = m // tm
    if w_is_transposed:
        w_spec = pl.BlockSpec((None, tn, d), lambda j, i: (layer, t0 + j, 0))
        w_tile = (tn, d)
    else:
        w_spec = pl.BlockSpec((None, d, tn), lambda j, i: (layer, 0, t0 + j))
        w_tile = (d, tn)
    in_specs = [pl.BlockSpec((tm, d), lambda j, i: (i, 0)),
                pl.BlockSpec((ms, d), lambda j, i: (0, 0)),
                w_spec]
    args = [h, hs, w_stack]
    out_specs = [pl.BlockSpec((tm, tn), lambda j, i: (i, j)),
                 pl.BlockSpec((ms, tn), lambda j, i: (0, j))]
    out_shape = [jax.ShapeDtypeStruct((m, n_cols), out_dtype), jax.ShapeDtypeStruct((ms, n_cols), F32)]
    if cast_job is not None:
        w2_stack, layer2 = cast_job
        _, k2, n2 = w2_stack.shape
        slabs = n2 // LANES
        assert (n_cols // tn) * m_tiles >= slabs
        slab = lambda j, i: jnp.minimum(j * m_tiles + i, slabs - 1)
        in_specs.append(pl.BlockSpec((None, k2, LANES), lambda j, i: (layer2, 0, slab(j, i))))
        args.append(w2_stack)
        out_specs.append(pl.BlockSpec((k2, LANES), lambda j, i: (0, slab(j, i))))
        out_shape.append(jax.ShapeDtypeStruct((k2, n2), BF16))
    return pl.pallas_call(
        functools.partial(_proj_kernel, w_is_transposed=w_is_transposed, q_tiles=q_tiles, q_scale=q_scale,
                          with_cast=cast_job is not None),
        grid=(n_cols // tn, m_tiles),
        in_specs=in_specs,
        out_specs=out_specs,
        out_shape=out_shape,
        scratch_shapes=[pltpu.VMEM(w_tile, BF16)],
        compiler_params=_params(("arbitrary", "arbitrary")),
        name="in_proj",
    )(*args)


def _post_norm_residual(y, x, gate, g):
    return x + gate * (y * lax.rsqrt(jnp.mean(y * y, axis=-1, keepdims=True) + NORM_EPS) * g)


def _out_kernel(*refs, emit_next, emit_gates):
    a_ref, as_ref, w_ref, x_ref, xs_ref = refs[:5]
    mod_p, mod_s = refs[5:8], refs[8:11]
    g_ref, gn_ref = refs[11:13]
    wg_ref = refs[13] if emit_gates else None
    outs = refs[13 + emit_gates:]
    pick = lambda k: outs[k] if len(outs) > k else None

    def finish(a_in, x_in, mod, x_out, h_out, g_out):
        per_row = mod[0].shape[0] > 1
        for r in range(0, a_in.shape[0], OUT_PROJ_SUB_ROWS):
            rows = slice(r, r + OUT_PROJ_SUB_ROWS)
            gate, shift, scale = [m[rows, :] if per_row else m[...] for m in mod]
            y = jnp.dot(a_in[rows, :], w_ref[...], preferred_element_type=F32)
            x_new = _post_norm_residual(y, x_in[rows, :], gate, g_ref[...])
            x_out[rows, :] = x_new
            if emit_next:
                h_out[rows, :] = _pre_norm_mod(x_new, shift, scale, gn_ref[...]).astype(h_out.dtype)
        if emit_gates:
            g_out[...] = _matmul(h_out[...], wg_ref[...].astype(BF16), True)

    @pl.when(pl.program_id(0) == 0)
    def _():
        finish(as_ref, xs_ref, mod_s, outs[1], pick(3), pick(5))

    finish(a_ref, x_ref, mod_p, outs[0], pick(2), pick(4))


def _out_proj(a, a_s, w, x, xs, mod_p, mod_s, layer, g_post, g_pre_next, tm, emit_next, gate_w=None):
    m, d_in = a.shape
    ms = a_s.shape[0]
    d = x.shape[1]
    nxt = min(layer + 1, mod_p.shape[0] - 1)
    tiles_per_seq = m // tm // mod_p.shape[1]
    once = pl.Buffered(1)
    row_spec = lambda width: pl.BlockSpec((tm, width), lambda i: (i, 0))
    fixed_spec = lambda width: pl.BlockSpec((ms, width), lambda i: (0, 0))
    p_spec = lambda l, part: pl.BlockSpec((None, None, 1, d), lambda i: (l, i // tiles_per_seq, 0, part))
    s_spec = lambda l, part: pl.BlockSpec((None, ms, d), lambda i: (l, 0, part), pipeline_mode=once)
    shift, scale, gate = 0, 1, 2
    in_specs = [row_spec(d_in),
                pl.BlockSpec((ms, d_in), lambda i: (0, 0), pipeline_mode=once),
                pl.BlockSpec((d_in, d), lambda i: (0, 0), pipeline_mode=once),
                row_spec(d),
                pl.BlockSpec((ms, d), lambda i: (0, 0), pipeline_mode=once),
                p_spec(layer, gate), p_spec(nxt, shift), p_spec(nxt, scale),
                s_spec(layer, gate), s_spec(nxt, shift), s_spec(nxt, scale),
                pl.BlockSpec((1, d), lambda i: (0, 0)),
                pl.BlockSpec((1, d), lambda i: (0, 0))]
    args = [a, a_s, w, x, xs, mod_p, mod_p, mod_p, mod_s, mod_s, mod_s,
            g_post.reshape(1, d), g_pre_next.reshape(1, d)]
    out_specs = [row_spec(d), fixed_spec(d)]
    out_shape = [jax.ShapeDtypeStruct((m, d), F32), jax.ShapeDtypeStruct((ms, d), F32)]
    if emit_next:
        out_specs += [row_spec(d), fixed_spec(d)]
        out_shape += [jax.ShapeDtypeStruct((m, d), BF16), jax.ShapeDtypeStruct((ms, d), BF16)]
    if gate_w is not None:
        wt_stack, g_layer, row0, n = gate_w
        in_specs.append(pl.BlockSpec((None, n, d), lambda i: (g_layer, row0 // n, 0)))
        args.append(wt_stack)
        out_specs += [row_spec(n), fixed_spec(n)]
        out_shape += [jax.ShapeDtypeStruct((m, n), F32), jax.ShapeDtypeStruct((ms, n), F32)]
    return pl.pallas_call(
        functools.partial(_out_kernel, emit_next=emit_next, emit_gates=gate_w is not None),
        grid=(m // tm,),
        in_specs=in_specs,
        out_specs=out_specs,
        out_shape=out_shape,
        compiler_params=_params(("arbitrary",)),
        name="out_proj_postnorm",
    )(*args)


def _group_select(group_col, values):
    out = values[ATT_GROUP - 1]
    for g in range(ATT_GROUP - 2, -1, -1):
        out = jnp.where(group_col == g, values[g], out)
    return out


def _attn_prompt_kernel(sink_ref, q_ref, kc_ref, kp_ref, vc_ref, vp_ref, z0_ref, z1_ref, o_ref, bias_ref):
    blk = WINDOW
    hd = ATT_HEAD_DIM
    i = pl.program_id(1)
    cols = ATT_GROUP * blk

    @pl.when(i <= 1)
    def _():
        key = lax.broadcasted_iota(jnp.int32, (2 * blk, cols), 0)
        qcol = lax.broadcasted_iota(jnp.int32, (2 * blk, cols), 1)
        dist = (qcol % blk) + blk - key
        valid = (dist >= 0) & (dist < WINDOW) & ((key >= blk) | (i > 0))
        dist_f = dist.astype(F32)
        group = qcol // blk
        for h in range(ATT_KV_HEADS):
            slope = _group_select(group, [_alibi_slope(ATT_GROUP * h + g) for g in range(ATT_GROUP)])
            bias_ref[h] = jnp.where(valid, (-slope * dist_f) * LOG2E, -jnp.inf)

    kcat = jnp.concatenate([kp_ref[...], kc_ref[...]], axis=0)
    vcat = jnp.concatenate([vp_ref[...], vc_ref[...]], axis=0)
    half = z0_ref.shape[1]
    vt_pairs = [jnp.concatenate([vcat[:blk, c * LANES:(c + 1) * LANES].T, vcat[blk:, c * LANES:(c + 1) * LANES].T],
                                axis=1) for c in range(ATT_KV_WIDTH // LANES)]
    zero = jnp.zeros((hd, blk), BF16)
    for h in range(ATT_KV_HEADS):
        pair, odd = divmod(h, 2)
        k_pair = kcat[:, pair * LANES:(pair + 1) * LANES]
        vt = vt_pairs[pair][hd:] if odd else vt_pairs[pair][:hd]
        lhs = jnp.concatenate([vt, jnp.ones_like(vt)], axis=0)
        for t in range(2):
            c0 = (2 * h + t) * LANES
            qt_pair = q_ref[:, c0:c0 + LANES].T
            outs = []
            for e in range(2):
                g = 2 * t + e
                qt = qt_pair[e * hd:(e + 1) * hd]
                rhs = jnp.concatenate([zero, qt] if odd else [qt, zero], axis=0)
                s = jnp.dot(k_pair, rhs, preferred_element_type=F32) + bias_ref[h, :, g * blk:(g + 1) * blk]
                sink = sink_ref[ATT_GROUP * h + g] * LOG2E
                mx = jnp.maximum(jnp.max(s, axis=0, keepdims=True), sink)
                p = jnp.exp2(s - mx).astype(BF16)
                oa = jnp.dot(lhs, p, preferred_element_type=F32)
                den = oa[hd:hd + 1] + jnp.exp2(sink - mx)
                outs.append(oa[:hd] * (1.0 / den))
            ot = jnp.concatenate(outs, axis=0).T
            z_ref, zc = (z0_ref, c0) if c0 < half else (z1_ref, c0 - half)
            o_ref[:, c0:c0 + LANES] = (ot * _silu(z_ref[:, zc:zc + LANES])).astype(o_ref.dtype)


def _attn_prompt(qkv, z, sinks, batch, seq):
    blk = WINDOW
    nb = seq // blk
    inner = ATT_Q_HEADS * ATT_HEAD_DIM
    kvw = ATT_KV_WIDTH
    k_col = inner // kvw
    v_col = k_col + 1
    half = inner // 2
    cur = lambda b, i: b * nb + i
    prev = lambda b, i: b * nb + jnp.maximum(i - 1, 0)
    return pl.pallas_call(
        _attn_prompt_kernel,
        grid=(batch, nb),
        in_specs=[pl.BlockSpec(memory_space=pltpu.SMEM),
                  pl.BlockSpec((blk, inner), lambda b, i: (cur(b, i), 0)),
                  pl.BlockSpec((blk, kvw), lambda b, i: (cur(b, i), k_col)),
                  pl.BlockSpec((blk, kvw), lambda b, i: (prev(b, i), k_col)),
                  pl.BlockSpec((blk, kvw), lambda b, i: (cur(b, i), v_col)),
                  pl.BlockSpec((blk, kvw), lambda b, i: (prev(b, i), v_col)),
                  pl.BlockSpec((blk, half), lambda b, i: (cur(b, i), 0)),
                  pl.BlockSpec((blk, half), lambda b, i: (cur(b, i), 1))],
        out_specs=pl.BlockSpec((blk, inner), lambda b, i: (cur(b, i), 0)),
        out_shape=jax.ShapeDtypeStruct((batch * seq, inner), BF16),
        scratch_shapes=[pltpu.VMEM((ATT_KV_HEADS, 2 * blk, ATT_GROUP * blk), F32)],
        compiler_params=_params(("arbitrary", "arbitrary")),
        name="attn_prompt",
    )(sinks, qkv, qkv, qkv, qkv, qkv, z, z)


def _attn_sample_kernel(*refs, n_new, has_acc):
    sink_ref, bias_ref = refs[0], refs[-1]
    seq_refs = refs[1:8] + refs[8 + 2 * has_acc:-1]
    n_seq, tp = refs[1].shape[:2]
    w = refs[6].shape[3]
    rows = ATT_GROUP * tp

    @pl.when(pl.program_id(0) == 0)
    def _():
        t_row = lax.broadcasted_iota(jnp.int32, (rows, 2 * w), 0) % tp
        key = lax.broadcasted_iota(jnp.int32, (rows, 2 * w), 1)
        dist = t_row + w - key
        valid = (dist >= 0) & (dist < WINDOW) & (key < w + n_new)
        dist_f = dist.astype(F32)
        group = lax.broadcasted_iota(jnp.int32, (rows, 2 * w), 0) // tp
        for h in range(ATT_KV_HEADS):
            slope = _group_select(group, [_alibi_slope(ATT_GROUP * h + g) for g in range(ATT_GROUP)])
            bias_ref[h] = jnp.where(valid, -slope * dist_f, -jnp.inf)

    for sq in range(n_seq):
        _attn_sample_seq(sink_ref, bias_ref, *[r.at[sq] for r in seq_refs], n_new=n_new)


def _attn_sample_seq(sink_ref, bias_ref, q_ref, kn_ref, vn_ref, z0_ref, z1_ref, kc_ref, vc_ref, o_ref, ko_ref, vo_ref,
                     *, n_new):
    hd = ATT_HEAD_DIM
    tp = q_ref.shape[0]
    w = kc_ref.shape[2]
    rows = ATT_GROUP * tp
    nt_dims = (((1,), (1,)), ((), ()))
    pad = jnp.zeros((w - tp, kn_ref.shape[1]), F32)
    kn_pad = jnp.concatenate([kn_ref[...], pad], axis=0)
    vn_pad = jnp.concatenate([vn_ref[...], pad], axis=0)
    kn_b = kn_pad.astype(BF16)
    vn_b = vn_pad.astype(BF16)
    n_pairs = ATT_KV_WIDTH // LANES
    knt_pairs = [kn_pad[:, c * LANES:(c + 1) * LANES].T for c in range(n_pairs)]
    vnt_pairs = [vn_pad[:, c * LANES:(c + 1) * LANES].T for c in range(n_pairs)]
    is_new = lax.broadcasted_iota(jnp.int32, (hd, w), 1) < n_new
    group_col = lax.broadcasted_iota(jnp.int32, (rows, 1), 0) // tp
    half = z0_ref.shape[1]
    hs = range(ATT_KV_HEADS)
    heads = [[ATT_GROUP * h + g for g in range(ATT_GROUP)] for h in hs]
    kts = [kc_ref[h] for h in hs]
    vts = [vc_ref[h] for h in hs]
    qss = [(jnp.concatenate([q_ref[:, j * hd:(j + 1) * hd] for j in heads[h]], axis=0) * (hd ** -0.5)).astype(BF16)
           for h in hs]
    ss = [jnp.concatenate([jnp.dot(qss[h], kts[h].astype(BF16), preferred_element_type=F32),
                           lax.dot_general(qss[h], kn_b[:, h * hd:(h + 1) * hd], nt_dims,
                                           preferred_element_type=F32)], axis=1) + bias_ref[h] for h in hs]
    sinks = [_group_select(group_col, [sink_ref[j] for j in heads[h]]) for h in hs]
    mxs = [jnp.maximum(jnp.max(ss[h], axis=-1, keepdims=True), sinks[h]) for h in hs]
    ps = [jnp.exp(ss[h] - mxs[h]) for h in hs]
    dens = [jnp.sum(ps[h], axis=-1, keepdims=True) + jnp.exp(sinks[h] - mxs[h]) for h in hs]
    pbs = [p.astype(BF16) for p in ps]
    outs = [(lax.dot_general(pbs[h][:, :w], vts[h].astype(BF16), nt_dims, preferred_element_type=F32)
             + jnp.dot(pbs[h][:, w:], vn_b[:, h * hd:(h + 1) * hd], preferred_element_type=F32)) * (1.0 / dens[h])
            for h in hs]
    for h in hs:
        for g, j in enumerate(heads[h]):
            c0 = j * hd
            z_ref, zc = (z0_ref, c0) if c0 < half else (z1_ref, c0 - half)
            z = z_ref[:, zc:zc + hd]
            o_ref[:, c0:c0 + hd] = (outs[h][g * tp:(g + 1) * tp] * _silu(z)).astype(o_ref.dtype)
    for h in hs:
        pair, odd = divmod(h, 2)
        knt = knt_pairs[pair][odd * hd:(odd + 1) * hd]
        vnt = vnt_pairs[pair][odd * hd:(odd + 1) * hd]
        ko_ref[h] = pltpu.roll(jnp.where(is_new, knt, kts[h]), w - n_new, axis=1)
        vo_ref[h] = pltpu.roll(jnp.where(is_new, vnt, vts[h]), w - n_new, axis=1)


def _attn_sample(qkv3, z3, sinks, cache_kt, cache_vt, layer, n_new, k_acc, v_acc):
    nbatch, tp, _ = z3.shape
    w = cache_kt.shape[4]
    inner = ATT_Q_HEADS * ATT_HEAD_DIM
    kvw = ATT_KV_WIDTH
    k_col = inner // kvw
    half = inner // 2
    ns = ATTN_SAMPLE_SEQS_PER_STEP
    cache_spec = pl.BlockSpec((None, ns, ATT_KV_HEADS, ATT_HEAD_DIM, w), lambda b: (layer, b, 0, 0, 0))
    has_acc = k_acc is not None
    in_specs = [pl.BlockSpec(memory_space=pltpu.SMEM),
                pl.BlockSpec((ns, tp, inner), lambda b: (b, 0, 0)),
                pl.BlockSpec((ns, tp, kvw), lambda b: (b, 0, k_col)),
                pl.BlockSpec((ns, tp, kvw), lambda b: (b, 0, k_col + 1)),
                pl.BlockSpec((ns, tp, half), lambda b: (b, 0, 0)),
                pl.BlockSpec((ns, tp, half), lambda b: (b, 0, 1)),
                cache_spec, cache_spec]
    args = [sinks, qkv3, qkv3, qkv3, z3, z3, cache_kt, cache_vt]
    if has_acc:
        in_specs += [pl.BlockSpec(memory_space=pl.ANY)] * 2
        args += [k_acc, v_acc]
    return pl.pallas_call(
        functools.partial(_attn_sample_kernel, n_new=n_new, has_acc=has_acc),
        grid=(nbatch // ns,),
        in_specs=in_specs,
        out_specs=[pl.BlockSpec((ns, tp, inner), lambda b: (b, 0, 0)), cache_spec, cache_spec],
        out_shape=[jax.ShapeDtypeStruct((nbatch, tp, inner), BF16),
                   jax.ShapeDtypeStruct(cache_kt.shape, F32), jax.ShapeDtypeStruct(cache_vt.shape, F32)],
        scratch_shapes=[pltpu.VMEM((ATT_KV_HEADS, ATT_GROUP * tp, 2 * w), F32)],
        input_output_aliases={len(args) - 2: 1, len(args) - 1: 2} if has_acc else {},
        compiler_params=_params(("arbitrary",)),
        name="attn_sample",
    )(*args)


def _lane_scan(x, op, lane):
    n = x.shape[-1]
    shift = 1
    while shift < n:
        x = jnp.where(lane >= shift, op(x, pltpu.roll(x, shift, axis=x.ndim - 1)), x)
        shift *= 2
    return x


def _gate_prep_kernel(bias_ref, g_ref, a_ref, em_ref, wt_ref, u_ref, w_ref, mn_ref, *, chunks_per_seq):
    rows, length = g_ref.shape[2], g_ref.shape[3]
    lane = lax.broadcasted_iota(jnp.int32, (rows, length), 1)
    chunk = lax.broadcasted_iota(jnp.int32, (rows, length), 0) % chunks_per_seq
    for h in range(M_HEADS):
        li = g_ref[0, h] + bias_ref[h]
        lf = _log_sigmoid(g_ref[1, h] + bias_ref[M_HEADS + h])
        bcum = _lane_scan(lf, jnp.add, lane)
        w = li - bcum
        cmax = _lane_scan(w, jnp.maximum, lane)
        e = jnp.broadcast_to(bcum[:, length - 1:length], (rows, length))
        y = e + jnp.broadcast_to(cmax[:, length - 1:length], (rows, length))
        shift = 1
        while shift < chunks_per_seq:
            e_prev = pltpu.roll(e, shift, axis=0)
            y_prev = pltpu.roll(y, shift, axis=0)
            take = chunk >= shift
            y = jnp.where(take, jnp.maximum(y_prev + e, y), y)
            e = jnp.where(take, e_prev + e, e)
            shift *= 2
        m_incl = jnp.maximum(e, y)
        m_prev = jnp.where(chunk >= 1, pltpu.roll(m_incl, 1, axis=0), 0.0)
        u = -jnp.maximum(m_prev, cmax)
        u_last = jnp.broadcast_to(u[:, length - 1:length], (rows, length))
        a_ref[h] = jnp.exp(m_prev + u)
        em_ref[h] = jnp.exp(u - bcum)
        wt_ref[h] = jnp.exp(w + u_last)
        u_ref[h] = u
        w_ref[h] = w
        mn_ref[h] = m_incl


def _gate_prep(gates_t, b_if, chunks_per_seq):
    shape = gates_t.shape[1:]
    out = jax.ShapeDtypeStruct(shape, F32)
    return pl.pallas_call(
        functools.partial(_gate_prep_kernel, chunks_per_seq=chunks_per_seq),
        in_specs=[pl.BlockSpec(memory_space=pltpu.SMEM), pl.BlockSpec(memory_space=pltpu.VMEM)],
        out_specs=[pl.BlockSpec(memory_space=pltpu.VMEM)] * 6,
        out_shape=[out] * 6,
        name="mlstm_gate_prep",
    )(b_if, gates_t)


def _head_norm_gate(hout, g_row, o, z):
    hn = hout * lax.rsqrt(jnp.mean(hout * hout, axis=-1, keepdims=True) + NORM_EPS) * g_row
    return hn * (z / ((1.0 + jnp.exp2(o * -LOG2E)) * (1.0 + jnp.exp2(z * -LOG2E))))


def _mlstm_prompt_kernel(q_ref, k_ref, v_ref, o_ref, z_ref, col_ref, row_ref, gh_ref,
                         out_ref, s_out_ref, n_out_ref, sn_ref):
    c = pl.program_id(1)
    n_sub, _, length = row_ref.shape
    hs = M_HEADS

    @pl.when(c == 0)
    def _():
        sn_ref[...] = jnp.zeros_like(sn_ref)

    t_idx = lax.broadcasted_iota(jnp.int32, (length, length), 0)
    s_idx = lax.broadcasted_iota(jnp.int32, (length, length), 1)
    causal = s_idx <= t_idx
    ones = jnp.ones((length, LANES), BF16)
    nt_dims = (((1,), (1,)), ((), ()))
    for sub in range(n_sub):
        r = slice(sub * length, (sub + 1) * length)
        col = col_ref[r, :]
        for h in range(hs):
            a_b = jnp.broadcast_to(col[:, h:h + 1], (length, LANES))
            e_b = jnp.broadcast_to(col[:, hs + h:hs + h + 1], (length, LANES))
            u_b = jnp.broadcast_to(col[:, 2 * hs + h:2 * hs + h + 1], (length, length))
            a_last = a_b[length - 1:length, :1]
            qb = q_ref[r, h * M_DK:(h + 1) * M_DK]
            kb = k_ref[r, h * M_DK:(h + 1) * M_DK]
            v1 = jnp.concatenate([v_ref[r, h * M_DV:(h + 1) * M_DV], ones], axis=1)
            dmat = jnp.where(causal, jnp.exp(u_b + row_ref[sub, h:h + 1, :]), 0.0)
            qk = lax.dot_general(qb, kb, nt_dims, preferred_element_type=F32) * dmat
            sn_old = sn_ref[h]
            inter = jnp.dot(qb, sn_old.astype(BF16), preferred_element_type=F32)
            intra = jnp.dot(qk.astype(BF16), v1, preferred_element_type=F32)
            a_b3 = jnp.concatenate([a_b] * (1 + M_DV // LANES), axis=1)
            tot = a_b3 * inter + intra
            inv = 1.0 / jnp.maximum(jnp.abs(tot[:, M_DV:]), e_b)
            hout = tot[:, :M_DV] * jnp.concatenate([inv] * (M_DV // LANES), axis=1)
            kwt = (kb.astype(F32).T * row_ref[sub, hs + h:hs + h + 1, :]).astype(BF16)
            sn_ref[h] = a_last * sn_old + jnp.dot(kwt, v1, preferred_element_type=F32)
            sl = slice(h * M_DV, (h + 1) * M_DV)
            out_ref[r, sl] = _head_norm_gate(hout, gh_ref[:, sl], o_ref[r, sl], z_ref[r, sl]).astype(out_ref.dtype)

    @pl.when(c == pl.num_programs(1) - 1)
    def _():
        for h in range(hs):
            s_out_ref[h] = sn_ref[h, :, :M_DV]
            n_out_ref[h:h + 1, :] = sn_ref[h, :, M_DV:].T[:1, :]


def _mlstm_prompt(qkv, oz, col, rows, g_head, batch, seq):
    n_sub = MLSTM_CHUNKS_PER_STEP
    length = M_CHUNK
    span = n_sub * length
    nc = seq // span
    inner = M_HEADS * M_DV
    qkw = M_QK_WIDTH
    v_col = 2 * qkw // inner
    row = lambda b, c: b * nc + c
    return pl.pallas_call(
        _mlstm_prompt_kernel,
        grid=(batch, nc),
        in_specs=[pl.BlockSpec((span, qkw), lambda b, c: (row(b, c), 0)),
                  pl.BlockSpec((span, qkw), lambda b, c: (row(b, c), 1)),
                  pl.BlockSpec((span, inner), lambda b, c: (row(b, c), v_col)),
                  pl.BlockSpec((span, inner), lambda b, c: (row(b, c), 0)),
                  pl.BlockSpec((span, inner), lambda b, c: (row(b, c), 1)),
                  pl.BlockSpec((span, 3 * M_HEADS), lambda b, c: (row(b, c), 0)),
                  pl.BlockSpec((n_sub, 2 * M_HEADS, length), lambda b, c: (row(b, c), 0, 0)),
                  pl.BlockSpec((1, inner), lambda b, c: (0, 0))],
        out_specs=[pl.BlockSpec((span, inner), lambda b, c: (row(b, c), 0)),
                   pl.BlockSpec((None, M_HEADS, M_DK, M_DV), lambda b, c: (b, 0, 0, 0)),
                   pl.BlockSpec((None, M_HEADS, M_DK), lambda b, c: (b, 0, 0))],
        out_shape=[jax.ShapeDtypeStruct((batch * seq, inner), BF16),
                   jax.ShapeDtypeStruct((batch, M_HEADS, M_DK, M_DV), F32),
                   jax.ShapeDtypeStruct((batch, M_HEADS, M_DK), F32)],
        scratch_shapes=[pltpu.VMEM((M_HEADS, M_DK, M_DV + LANES), F32)],
        compiler_params=_params(("arbitrary", "arbitrary")),
        name="mlstm_prompt_scan",
    )(qkv, qkv, qkv, oz, oz, col, rows, g_head.reshape(1, inner))


def _prefix_scan(x, op, axis, n):
    idx = lax.broadcasted_iota(jnp.int32, x.shape, axis)
    take = (lambda t: x[t:t + 1, :]) if axis == 0 else (lambda t: x[:, t:t + 1])
    run = take(0)
    out = jnp.broadcast_to(run, x.shape)
    for t in range(1, n):
        run = op(run, take(t))
        out = jnp.where(idx >= t, run, out)
    return out


def _mlstm_sample_kernel(*refs, n_new, has_acc):
    ins, outs = refs[:13], refs[13 + has_acc:]
    shared = (7, 8, 10)
    for sq in range(ins[0].shape[0]):
        seq_ins = [r if i in shared else r.at[sq] for i, r in enumerate(ins)]
        _mlstm_sample_seq(*seq_ins, *[r.at[sq] for r in outs], n_new=n_new)


def _mlstm_sample_seq(q_ref, k_ref, v_ref, o_ref, z_ref, g_ref, gt_ref, brow_ref, bcol_ref, mrow_ref, gh_ref,
                      s_in_ref, n_in_ref, out_ref, s_out_ref, n_out_ref, m_out_ref, *, n_new):
    tp = q_ref.shape[0]
    last = n_new - 1
    hs = M_HEADS
    g = g_ref[...] + brow_ref[...]
    bcum_c = _prefix_scan(_log_sigmoid(g[:, hs:]), jnp.add, 0, n_new)
    w_c = g[:, :hs] - bcum_c
    m_prev_c = mrow_ref[...]
    u_c = -jnp.maximum(m_prev_c, _prefix_scan(w_c, jnp.maximum, 0, n_new))
    a_c = jnp.exp(m_prev_c + u_c)
    e_c = jnp.exp(u_c - bcum_c)
    real_c = lax.broadcasted_iota(jnp.int32, (tp, hs), 0) < n_new
    wt_c = jnp.where(real_c, jnp.exp(w_c + u_c[last:last + 1, :]), 0.0)
    m_out_ref[...] = bcum_c[last:last + 1, :] - u_c[last:last + 1, :]
    gt = gt_ref[...] + bcol_ref[...]
    w_r = gt[:hs, :] - _prefix_scan(_log_sigmoid(gt[hs:, :]), jnp.add, 1, n_new)
    keys = w_r.shape[1]
    t_idx = lax.broadcasted_iota(jnp.int32, (tp, keys), 0)
    s_idx = lax.broadcasted_iota(jnp.int32, (tp, keys), 1)
    causal = s_idx <= t_idx
    k_pad = jnp.concatenate([k_ref[...], jnp.zeros((keys - tp, k_ref.shape[1]), F32)], axis=0)
    v_pad = jnp.concatenate([v_ref[...], jnp.zeros((keys - tp, v_ref.shape[1]), F32)], axis=0).astype(BF16)
    wt_pad = jnp.concatenate([wt_c, jnp.zeros((keys - tp, hs), F32)], axis=0)
    for h in range(hs):
        a_col = a_c[:, h:h + 1]
        a_last = a_col[last:last + 1, :]
        qf = q_ref[:, h * M_DK:(h + 1) * M_DK] * (M_DK ** -0.5)
        qb = qf.astype(BF16)
        kf = k_pad[:, h * M_DK:(h + 1) * M_DK]
        vb = v_pad[:, h * M_DV:(h + 1) * M_DV]
        dmat = jnp.where(causal, jnp.exp(u_c[:, h:h + 1] + w_r[h:h + 1, :]), 0.0)
        qk = lax.dot_general(qb, kf.astype(BF16), (((1,), (1,)), ((), ())), preferred_element_type=F32) * dmat
        s_old = s_in_ref[h]
        n_old = n_in_ref[h:h + 1, :]
        num = a_col * jnp.dot(qb, s_old.astype(BF16), preferred_element_type=F32) \
            + jnp.dot(qk.astype(BF16), vb, preferred_element_type=F32)
        den = a_col * jnp.sum(qf * n_old, axis=-1, keepdims=True) + jnp.sum(qk, axis=-1, keepdims=True)
        hout = num / jnp.maximum(jnp.abs(den), e_c[:, h:h + 1])
        kw = kf * wt_pad[:, h:h + 1]
        s_out_ref[h] = a_last * s_old + jnp.dot(kw.T.astype(BF16), vb, preferred_element_type=F32)
        n_out_ref[h:h + 1, :] = a_last * n_old + jnp.sum(kw, axis=0, keepdims=True)
        sl = slice(h * M_DV, (h + 1) * M_DV)
        out_ref[:, sl] = _head_norm_gate(hout, gh_ref[:, sl], o_ref[:, sl], z_ref[:, sl]).astype(out_ref.dtype)


def _mlstm_sample(qkv3, oz3, gates3, gates3_t, b_if, state_c, state_n, state_m, g_head, layer, n_new, c_acc):
    nbatch, tp, _ = qkv3.shape
    hs = M_HEADS
    inner = hs * M_DV
    qkw = M_QK_WIDTH
    v_col = 2 * qkw // inner
    keys = gates3_t.shape[2]
    m_row = state_m[layer].reshape(nbatch, 1, hs)
    has_acc = c_acc is not None
    ns = MLSTM_SAMPLE_SEQS_PER_STEP
    in_specs = [pl.BlockSpec((ns, tp, qkw), lambda b: (b, 0, 0)),
                pl.BlockSpec((ns, tp, qkw), lambda b: (b, 0, 1)),
                pl.BlockSpec((ns, tp, inner), lambda b: (b, 0, v_col)),
                pl.BlockSpec((ns, tp, inner), lambda b: (b, 0, 0)),
                pl.BlockSpec((ns, tp, inner), lambda b: (b, 0, 1)),
                pl.BlockSpec((ns, tp, 2 * hs), lambda b: (b, 0, 0)),
                pl.BlockSpec((ns, 2 * hs, keys), lambda b: (b, 0, 0)),
                pl.BlockSpec((1, 2 * hs), lambda b: (0, 0)),
                pl.BlockSpec((2 * hs, 1), lambda b: (0, 0)),
                pl.BlockSpec((ns, 1, hs), lambda b: (b, 0, 0)),
                pl.BlockSpec((1, inner), lambda b: (0, 0)),
                pl.BlockSpec((None, ns, hs, M_DK, M_DV), lambda b: (layer, b, 0, 0, 0)),
                pl.BlockSpec((None, ns, hs, M_DK), lambda b: (layer, b, 0, 0))]
    args = [qkv3, qkv3, qkv3, oz3, oz3, gates3, gates3_t, b_if.reshape(1, 2 * hs), b_if.reshape(2 * hs, 1),
            m_row, g_head.reshape(1, inner), state_c, state_n]
    if has_acc:
        in_specs.append(pl.BlockSpec(memory_space=pl.ANY))
        args.append(c_acc)
    return pl.pallas_call(
        functools.partial(_mlstm_sample_kernel, n_new=n_new, has_acc=has_acc),
        grid=(nbatch // ns,),
        in_specs=in_specs,
        out_specs=[pl.BlockSpec((ns, tp, inner), lambda b: (b, 0, 0)),
                   pl.BlockSpec((None, ns, hs, M_DK, M_DV), lambda b: (layer, b, 0, 0, 0)),
                   pl.BlockSpec((ns, hs, M_DK), lambda b: (b, 0, 0)),
                   pl.BlockSpec((ns, 1, hs), lambda b: (b, 0, 0))],
        out_shape=[jax.ShapeDtypeStruct((nbatch, tp, inner), BF16),
                   jax.ShapeDtypeStruct(state_c.shape, F32),
                   jax.ShapeDtypeStruct((nbatch, hs, M_DK), F32),
                   jax.ShapeDtypeStruct((nbatch, 1, hs), F32)],
        input_output_aliases={len(args) - 1: 1} if has_acc else {},
        compiler_params=_params(("arbitrary",)),
        name="mlstm_sample_step",
    )(*args)


def kernel(x_prompt, x_sample, c_prompt, c_sample, cache_k, cache_v, state_C, state_n, state_m, w_ada, b_ada,
           g_pre, g_post, w_in_attn, sinks, w_out_attn, w_in_mlstm, b_if_mlstm, g_head_mlstm, w_out_mlstm):
    batch, seq, d = x_prompt.shape
    dec_batch, dec_seq, _ = x_sample.shape
    depth = w_ada.shape[0]
    tp = SAMPLE_ROWS
    rows_p = batch * seq
    rows_s = dec_batch * tp
    tm, tn = PROJ_ROWS, PROJ_COLS
    att_inner = ATT_Q_HEADS * ATT_HEAD_DIM
    att_qkv = att_inner + 2 * ATT_KV_WIDTH
    m_qkv = 2 * M_QK_WIDTH + M_HEADS * M_DV
    m_cols = m_qkv + 2 * M_HEADS * M_DV

    c_rows = batch + dec_batch
    c_pad = -c_rows % 8
    c_all = jnp.concatenate([c_prompt, c_sample, jnp.zeros((c_pad, d), F32)], axis=0)
    mod = _ada_all_layers(c_all, w_ada, b_ada)

    xp = x_prompt.reshape(rows_p, d)
    xs = jnp.pad(x_sample, ((0, 0), (0, tp - dec_seq), (0, 0))).reshape(rows_s, d)
    cache_kt = cache_k.transpose(0, 1, 3, 4, 2)
    cache_vt = cache_v.transpose(0, 1, 3, 4, 2)
    w_in_mlstm_t = w_in_mlstm.transpose(0, 2, 1)

    mod_p = mod[:, :batch].reshape(depth, batch, 1, 3 * d)
    mod_s = jnp.repeat(mod[:, batch:c_rows], tp, axis=1)

    hp = _prenorm(xp, mod_p[0, :, :, :d], mod_p[0, :, :, d:2 * d], g_pre[0], PRENORM_ROWS)
    hs = _prenorm(xs, mod_s[:1, :, :d], mod_s[:1, :, d:2 * d], g_pre[0], rows_s)
    gp = gs = None
    kp_l, vp_l = [], []
    cp_l, np_l, mp_l, ns_l, ms_l = [], [], [], [], []
    c_sample_new = k_sample_new = v_sample_new = None
    for l in range(depth):
        j = l // 2
        if l % 2 == 0:
            h_tail = hp.reshape(batch, seq, d)[:, seq - WINDOW:].reshape(batch * WINDOW, d)
            qkv_p, qkv_s = _proj(hp, jnp.concatenate([hs, h_tail], axis=0), w_in_attn, j, 0, att_qkv, tm, tn, BF16,
                                 q_tiles=att_inner // tn, q_scale=ATT_HEAD_DIM ** -0.5 * LOG2E)
            z_p, z_s, w_out = _proj(hp, hs, w_in_attn, j, att_qkv, att_inner, tm, tn, F32,
                                    cast_job=(w_out_attn, j))
            ap = _attn_prompt(qkv_p, z_p, sinks[j], batch, seq)
            qkv_s3 = qkv_s.reshape(-1, tp, att_qkv)
            a_s, k_sample_new, v_sample_new = _attn_sample(qkv_s3, z_s.reshape(dec_batch, tp, att_inner), sinks[j],
                                                           cache_kt, cache_vt, j, dec_seq, k_sample_new, v_sample_new)
            a_s = a_s.reshape(rows_s, att_inner)
            k0, v0 = att_inner, att_inner + ATT_KV_WIDTH
            kv_tail = qkv_s[rows_s:].reshape(batch, WINDOW, att_qkv)
            kv_shape = (ATT_KV_HEADS, ATT_HEAD_DIM)
            kp_l.append(kv_tail[:, :, k0:v0].reshape((batch, WINDOW) + kv_shape))
            vp_l.append(kv_tail[:, :, v0:v0 + ATT_KV_WIDTH].reshape((batch, WINDOW) + kv_shape))
        else:
            qkv_p, qkv_s = _proj(hp, hs, w_in_mlstm_t, j, 0, m_qkv, tm, tn, BF16, q_tiles=M_QK_WIDTH // tn,
                                 q_scale=M_DK ** -0.5, w_is_transposed=True)
            oz_p, oz_s, w_out = _proj(hp, hs, w_in_mlstm_t, j, m_qkv, m_cols - m_qkv, tm, tn, F32,
                                      w_is_transposed=True, cast_job=(w_out_mlstm, j))
            nc = seq // M_CHUNK
            gates_t = gp.T.reshape(2, M_HEADS, batch * nc, M_CHUNK)
            a_q, em_q, wt_q, u_q, w_q, mn_q = _gate_prep(gates_t, b_if_mlstm[j], nc)
            col = jnp.stack([a_q, em_q, u_q]).transpose(2, 3, 0, 1).reshape(rows_p, 3 * M_HEADS)
            rows = jnp.concatenate([w_q, wt_q], axis=0).transpose(1, 0, 2)
            ap, c_new, n_new = _mlstm_prompt(qkv_p, oz_p, col, rows, g_head_mlstm[j], batch, seq)
            cp_l.append(c_new)
            np_l.append(n_new)
            mp_l.append(mn_q.reshape(M_HEADS, batch, nc, M_CHUNK)[:, :, nc - 1, 0].T)
            gs3 = gs.reshape(dec_batch, tp, 2 * M_HEADS)
            gs3_t = jnp.pad(gs3.transpose(0, 2, 1), ((0, 0), (0, 0), (0, LANES - tp)))
            a_s, c_sample_new, n_new, m_new = _mlstm_sample(
                qkv_s.reshape(dec_batch, tp, m_qkv), oz_s.reshape(dec_batch, tp, m_cols - m_qkv), gs3, gs3_t,
                b_if_mlstm[j], state_C, state_n, state_m, g_head_mlstm[j], j, dec_seq, c_sample_new)
            a_s = a_s.reshape(rows_s, M_HEADS * M_DV)
            ns_l.append(n_new)
            ms_l.append(m_new.reshape(dec_batch, M_HEADS))
        nxt = min(l + 1, depth - 1)
        next_is_mlstm = l + 1 < depth and (l + 1) % 2 == 1
        gate_w = (w_in_mlstm_t, (l + 1) // 2, m_cols, 2 * M_HEADS) if next_is_mlstm else None
        outs = _out_proj(ap, a_s, w_out, xp, xs, mod_p, mod_s, l, g_post[l], g_pre[nxt], OUT_PROJ_ROWS,
                         emit_next=l + 1 < depth, gate_w=gate_w)
        xp, xs = outs[:2]
        if l + 1 < depth:
            hp, hs = outs[2:4]
        if next_is_mlstm:
            gp, gs = outs[4:]

    y_prompt = xp.reshape(batch, seq, d)
    y_sample = xs.reshape(dec_batch, tp, d)[:, :dec_seq]
    to_cache_layout = lambda c: c.transpose(0, 1, 4, 2, 3)
    return (y_prompt, y_sample, jnp.stack(kp_l), jnp.stack(vp_l), to_cache_layout(k_sample_new),
            to_cache_layout(v_sample_new), jnp.stack(cp_l), jnp.stack(np_l), jnp.stack(mp_l), c_sample_new,
            jnp.stack(ns_l), jnp.stack(ms_l))
```
